```python
import jax, jax.numpy as jnp
from jax import lax
import numpy as np

D_MODEL = 2048
BATCH = 8
SEQ = 2048
DEPTH = 1

HEAD_DIM = 64
FOX_HEADS = D_MODEL // (2 * HEAD_DIM)
SWA_Q_HEADS = D_MODEL // (2 * HEAD_DIM)
SWA_KV_HEADS = SWA_Q_HEADS // 4
D_MIX = (FOX_HEADS + SWA_Q_HEADS) * HEAD_DIM
WINDOW = 128
SWA_BLOCK = WINDOW
Q_BLOCK = 128
D_FF = 5632
ROPE_THETA = 10000.0
EPS = 1e-6

SPLIT_SIZES = (
    FOX_HEADS * HEAD_DIM,
    FOX_HEADS * HEAD_DIM,
    FOX_HEADS * HEAD_DIM,
    FOX_HEADS,
    SWA_Q_HEADS * HEAD_DIM,
    SWA_KV_HEADS * HEAD_DIM,
    SWA_KV_HEADS * HEAD_DIM,
)
D_IN_PROJ = sum(SPLIT_SIZES)
SPLIT_POINTS = tuple(int(v) for v in np.cumsum(SPLIT_SIZES)[:-1])

kernel_name = "hybrid_fox_swa_sink_macaron"


def rms_norm(x, g):
    xf = x.astype(jnp.float32)
    y = xf * lax.rsqrt(jnp.mean(xf * xf, axis=-1, keepdims=True) + EPS)
    return (y * g.astype(jnp.float32)).astype(x.dtype)


def swiglu(x, w_gate, w_up, w_down):
    return (jax.nn.silu(x @ w_gate) * (x @ w_up)) @ w_down


def rope(x, positions):
    d = x.shape[-1]
    inv_freq = ROPE_THETA ** (-jnp.arange(0, d, 2, dtype=jnp.float32) / d)
    ang = positions.astype(jnp.float32)[..., None] * inv_freq
    cos = jnp.cos(ang)[:, :, None, :]
    sin = jnp.sin(ang)[:, :, None, :]
    x1, x2 = jnp.split(x.astype(jnp.float32), 2, axis=-1)
    out = jnp.concatenate([x1 * cos - x2 * sin, x2 * cos + x1 * sin], axis=-1)
    return out.astype(x.dtype)


def forgetting_attention(q, k, v, log_f):
    B, S, H, d = q.shape
    nb = S // Q_BLOCK
    c = jnp.cumsum(log_f, axis=1).transpose(0, 2, 1)
    qh = q.transpose(0, 2, 1, 3)
    kh = k.transpose(0, 2, 1, 3)
    vh = v.transpose(0, 2, 1, 3)
    qb = qh.reshape(B, H, nb, Q_BLOCK, d).transpose(2, 0, 1, 3, 4)
    cb = c.reshape(B, H, nb, Q_BLOCK).transpose(2, 0, 1, 3)
    q_pos = jnp.arange(S).reshape(nb, Q_BLOCK)
    k_pos = jnp.arange(S)
    scale = d ** -0.5

    def block(args):
        q_i, c_i, p_i = args
        s = jnp.einsum('bhqd,bhkd->bhqk', q_i, kh).astype(jnp.float32) * scale
        s = s + c_i[..., None] - c[:, :, None, :]
        causal = p_i[:, None] >= k_pos[None, :]
        s = jnp.where(causal, s, -jnp.inf)
        p = jax.nn.softmax(s, axis=-1).astype(vh.dtype)
        return jnp.einsum('bhqk,bhkd->bhqd', p, vh)

    o = lax.map(block, (qb, cb, q_pos))
    return o.transpose(1, 0, 3, 2, 4).reshape(B, S, H * d)


def sliding_window_sink_attention(q, k, v, sinks):
    B, S, Hq, d = q.shape
    Hk = k.shape[2]
    G = Hq // Hk
    W = SWA_BLOCK
    nb = S // W
    qb = q.reshape(B, nb, W, Hk, G, d)

    def with_prev(t):
        t = t.reshape(B, nb, W, Hk, d)
        prev = jnp.pad(t, ((0, 0), (1, 0), (0, 0), (0, 0), (0, 0)))[:, :-1]
        return jnp.concatenate([prev, t], axis=2)

    kw = with_prev(k)
    vw = with_prev(v)
    s = jnp.einsum('bnqhgd,bnkhd->bnhgqk', qb, kw).astype(jnp.float32) * (d ** -0.5)
    blk = jnp.arange(nb)[:, None]
    q_abs = blk * W + jnp.arange(W)[None, :]
    k_abs = blk * W - W + jnp.arange(2 * W)[None, :]
    rel = q_abs[:, :, None] - k_abs[:, None, :]
    band = (rel >= 0) & (rel < WINDOW) & (k_abs[:, None, :] >= 0)
    s = jnp.where(band[None, :, None, None], s, -jnp.inf)
    sink = jnp.broadcast_to(sinks.astype(jnp.float32).reshape(1, 1, Hk, G, 1, 1), s.shape[:-1] + (1,))
    p = jax.nn.softmax(jnp.concatenate([s, sink], axis=-1), axis=-1)[..., :-1].astype(v.dtype)
    o = jnp.einsum('bnhgqk,bnkhd->bnqhgd', p, vw)
    return o.reshape(B, S, Hq * d)


def setup_inputs(seed: int = 0) -> dict:
    key = jax.random.key(seed)
    ks = jax.random.split(key, 24)
    f32 = jnp.float32

    def w(k, shape, fan_in):
        return jax.random.normal(k, shape, f32) * (fan_in ** -0.5)

    def gain(k, shape):
        return 1.0 + 0.02 * jax.random.normal(k, shape, f32)

    x = jax.random.normal(ks[0], (BATCH, SEQ, D_MODEL), f32)
    positions = jnp.broadcast_to(jnp.arange(SEQ, dtype=jnp.int32), (BATCH, SEQ))
    return {
        "x": x,
        "positions": positions,
        "norm_ffn1_g": gain(ks[1], (DEPTH, D_MODEL)),
        "ffn1_w_gate": w(ks[2], (DEPTH, D_MODEL, D_FF), D_MODEL),
        "ffn1_w_up": w(ks[3], (DEPTH, D_MODEL, D_FF), D_MODEL),
        "ffn1_w_down": w(ks[4], (DEPTH, D_FF, D_MODEL), D_FF),
        "norm_mix_g": gain(ks[5], (DEPTH, D_MODEL)),
        "w_in": w(ks[6], (DEPTH, D_MODEL, D_IN_PROJ), D_MODEL),
        "b_forget": 0.1 * jax.random.normal(ks[7], (DEPTH, FOX_HEADS), f32),
        "fox_q_norm_g": gain(ks[8], (DEPTH, HEAD_DIM)),
        "fox_k_norm_g": gain(ks[9], (DEPTH, HEAD_DIM)),
        "swa_q_norm_g": gain(ks[10], (DEPTH, HEAD_DIM)),
        "swa_k_norm_g": gain(ks[11], (DEPTH, HEAD_DIM)),
        "swa_sinks": 0.5 * jax.random.normal(ks[12], (DEPTH, SWA_Q_HEADS), f32),
        "out_norm_fox_g": gain(ks[13], (DEPTH, FOX_HEADS * HEAD_DIM)),
        "out_norm_swa_g": gain(ks[14], (DEPTH, SWA_Q_HEADS * HEAD_DIM)),
        "w_out": w(ks[15], (DEPTH, D_MIX, D_MODEL), D_MIX),
        "norm_ffn2_g": gain(ks[16], (DEPTH, D_MODEL)),
        "ffn2_w_gate": w(ks[17], (DEPTH, D_MODEL, D_FF), D_MODEL),
        "ffn2_w_up": w(ks[18], (DEPTH, D_MODEL, D_FF), D_MODEL),
        "ffn2_w_down": w(ks[19], (DEPTH, D_FF, D_MODEL), D_FF),
    }


def reference(x, positions, norm_ffn1_g, ffn1_w_gate, ffn1_w_up, ffn1_w_down, norm_mix_g, w_in,
              b_forget, fox_q_norm_g, fox_k_norm_g, swa_q_norm_g, swa_k_norm_g, swa_sinks,
              out_norm_fox_g, out_norm_swa_g, w_out, norm_ffn2_g, ffn2_w_gate, ffn2_w_up, ffn2_w_down):
    B, S, _ = x.shape
    h = x
    for l in range(DEPTH):
        h = h + 0.5 * swiglu(rms_norm(h, norm_ffn1_g[l]), ffn1_w_gate[l], ffn1_w_up[l], ffn1_w_down[l])

        u = rms_norm(h, norm_mix_g[l])
        proj = u @ w_in[l]
        q_f, k_f, v_f, f_logit, q_s, k_s, v_s = jnp.split(proj, SPLIT_POINTS, axis=-1)

        q_f = rms_norm(q_f.reshape(B, S, FOX_HEADS, HEAD_DIM), fox_q_norm_g[l])
        k_f = rms_norm(k_f.reshape(B, S, FOX_HEADS, HEAD_DIM), fox_k_norm_g[l])
        v_f = v_f.reshape(B, S, FOX_HEADS, HEAD_DIM)
        log_f = jax.nn.log_sigmoid((f_logit + b_forget[l]).astype(jnp.float32))
        o_fox = forgetting_attention(q_f, k_f, v_f, log_f)

        q_s = rope(rms_norm(q_s.reshape(B, S, SWA_Q_HEADS, HEAD_DIM), swa_q_norm_g[l]), positions)
        k_s = rope(rms_norm(k_s.reshape(B, S, SWA_KV_HEADS, HEAD_DIM), swa_k_norm_g[l]), positions)
        v_s = v_s.reshape(B, S, SWA_KV_HEADS, HEAD_DIM)
        o_swa = sliding_window_sink_attention(q_s, k_s, v_s, swa_sinks[l])

        o = jnp.concatenate([rms_norm(o_fox, out_norm_fox_g[l]), rms_norm(o_swa, out_norm_swa_g[l])], axis=-1)
        h = h + o @ w_out[l]

        h = h + 0.5 * swiglu(rms_norm(h, norm_ffn2_g[l]), ffn2_w_gate[l], ffn2_w_up[l], ffn2_w_down[l])
    return h
```

```python
import functools

import jax
import jax.numpy as jnp
import numpy as np
from jax import lax
from jax.experimental import pallas as pl
from jax.experimental.pallas import tpu as pltpu

F32 = jnp.float32
BF16 = jnp.bfloat16

D_MODEL = 2048
HEAD_DIM = 64
FOX_HEADS = 16
SWA_Q_HEADS = 16
SWA_KV_HEADS = 4
SWA_GROUP = SWA_Q_HEADS // SWA_KV_HEADS
WINDOW = 128
D_FF = 5632
ROPE_THETA = 10000.0
EPS = 1e-6

D_FOX = FOX_HEADS * HEAD_DIM
D_SWA_Q = SWA_Q_HEADS * HEAD_DIM
D_SWA_KV = SWA_KV_HEADS * HEAD_DIM
N_GATE = FOX_HEADS

LANES = 128
MXU_TILE = 256
VMEM_LIMIT = 56 * 1024 * 1024

C_FQ = 0
C_FK = C_FQ + D_FOX
C_FV = C_FK + D_FOX
C_SQ = C_FV + D_FOX
C_SK = C_SQ + D_SWA_Q
C_SV = C_SK + D_SWA_KV
C_GATE = C_SV + D_SWA_KV
D_IN_PAD = C_GATE + LANES

SWA_HEAD_ORDER = tuple(8 * p + 4 * side + j for p in range(2) for j in range(SWA_GROUP) for side in range(2))

FFN_TM = 512
FFN_TF = 512
ROW_TM = 512
FOX_TQ = 256
CUM_BLK = 256


def _rms_rows(x, g):
    return x * lax.rsqrt(jnp.mean(x * x, axis=-1, keepdims=True) + EPS) * g


def _ffn_kernel(x_ref, g_ref, wg_ref, wu_ref, wd_ref, o_ref, u_ref):
    j = pl.program_id(1)

    @pl.when(j == 0)
    def _():
        x = x_ref[...]
        u_ref[...] = _rms_rows(x, g_ref[...]).astype(BF16)
        o_ref[...] = x

    u = u_ref[...]
    gate = jnp.dot(u, wg_ref[...], preferred_element_type=F32)
    up = jnp.dot(u, wu_ref[...], preferred_element_type=F32)
    a = (gate * jax.nn.sigmoid(gate) * up * 0.5).astype(BF16)
    o_ref[...] += jnp.dot(a, wd_ref[...], preferred_element_type=F32)


def _ffn(x, g, wg, wu, wd):
    t, d = x.shape
    f = wg.shape[1]
    return pl.pallas_call(
        _ffn_kernel,
        grid=(t // FFN_TM, f // FFN_TF),
        in_specs=[
            pl.BlockSpec((FFN_TM, d), lambda i, j: (i, 0)),
            pl.BlockSpec((1, d), lambda i, j: (0, 0)),
            pl.BlockSpec((d, FFN_TF), lambda i, j: (0, j)),
            pl.BlockSpec((d, FFN_TF), lambda i, j: (0, j)),
            pl.BlockSpec((FFN_TF, d), lambda i, j: (j, 0)),
        ],
        out_specs=pl.BlockSpec((FFN_TM, d), lambda i, j: (i, 0)),
        out_shape=jax.ShapeDtypeStruct((t, d), F32),
        scratch_shapes=[pltpu.VMEM((FFN_TM, d), BF16)],
        compiler_params=pltpu.CompilerParams(
            dimension_semantics=("parallel", "arbitrary"), vmem_limit_bytes=VMEM_LIMIT),
        name="ffn",
    )(x, g, wg, wu, wd)


def _head_rms(y, gain, bd):
    n = y.shape[1]
    y2 = y * y
    hi = y2.astype(BF16)
    lo = (y2 - hi.astype(F32)).astype(BF16)
    parts = []
    for c in range(n // MXU_TILE):
        sl = slice(c * MXU_TILE, (c + 1) * MXU_TILE)
        parts.append(jnp.dot(hi[:, sl], bd, preferred_element_type=F32)
                     + jnp.dot(lo[:, sl], bd, preferred_element_type=F32))
    ss = parts[0] if len(parts) == 1 else jnp.concatenate(parts, axis=1)
    return y * lax.rsqrt(ss * (1.0 / HEAD_DIM) + EPS) * gain


def _rope(y, cos, sin_signed, first_half):
    outs = []
    for c in range(y.shape[1] // LANES):
        yc = y[:, c * LANES:(c + 1) * LANES]
        partner = jnp.where(first_half, pltpu.roll(yc, LANES - HEAD_DIM // 2, 1),
                            pltpu.roll(yc, HEAD_DIM // 2, 1))
        outs.append(yc * cos + partner * sin_signed)
    return outs[0] if len(outs) == 1 else jnp.concatenate(outs, axis=1)


def _inproj_kernel(h_ref, g_ref, w_ref, pos_ref, invf_ref, gfq_ref, gfk_ref, gsq_ref, gsk_ref, bf_ref,
                   qf_ref, kf_ref, vf_ref, qs_ref, ks_ref, vs_ref, lf_ref):
    u = _rms_rows(h_ref[...], g_ref[...]).astype(BF16)

    def proj(lo, hi):
        return jnp.dot(u, w_ref[:, lo:hi], preferred_element_type=F32)

    r = lax.broadcasted_iota(jnp.int32, (MXU_TILE, MXU_TILE), 0) // HEAD_DIM
    c = lax.broadcasted_iota(jnp.int32, (MXU_TILE, MXU_TILE), 1) // HEAD_DIM
    bd = jnp.where(r == c, 1.0, 0.0).astype(BF16)

    scale = HEAD_DIM ** -0.5

    qf_ref[...] = (_head_rms(proj(C_FQ, C_FK), gfq_ref[...], bd) * scale).astype(BF16)
    kf_ref[...] = _head_rms(proj(C_FK, C_FV), gfk_ref[...], bd).astype(BF16)
    vf_ref[...] = proj(C_FV, C_SQ).astype(BF16)

    ang = pos_ref[...].astype(F32) * invf_ref[...]
    lane = lax.broadcasted_iota(jnp.int32, (1, LANES), 1)
    first_half = (lane % HEAD_DIM) < (HEAD_DIM // 2)
    cos = jnp.cos(ang)
    sin = jnp.sin(ang)
    sin_signed = jnp.where(first_half, -sin, sin)

    qs = _rope(_head_rms(proj(C_SQ, C_SK), gsq_ref[...], bd), cos, sin_signed, first_half)
    qs_ref[...] = (qs * scale).astype(BF16)
    ks = _rope(_head_rms(proj(C_SK, C_SV), gsk_ref[...], bd), cos, sin_signed, first_half)
    ks_ref[...] = ks.astype(BF16)
    vs_ref[...] = proj(C_SV, C_GATE).astype(BF16)

    z = proj(C_GATE, D_IN_PAD) + bf_ref[...]
    lf_ref[...] = jnp.minimum(z, 0.0) - jnp.log1p(jnp.exp(-jnp.abs(z)))


def _inproj(h, g, w, pos, invf, gfq, gfk, gsq, gsk, bf):
    t, d = h.shape
    row = lambda n: pl.BlockSpec((ROW_TM, n), lambda i: (i, 0))
    const = lambda shape: pl.BlockSpec(shape, lambda i: (0, 0))
    return pl.pallas_call(
        _inproj_kernel,
        grid=(t // ROW_TM,),
        in_specs=[
            row(d),
            const((1, d)),
            pl.BlockSpec((d, D_IN_PAD), lambda i: (0, 0), pipeline_mode=pl.Buffered(1)),
            row(1),
            const((1, LANES)),
            const((1, D_FOX)), const((1, D_FOX)), const((1, D_SWA_Q)), const((1, D_SWA_KV)),
            const((1, LANES)),
        ],
        out_specs=[row(D_FOX), row(D_FOX), row(D_FOX), row(D_SWA_Q), row(D_SWA_KV), row(D_SWA_KV), row(LANES)],
        out_shape=[
            jax.ShapeDtypeStruct((t, D_FOX), BF16), jax.ShapeDtypeStruct((t, D_FOX), BF16),
            jax.ShapeDtypeStruct((t, D_FOX), BF16), jax.ShapeDtypeStruct((t, D_SWA_Q), BF16),
            jax.ShapeDtypeStruct((t, D_SWA_KV), BF16), jax.ShapeDtypeStruct((t, D_SWA_KV), BF16),
            jax.ShapeDtypeStruct((t, LANES), F32),
        ],
        compiler_params=pltpu.CompilerParams(dimension_semantics=("parallel",), vmem_limit_bytes=VMEM_LIMIT),
        name="inproj",
    )(h, g, w, pos, invf, gfq, gfk, gsq, gsk, bf)


def _cumsum_kernel(lf_ref, c_ref, ct_ref):
    s = lf_ref.shape[0]
    r = lax.broadcasted_iota(jnp.int32, (CUM_BLK, CUM_BLK), 0)
    c = lax.broadcasted_iota(jnp.int32, (CUM_BLK, CUM_BLK), 1)
    tri = jnp.where(r >= c, 1.0, 0.0).astype(BF16)
    carry = jnp.zeros((1, LANES), F32)
    for b in range(s // CUM_BLK):
        rows = slice(b * CUM_BLK, (b + 1) * CUM_BLK)
        x = lf_ref[rows, :]
        p1 = x.astype(BF16)
        r1 = x - p1.astype(F32)
        p2 = r1.astype(BF16)
        p3 = (r1 - p2.astype(F32)).astype(BF16)
        loc = (jnp.dot(tri, p1, preferred_element_type=F32)
               + jnp.dot(tri, p2, preferred_element_type=F32)
               + jnp.dot(tri, p3, preferred_element_type=F32))
        cb = loc + carry
        c_ref[rows, :] = cb
        carry = cb[CUM_BLK - 1:CUM_BLK, :]
    ct_ref[0] = jnp.transpose(c_ref[...])[:N_GATE, :]


def _cumsum(lf, batch, seq):
    return pl.pallas_call(
        _cumsum_kernel,
        grid=(batch,),
        in_specs=[pl.BlockSpec((seq, LANES), lambda b: (b, 0))],
        out_specs=[pl.BlockSpec((seq, LANES), lambda b: (b, 0)),
                   pl.BlockSpec((1, N_GATE, seq), lambda b: (b, 0, 0))],
        out_shape=[jax.ShapeDtypeStruct((batch * seq, LANES), F32),
                   jax.ShapeDtypeStruct((batch, N_GATE, seq), F32)],
        compiler_params=pltpu.CompilerParams(dimension_semantics=("parallel",)),
        name="gate_cumsum",
    )(lf)


def _fox_kernel(q_ref, k_ref, v_ref, c_ref, ct_ref, o_ref):
    hp = pl.program_id(1)
    seq = q_ref.shape[0]
    lane = lax.broadcasted_iota(jnp.int32, (1, LANES), 1)
    rr = lax.broadcasted_iota(jnp.int32, (FOX_TQ, FOX_TQ), 0)
    cc = lax.broadcasted_iota(jnp.int32, (FOX_TQ, FOX_TQ), 1)
    causal = cc <= rr
    nt = (((1,), (1,)), ((), ()))

    for i in range(seq // FOX_TQ):
        rows = slice(i * FOX_TQ, (i + 1) * FOX_TQ)
        w = (i + 1) * FOX_TQ
        q = q_ref[rows, :]
        cblk = c_ref[rows, :]
        outs = []
        for side in range(2):
            h = 2 * hp + side
            qm = jnp.where((lane // HEAD_DIM) == side, q, jnp.zeros_like(q))
            s = lax.dot_general(qm, k_ref[0:w, :], nt, preferred_element_type=F32)
            c_t = jnp.sum(jnp.where(lane == h, cblk, 0.0), axis=1, keepdims=True)
            c_s = ct_ref[0, pl.ds(h, 1), 0:w]
            s = s + c_t - c_s
            diag = jnp.where(causal, s[:, w - FOX_TQ:], -jnp.inf)
            s = diag if i == 0 else jnp.concatenate([s[:, :w - FOX_TQ], diag], axis=1)
            m = jnp.max(s, axis=1, keepdims=True)
            p = jnp.exp(s - m)
            l = jnp.sum(p, axis=1, keepdims=True)
            o = jnp.dot(p.astype(BF16), v_ref[0:w, :], preferred_element_type=F32)
            outs.append(o / l)
        o_ref[rows, :] = jnp.where(lane < HEAD_DIM, outs[0], outs[1]).astype(BF16)


def _fox(q, k, v, c, ct, batch, seq):
    blk = pl.BlockSpec((seq, LANES), lambda b, p: (b, p))
    return pl.pallas_call(
        _fox_kernel,
        grid=(batch, D_FOX // LANES),
        in_specs=[blk, blk, blk,
                  pl.BlockSpec((seq, LANES), lambda b, p: (b, 0)),
                  pl.BlockSpec((1, N_GATE, seq), lambda b, p: (b, 0, 0))],
        out_specs=blk,
        out_shape=jax.ShapeDtypeStruct((batch * seq, D_FOX), BF16),
        compiler_params=pltpu.CompilerParams(
            dimension_semantics=("parallel", "arbitrary"), vmem_limit_bytes=VMEM_LIMIT),
        name="fox_attn",
    )(q, k, v, c, ct)


def _swa_kernel(sink_ref, q_ref, k_ref, v_ref, o_ref):
    seq = q_ref.shape[0]
    nblk = seq // WINDOW
    lane = lax.broadcasted_iota(jnp.int32, (1, LANES), 1)
    nt = (((1,), (1,)), ((), ()))
    rows_st = SWA_GROUP * WINDOW

    def band_mask(nk):
        r = lax.broadcasted_iota(jnp.int32, (rows_st, nk), 0) % WINDOW
        c = lax.broadcasted_iota(jnp.int32, (rows_st, nk), 1)
        if nk == WINDOW:
            return c <= r
        return (c > r) & (c <= r + WINDOW)

    def sink_col(p, side):
        r = lax.broadcasted_iota(jnp.int32, (rows_st, 1), 0) // WINDOW
        col = jnp.zeros((rows_st, 1), F32)
        for j in range(SWA_GROUP):
            col = jnp.where(r == j, sink_ref[8 * p + 4 * side + j], col)
        return col

    def block(q_rows, k_rows, nk):
        mask = band_mask(nk)
        for p in range(D_SWA_KV // LANES):
            kw = k_ref[k_rows, p * LANES:(p + 1) * LANES]
            vw = v_ref[k_rows, p * LANES:(p + 1) * LANES]
            tiles = [q_ref[q_rows, (p * SWA_GROUP + j) * LANES:(p * SWA_GROUP + j + 1) * LANES]
                     for j in range(SWA_GROUP)]
            qst = jnp.concatenate(tiles, axis=0)
            outs = []
            for side in range(2):
                qm = jnp.where((lane // HEAD_DIM) == side, qst, jnp.zeros_like(qst))
                s = lax.dot_general(qm, kw, nt, preferred_element_type=F32)
                s = jnp.where(mask, s, -jnp.inf)
                sk = sink_col(p, side)
                m = jnp.maximum(jnp.max(s, axis=1, keepdims=True), sk)
                e = jnp.exp(s - m)
                l = jnp.sum(e, axis=1, keepdims=True) + jnp.exp(sk - m)
                o = jnp.dot(e.astype(BF16), vw, preferred_element_type=F32)
                outs.append(o / l)
            o_both = jnp.where(lane < HEAD_DIM, outs[0], outs[1]).astype(BF16)
            for j in range(SWA_GROUP):
                t = p * SWA_GROUP + j
                o_ref[q_rows, t * LANES:(t + 1) * LANES] = o_both[j * WINDOW:(j + 1) * WINDOW, :]

    block(slice(0, WINDOW), slice(0, WINDOW), WINDOW)

    def body(n, carry):
        q0 = pl.multiple_of(n * WINDOW, WINDOW)
        k0 = pl.multiple_of((n - 1) * WINDOW, WINDOW)
        block(pl.ds(q0, WINDOW), pl.ds(k0, 2 * WINDOW), 2 * WINDOW)
        return carry

    lax.fori_loop(1, nblk, body, 0)


def _swa(sinks, q, k, v, batch, seq):
    return pl.pallas_call(
        _swa_kernel,
        grid=(batch,),
        in_specs=[pl.BlockSpec(memory_space=pltpu.SMEM),
                  pl.BlockSpec((seq, D_SWA_Q), lambda b: (b, 0)),
                  pl.BlockSpec((seq, D_SWA_KV), lambda b: (b, 0)),
                  pl.BlockSpec((seq, D_SWA_KV), lambda b: (b, 0))],
        out_specs=pl.BlockSpec((seq, D_SWA_Q), lambda b: (b, 0)),
        out_shape=jax.ShapeDtypeStruct((batch * seq, D_SWA_Q), BF16),
        compiler_params=pltpu.CompilerParams(dimension_semantics=("parallel",), vmem_limit_bytes=VMEM_LIMIT),
        name="swa_attn",
    )(sinks, q, k, v)


def _outproj_kernel(of_ref, os_ref, h_ref, gf_ref, gs_ref, w_ref, o_ref):
    nf = _rms_rows(of_ref[...].astype(F32), gf_ref[...]).astype(BF16)
    ns = _rms_rows(os_ref[...].astype(F32), gs_ref[...]).astype(BF16)
    acc = jnp.dot(nf, w_ref[0:D_FOX, :], preferred_element_type=F32)
    acc = acc + jnp.dot(ns, w_ref[D_FOX:D_FOX + D_SWA_Q, :], preferred_element_type=F32)
    o_ref[...] = h_ref[...] + acc


def _outproj(of, os_, h, gf, gs, w):
    t, d = h.shape
    row = lambda n: pl.BlockSpec((ROW_TM, n), lambda i: (i, 0))
    const = lambda shape: pl.BlockSpec(shape, lambda i: (0, 0))
    return pl.pallas_call(
        _outproj_kernel,
        grid=(t // ROW_TM,),
        in_specs=[row(D_FOX), row(D_SWA_Q), row(d), const((1, D_FOX)), const((1, D_SWA_Q)),
                  pl.BlockSpec((D_FOX + D_SWA_Q, d), lambda i: (0, 0), pipeline_mode=pl.Buffered(1))],
        out_specs=row(d),
        out_shape=jax.ShapeDtypeStruct((t, d), F32),
        compiler_params=pltpu.CompilerParams(dimension_semantics=("parallel",), vmem_limit_bytes=VMEM_LIMIT),
        name="outproj",
    )(of, os_, h, gf, gs, w)


def _swa_col_perm():
    cols = np.concatenate([np.arange(h * HEAD_DIM, (h + 1) * HEAD_DIM) for h in SWA_HEAD_ORDER])
    return cols.astype(np.int32)


def _layer(h, pos, invf, p):
    (norm_ffn1_g, w_gate1, w_up1, w_down1, norm_mix_g, w_in, b_forget, fox_q_g, fox_k_g, swa_q_g, swa_k_g,
     sinks, out_fox_g, out_swa_g, w_out, norm_ffn2_g, w_gate2, w_up2, w_down2) = p
    batch_seq, d = h.shape
    batch, seq = pos.shape
    perm = _swa_col_perm()
    row = lambda v: v.reshape(1, -1).astype(F32)
    tile = lambda v, n: jnp.tile(v.astype(F32), n).reshape(1, -1)

    sq0 = 3 * D_FOX + N_GATE
    w_in_r = jnp.concatenate([
        w_in[:, :3 * D_FOX],
        w_in[:, sq0:sq0 + D_SWA_Q][:, perm],
        w_in[:, sq0 + D_SWA_Q:],
        w_in[:, 3 * D_FOX:sq0],
        jnp.zeros((d, LANES - N_GATE), w_in.dtype),
    ], axis=1).astype(BF16)
    bf_pad = jnp.concatenate([b_forget.astype(F32), jnp.zeros((LANES - N_GATE,), F32)]).reshape(1, LANES)
    w_out_r = jnp.concatenate([w_out[:D_FOX], w_out[D_FOX:][perm]], axis=0).astype(BF16)

    h1 = _ffn(h, row(norm_ffn1_g), w_gate1.astype(BF16), w_up1.astype(BF16), w_down1.astype(BF16))

    qf, kf, vf, qs, ks, vs, lf = _inproj(
        h1, row(norm_mix_g), w_in_r, pos.reshape(batch_seq, 1), invf,
        tile(fox_q_g, FOX_HEADS), tile(fox_k_g, FOX_HEADS), tile(swa_q_g, SWA_Q_HEADS),
        tile(swa_k_g, SWA_KV_HEADS), bf_pad)

    c, ct = _cumsum(lf, batch, seq)
    o_fox = _fox(qf, kf, vf, c, ct, batch, seq)
    o_swa = _swa(sinks.astype(F32), qs, ks, vs, batch, seq)

    h2 = _outproj(o_fox, o_swa, h1, row(out_fox_g), row(out_swa_g[perm]), w_out_r)
    return _ffn(h2, row(norm_ffn2_g), w_gate2.astype(BF16), w_up2.astype(BF16), w_down2.astype(BF16))


def kernel(x, positions, norm_ffn1_g, ffn1_w_gate, ffn1_w_up, ffn1_w_down, norm_mix_g, w_in, b_forget, fox_q_norm_g, fox_k_norm_g, swa_q_norm_g, swa_k_norm_g, swa_sinks, out_norm_fox_g, out_norm_swa_g, w_out, norm_ffn2_g, ffn2_w_gate, ffn2_w_up, ffn2_w_down):
    batch, seq, d = x.shape
    half = jnp.arange(0, HEAD_DIM, 2, dtype=F32)
    inv_freq = ROPE_THETA ** (-half / HEAD_DIM)
    invf = jnp.tile(inv_freq, LANES // (HEAD_DIM // 2)).reshape(1, LANES)
    stacks = (norm_ffn1_g, ffn1_w_gate, ffn1_w_up, ffn1_w_down, norm_mix_g, w_in, b_forget, fox_q_norm_g,
              fox_k_norm_g, swa_q_norm_g, swa_k_norm_g, swa_sinks, out_norm_fox_g, out_norm_swa_g, w_out,
              norm_ffn2_g, ffn2_w_gate, ffn2_w_up, ffn2_w_down)
    h = x.reshape(batch * seq, d)
    for layer in range(norm_ffn1_g.shape[0]):
        h = _layer(h, positions, invf, tuple(s[layer] for s in stacks))
    return h.reshape(batch, seq, d)
```

```python
import functools

import jax
import jax.numpy as jnp
import numpy as np
from jax import lax
from jax.experimental import pallas as pl
from jax.experimental.pallas import tpu as pltpu

F32 = jnp.float32
BF16 = jnp.bfloat16

D_MODEL = 2048
HEAD_DIM = 64
FOX_HEADS = 16
SWA_Q_HEADS = 16
SWA_KV_HEADS = 4
SWA_GROUP = SWA_Q_HEADS // SWA_KV_HEADS
WINDOW = 128
D_FF = 5632
ROPE_THETA = 10000.0
EPS = 1e-6

D_FOX = FOX_HEADS * HEAD_DIM
D_SWA_Q = SWA_Q_HEADS * HEAD_DIM
D_SWA_KV = SWA_KV_HEADS * HEAD_DIM
N_GATE = FOX_HEADS

LANES = 128
MXU_TILE = 256
VMEM_LIMIT = 60 * 1024 * 1024

C_FQ = 0
C_FK = C_FQ + D_FOX
C_FV = C_FK + D_FOX
C_SQ = C_FV + D_FOX
C_SK = C_SQ + D_SWA_Q
C_SV = C_SK + D_SWA_KV
C_GATE = C_SV + D_SWA_KV
D_IN_PAD = C_GATE + LANES

SWA_HEAD_ORDER = tuple(8 * p + 4 * side + j for p in range(2) for j in range(SWA_GROUP) for side in range(2))

FFN_TM = 1024
FFN_TF = 512
ROW_TM = 512
FOX_TQ = 256
CUM_BLK = 256

LOG2E = 1.4426950408889634

AUG_PARTS = 3
ONE_LANE = N_GATE


def _rms_rows(x, g):
    return x * lax.rsqrt(jnp.mean(x * x, axis=-1, keepdims=True) + EPS) * g


def _ffn_kernel(x_ref, g_ref, wg_ref, wu_ref, wd_ref, o_ref, u_ref):
    j = pl.program_id(1)

    @pl.when(j == 0)
    def _():
        x = x_ref[...]
        u_ref[...] = _rms_rows(x, g_ref[...]).astype(BF16)
        o_ref[...] = x

    u = u_ref[...]
    gate = jnp.dot(u, wg_ref[...], preferred_element_type=F32)
    up = jnp.dot(u, wu_ref[...], preferred_element_type=F32)
    a = (gate * jax.nn.sigmoid(gate) * up * 0.5).astype(BF16)
    o_ref[...] += jnp.dot(a, wd_ref[...], preferred_element_type=F32)


def _ffn(x, g, wg, wu, wd):
    t, d = x.shape
    f = wg.shape[1]
    return pl.pallas_call(
        _ffn_kernel,
        grid=(t // FFN_TM, f // FFN_TF),
        in_specs=[
            pl.BlockSpec((FFN_TM, d), lambda i, j: (i, 0)),
            pl.BlockSpec((1, d), lambda i, j: (0, 0)),
            pl.BlockSpec((d, FFN_TF), lambda i, j: (0, j)),
            pl.BlockSpec((d, FFN_TF), lambda i, j: (0, j)),
            pl.BlockSpec((FFN_TF, d), lambda i, j: (j, 0)),
        ],
        out_specs=pl.BlockSpec((FFN_TM, d), lambda i, j: (i, 0)),
        out_shape=jax.ShapeDtypeStruct((t, d), F32),
        scratch_shapes=[pltpu.VMEM((FFN_TM, d), BF16)],
        compiler_params=pltpu.CompilerParams(
            dimension_semantics=("parallel", "arbitrary"), vmem_limit_bytes=VMEM_LIMIT),
        name="ffn",
    )(x, g, wg, wu, wd)


def _head_rms(y, gain, bd):
    n = y.shape[1]
    y2 = y * y
    hi = y2.astype(BF16)
    lo = (y2 - hi.astype(F32)).astype(BF16)
    parts = []
    for c in range(n // MXU_TILE):
        sl = slice(c * MXU_TILE, (c + 1) * MXU_TILE)
        parts.append(jnp.dot(hi[:, sl], bd, preferred_element_type=F32)
                     + jnp.dot(lo[:, sl], bd, preferred_element_type=F32))
    ss = parts[0] if len(parts) == 1 else jnp.concatenate(parts, axis=1)
    return y * lax.rsqrt(ss * (1.0 / HEAD_DIM) + EPS) * gain


def _rope(y, cos, sin_signed, first_half):
    outs = []
    for c in range(y.shape[1] // LANES):
        yc = y[:, c * LANES:(c + 1) * LANES]
        partner = jnp.where(first_half, pltpu.roll(yc, LANES - HEAD_DIM // 2, 1),
                            pltpu.roll(yc, HEAD_DIM // 2, 1))
        outs.append(yc * cos + partner * sin_signed)
    return outs[0] if len(outs) == 1 else jnp.concatenate(outs, axis=1)


def _inproj_kernel(h_ref, g_ref, w_ref, pos_ref, invf_ref, gfq_ref, gfk_ref, gsq_ref, gsk_ref, bf_ref,
                   qf_ref, kf_ref, vf_ref, qs_ref, ks_ref, vs_ref, lf_ref):
    u = _rms_rows(h_ref[...], g_ref[...]).astype(BF16)

    def proj(lo, hi):
        return jnp.dot(u, w_ref[:, lo:hi], preferred_element_type=F32)

    r = lax.broadcasted_iota(jnp.int32, (MXU_TILE, MXU_TILE), 0) // HEAD_DIM
    c = lax.broadcasted_iota(jnp.int32, (MXU_TILE, MXU_TILE), 1) // HEAD_DIM
    bd = jnp.where(r == c, 1.0, 0.0).astype(BF16)

    scale = HEAD_DIM ** -0.5

    qf_ref[...] = (_head_rms(proj(C_FQ, C_FK), gfq_ref[...], bd) * (scale * LOG2E)).astype(BF16)
    kf_ref[...] = _head_rms(proj(C_FK, C_FV), gfk_ref[...], bd).astype(BF16)
    vf_ref[...] = proj(C_FV, C_SQ).astype(BF16)

    ang = pos_ref[...].astype(F32) * invf_ref[...]
    lane = lax.broadcasted_iota(jnp.int32, (1, LANES), 1)
    first_half = (lane % HEAD_DIM) < (HEAD_DIM // 2)
    cos = jnp.cos(ang)
    sin = jnp.sin(ang)
    sin_signed = jnp.where(first_half, -sin, sin)

    qs = _rope(_head_rms(proj(C_SQ, C_SK), gsq_ref[...], bd), cos, sin_signed, first_half)
    qs_ref[...] = (qs * scale).astype(BF16)
    ks = _rope(_head_rms(proj(C_SK, C_SV), gsk_ref[...], bd), cos, sin_signed, first_half)
    ks_ref[...] = ks.astype(BF16)
    vs_ref[...] = proj(C_SV, C_GATE).astype(BF16)

    z = proj(C_GATE, D_IN_PAD) + bf_ref[...]
    lf_ref[...] = jnp.minimum(z, 0.0) - jnp.log1p(jnp.exp(-jnp.abs(z)))


def _inproj(h, g, w, pos, invf, gfq, gfk, gsq, gsk, bf):
    t, d = h.shape
    row = lambda n: pl.BlockSpec((ROW_TM, n), lambda i: (i, 0))
    const = lambda shape: pl.BlockSpec(shape, lambda i: (0, 0))
    return pl.pallas_call(
        _inproj_kernel,
        grid=(t // ROW_TM,),
        in_specs=[
            row(d),
            const((1, d)),
            pl.BlockSpec((d, D_IN_PAD), lambda i: (0, 0), pipeline_mode=pl.Buffered(1)),
            row(1),
            const((1, LANES)),
            const((1, D_FOX)), const((1, D_FOX)), const((1, D_SWA_Q)), const((1, D_SWA_KV)),
            const((1, LANES)),
        ],
        out_specs=[row(D_FOX), row(D_FOX), row(D_FOX), row(D_SWA_Q), row(D_SWA_KV), row(D_SWA_KV), row(LANES)],
        out_shape=[
            jax.ShapeDtypeStruct((t, D_FOX), BF16), jax.ShapeDtypeStruct((t, D_FOX), BF16),
            jax.ShapeDtypeStruct((t, D_FOX), BF16), jax.ShapeDtypeStruct((t, D_SWA_Q), BF16),
            jax.ShapeDtypeStruct((t, D_SWA_KV), BF16), jax.ShapeDtypeStruct((t, D_SWA_KV), BF16),
            jax.ShapeDtypeStruct((t, LANES), F32),
        ],
        compiler_params=pltpu.CompilerParams(dimension_semantics=("parallel",), vmem_limit_bytes=VMEM_LIMIT),
        name="inproj",
    )(h, g, w, pos, invf, gfq, gfk, gsq, gsk, bf)


def _split3(x):
    p1 = x.astype(BF16)
    r1 = x - p1.astype(F32)
    p2 = r1.astype(BF16)
    p3 = (r1 - p2.astype(F32)).astype(BF16)
    return p1, p2, p3


def _cumsum_kernel(lf_ref, pk_ref, pq_ref, ak_ref, aq_ref):
    s = lf_ref.shape[0]
    r = lax.broadcasted_iota(jnp.int32, (CUM_BLK, CUM_BLK), 0)
    c = lax.broadcasted_iota(jnp.int32, (CUM_BLK, CUM_BLK), 1)
    tri = jnp.where(r >= c, 1.0, 0.0).astype(BF16)
    lane = lax.broadcasted_iota(jnp.int32, (1, LANES), 1)
    is_gate = lane < N_GATE
    one = jnp.where(lane == ONE_LANE, 1.0, 0.0).astype(BF16)
    carry = jnp.zeros((1, LANES), F32)
    for b in range(s // CUM_BLK):
        rows = slice(b * CUM_BLK, (b + 1) * CUM_BLK)
        p1, p2, p3 = _split3(lf_ref[rows, :])
        cb = (jnp.dot(tri, p1, preferred_element_type=F32)
              + jnp.dot(tri, p2, preferred_element_type=F32)
              + jnp.dot(tri, p3, preferred_element_type=F32)) + carry
        carry = cb[CUM_BLK - 1:CUM_BLK, :]
        c1, c2, c3 = _split3(cb * LOG2E)
        zero = jnp.zeros_like(c1)
        c1 = jnp.where(is_gate, c1, one)
        x = jnp.concatenate([c1, jnp.where(is_gate, c2, zero), jnp.where(is_gate, c3, zero)], axis=1)
        ak_ref[rows, :] = jnp.dot(x, pk_ref[...], preferred_element_type=F32).astype(BF16)
        aq_ref[rows, :] = jnp.dot(c1, pq_ref[...], preferred_element_type=F32).astype(BF16)


def _aug_maps():
    pk = np.zeros((AUG_PARTS * LANES, D_FOX), np.float32)
    pq = np.zeros((LANES, D_FOX), np.float32)
    for h in range(FOX_HEADS):
        base = (h // 2) * LANES + (HEAD_DIM if h % 2 == 0 else 0)
        for i in range(AUG_PARTS):
            pk[i * LANES + h, base + i] = -1.0
            pq[ONE_LANE, base + i] = 1.0
        pk[ONE_LANE, base + AUG_PARTS] = 1.0
        pq[h, base + AUG_PARTS] = 1.0
    return jnp.asarray(pk, BF16), jnp.asarray(pq, BF16)


def _cumsum(lf, batch, seq):
    pk, pq = _aug_maps()
    const = lambda a: pl.BlockSpec(a.shape, lambda b: (0, 0))
    out = pl.BlockSpec((seq, D_FOX), lambda b: (b, 0))
    return pl.pallas_call(
        _cumsum_kernel,
        grid=(batch,),
        in_specs=[pl.BlockSpec((seq, LANES), lambda b: (b, 0)), const(pk), const(pq)],
        out_specs=[out, out],
        out_shape=[jax.ShapeDtypeStruct((batch * seq, D_FOX), BF16)] * 2,
        compiler_params=pltpu.CompilerParams(dimension_semantics=("parallel",)),
        name="gate_cumsum",
    )(lf, pk, pq)


def _fox_kernel(q_ref, k_ref, v_ref, aq_ref, ak_ref, o_ref, qa_ref, ka_ref):
    seq = q_ref.shape[0]
    lane = lax.broadcasted_iota(jnp.int32, (1, LANES), 1)
    lo = lane < HEAD_DIM
    q, k, aq, ak = q_ref[...], k_ref[...], aq_ref[...], ak_ref[...]
    qa_ref[0] = jnp.where(lo, q, aq)
    qa_ref[1] = jnp.where(lo, aq, q)
    ka_ref[0] = jnp.where(lo, k, ak)
    ka_ref[1] = jnp.where(lo, ak, k)

    qry = lax.broadcasted_iota(jnp.int32, (FOX_TQ, FOX_TQ), 0)
    key = lax.broadcasted_iota(jnp.int32, (FOX_TQ, FOX_TQ), 1)
    causal = key <= qry
    nt = (((1,), (1,)), ((), ()))

    def scores(i, side):
        rows = slice(i * FOX_TQ, (i + 1) * FOX_TQ)
        return lax.dot_general(qa_ref[side, rows, :], ka_ref[side, 0:(i + 1) * FOX_TQ, :], nt,
                               preferred_element_type=F32)

    def softmax(i, s):
        w = (i + 1) * FOX_TQ
        diag = jnp.where(causal, s[:, w - FOX_TQ:], -jnp.inf)
        s = diag if i == 0 else jnp.concatenate([s[:, :w - FOX_TQ], diag], axis=1)
        m = jnp.max(s, axis=1, keepdims=True)
        p = jnp.exp2(s - m)
        return p.astype(BF16), jnp.sum(p, axis=1, keepdims=True)

    def values(i, p, l):
        o = jnp.dot(p, v_ref[0:(i + 1) * FOX_TQ, :], preferred_element_type=F32)
        return o / l

    chains = [(i, side) for i in reversed(range(seq // FOX_TQ)) for side in range(2)]
    s_buf, p_buf, done = {}, {}, {}
    for t in range(len(chains) + 2):
        if t < len(chains):
            s_buf[t] = scores(*chains[t])
        if 1 <= t <= len(chains):
            p_buf[t - 1] = softmax(chains[t - 1][0], s_buf.pop(t - 1))
        if t >= 2:
            i, side = chains[t - 2]
            done[(i, side)] = values(i, *p_buf.pop(t - 2))
            if side == 1:
                rows = slice(i * FOX_TQ, (i + 1) * FOX_TQ)
                o_ref[rows, :] = jnp.where(lo, done.pop((i, 0)), done.pop((i, 1))).astype(BF16)


def _fox(q, k, v, aq, ak, batch, seq):
    blk = pl.BlockSpec((seq, LANES), lambda b, p: (b, p))
    return pl.pallas_call(
        _fox_kernel,
        grid=(batch, D_FOX // LANES),
        in_specs=[blk] * 5,
        out_specs=blk,
        out_shape=jax.ShapeDtypeStruct((batch * seq, D_FOX), BF16),
        scratch_shapes=[pltpu.VMEM((2, seq, LANES), BF16), pltpu.VMEM((2, seq, LANES), BF16)],
        compiler_params=pltpu.CompilerParams(
            dimension_semantics=("parallel", "arbitrary"), vmem_limit_bytes=VMEM_LIMIT),
        name="fox_attn",
    )(q, k, v, aq, ak)


def _swa_kernel(sink_ref, q_ref, k_ref, v_ref, o_ref):
    seq = q_ref.shape[0]
    nblk = seq // WINDOW
    lane = lax.broadcasted_iota(jnp.int32, (1, LANES), 1)
    nt = (((1,), (1,)), ((), ()))
    rows_st = SWA_GROUP * WINDOW

    def band_mask(nk):
        r = lax.broadcasted_iota(jnp.int32, (rows_st, nk), 0) % WINDOW
        c = lax.broadcasted_iota(jnp.int32, (rows_st, nk), 1)
        if nk == WINDOW:
            return c <= r
        return (c > r) & (c <= r + WINDOW)

    def sink_col(p, side):
        r = lax.broadcasted_iota(jnp.int32, (rows_st, 1), 0) // WINDOW
        col = jnp.zeros((rows_st, 1), F32)
        for j in range(SWA_GROUP):
            col = jnp.where(r == j, sink_ref[8 * p + 4 * side + j], col)
        return col

    def block(q_rows, k_rows, nk):
        mask = band_mask(nk)
        for p in range(D_SWA_KV // LANES):
            kw = k_ref[k_rows, p * LANES:(p + 1) * LANES]
            vw = v_ref[k_rows, p * LANES:(p + 1) * LANES]
            tiles = [q_ref[q_rows, (p * SWA_GROUP + j) * LANES:(p * SWA_GROUP + j + 1) * LANES]
                     for j in range(SWA_GROUP)]
            qst = jnp.concatenate(tiles, axis=0)
            outs = []
            for side in range(2):
                qm = jnp.where((lane // HEAD_DIM) == side, qst, jnp.zeros_like(qst))
                s = lax.dot_general(qm, kw, nt, preferred_element_type=F32)
                s = jnp.where(mask, s, -jnp.inf)
                sk = sink_col(p, side)
                m = jnp.maximum(jnp.max(s, axis=1, keepdims=True), sk)
                e = jnp.exp(s - m)
                l = jnp.sum(e, axis=1, keepdims=True) + jnp.exp(sk - m)
                o = jnp.dot(e.astype(BF16), vw, preferred_element_type=F32)
                outs.append(o / l)
            o_both = jnp.where(lane < HEAD_DIM, outs[0], outs[1]).astype(BF16)
            for j in range(SWA_GROUP):
                t = p * SWA_GROUP + j
                o_ref[q_rows, t * LANES:(t + 1) * LANES] = o_both[j * WINDOW:(j + 1) * WINDOW, :]

    block(slice(0, WINDOW), slice(0, WINDOW), WINDOW)

    def body(n, carry):
        q0 = pl.multiple_of(n * WINDOW, WINDOW)
        k0 = pl.multiple_of((n - 1) * WINDOW, WINDOW)
        block(pl.ds(q0, WINDOW), pl.ds(k0, 2 * WINDOW), 2 * WINDOW)
        return carry

    lax.fori_loop(1, nblk, body, 0)


def _swa(sinks, q, k, v, batch, seq):
    return pl.pallas_call(
        _swa_kernel,
        grid=(batch,),
        in_specs=[pl.BlockSpec(memory_space=pltpu.SMEM),
                  pl.BlockSpec((seq, D_SWA_Q), lambda b: (b, 0)),
                  pl.BlockSpec((seq, D_SWA_KV), lambda b: (b, 0)),
                  pl.BlockSpec((seq, D_SWA_KV), lambda b: (b, 0))],
        out_specs=pl.BlockSpec((seq, D_SWA_Q), lambda b: (b, 0)),
        out_shape=jax.ShapeDtypeStruct((batch * seq, D_SWA_Q), BF16),
        compiler_params=pltpu.CompilerParams(dimension_semantics=("parallel",), vmem_limit_bytes=VMEM_LIMIT),
        name="swa_attn",
    )(sinks, q, k, v)


def _outproj_kernel(of_ref, os_ref, h_ref, gf_ref, gs_ref, wf_ref, ws_ref, o_ref):
    nf = _rms_rows(of_ref[...].astype(F32), gf_ref[...]).astype(BF16)
    ns = _rms_rows(os_ref[...].astype(F32), gs_ref[...]).astype(BF16)
    acc = jnp.dot(nf, wf_ref[...], preferred_element_type=F32)
    acc = acc + jnp.dot(ns, ws_ref[...], preferred_element_type=F32)
    o_ref[...] = h_ref[...] + acc


def _outproj(of, os_, h, gf, gs, wf, ws):
    t, d = h.shape
    row = lambda n: pl.BlockSpec((ROW_TM, n), lambda i: (i, 0))
    const = lambda shape: pl.BlockSpec(shape, lambda i: (0, 0))
    resident = lambda a: pl.BlockSpec(a.shape, lambda i: (0, 0), pipeline_mode=pl.Buffered(1))
    return pl.pallas_call(
        _outproj_kernel,
        grid=(t // ROW_TM,),
        in_specs=[row(D_FOX), row(D_SWA_Q), row(d), const((1, D_FOX)), const((1, D_SWA_Q)),
                  resident(wf), resident(ws)],
        out_specs=row(d),
        out_shape=jax.ShapeDtypeStruct((t, d), F32),
        compiler_params=pltpu.CompilerParams(dimension_semantics=("parallel",), vmem_limit_bytes=VMEM_LIMIT),
        name="outproj",
    )(of, os_, h, gf, gs, wf, ws)


def _swa_reorder(a, axis):
    shape = a.shape
    a = a.reshape(shape[:axis] + (2, 2, SWA_GROUP, HEAD_DIM) + shape[axis + 1:])
    a = jnp.swapaxes(a, axis + 1, axis + 2)
    return a.reshape(shape)


def _layer(h, pos, invf, p):
    (norm_ffn1_g, w_gate1, w_up1, w_down1, norm_mix_g, w_in, b_forget, fox_q_g, fox_k_g, swa_q_g, swa_k_g,
     sinks, out_fox_g, out_swa_g, w_out, norm_ffn2_g, w_gate2, w_up2, w_down2) = p
    batch_seq, d = h.shape
    batch, seq = pos.shape
    row = lambda v: v.reshape(1, -1).astype(F32)
    tile = lambda v, n: jnp.tile(v.astype(F32), n).reshape(1, -1)

    sq0 = 3 * D_FOX + N_GATE
    w_in_b = w_in.astype(BF16)
    w_in_r = jnp.concatenate([
        w_in_b[:, :3 * D_FOX],
        _swa_reorder(w_in_b[:, sq0:sq0 + D_SWA_Q], 1),
        w_in_b[:, sq0 + D_SWA_Q:],
        w_in_b[:, 3 * D_FOX:sq0],
        jnp.zeros((d, LANES - N_GATE), BF16),
    ], axis=1)
    bf_pad = jnp.concatenate([b_forget.astype(F32), jnp.zeros((LANES - N_GATE,), F32)]).reshape(1, LANES)
    w_out_b = w_out.astype(BF16)
    w_out_fox = w_out_b[:D_FOX]
    w_out_swa = _swa_reorder(w_out_b[D_FOX:], 0)

    h1 = _ffn(h, row(norm_ffn1_g), w_gate1.astype(BF16), w_up1.astype(BF16), w_down1.astype(BF16))

    qf, kf, vf, qs, ks, vs, lf = _inproj(
        h1, row(norm_mix_g), w_in_r, pos.reshape(batch_seq, 1), invf,
        tile(fox_q_g, FOX_HEADS), tile(fox_k_g, FOX_HEADS), tile(swa_q_g, SWA_Q_HEADS),
        tile(swa_k_g, SWA_KV_HEADS), bf_pad)

    ak, aq = _cumsum(lf, batch, seq)
    o_fox = _fox(qf, kf, vf, aq, ak, batch, seq)
    o_swa = _swa(sinks.astype(F32), qs, ks, vs, batch, seq)

    h2 = _outproj(o_fox, o_swa, h1, row(out_fox_g), row(_swa_reorder(out_swa_g, 0)), w_out_fox, w_out_swa)
    return _ffn(h2, row(norm_ffn2_g), w_gate2.astype(BF16), w_up2.astype(BF16), w_down2.astype(BF16))


def kernel(x, positions, norm_ffn1_g, ffn1_w_gate, ffn1_w_up, ffn1_w_down, norm_mix_g, w_in, b_forget, fox_q_norm_g, fox_k_norm_g, swa_q_norm_g, swa_k_norm_g, swa_sinks, out_norm_fox_g, out_norm_swa_g, w_out, norm_ffn2_g, ffn2_w_gate, ffn2_w_up, ffn2_w_down):
    batch, seq, d = x.shape
    half = jnp.arange(0, HEAD_DIM, 2, dtype=F32)
    inv_freq = ROPE_THETA ** (-half / HEAD_DIM)
    invf = jnp.tile(inv_freq, LANES // (HEAD_DIM // 2)).reshape(1, LANES)
    stacks = (norm_ffn1_g, ffn1_w_gate, ffn1_w_up, ffn1_w_down, norm_mix_g, w_in, b_forget, fox_q_norm_g,
              fox_k_norm_g, swa_q_norm_g, swa_k_norm_g, swa_sinks, out_norm_fox_g, out_norm_swa_g, w_out,
              norm_ffn2_g, ffn2_w_gate, ffn2_w_up, ffn2_w_down)
    h = x.reshape(batch * seq, d)
    for layer in range(norm_ffn1_g.shape[0]):
        h = _layer(h, positions, invf, tuple(s[layer] for s in stacks))
    return h.reshape(batch, seq, d)
```

```python
import functools

import jax
import jax.numpy as jnp
import numpy as np
from jax import lax
from jax.experimental import pallas as pl
from jax.experimental.pallas import tpu as pltpu

F32 = jnp.float32
BF16 = jnp.bfloat16

D_MODEL = 2048
HEAD_DIM = 64
FOX_HEADS = 16
SWA_Q_HEADS = 16
SWA_KV_HEADS = 4
SWA_GROUP = SWA_Q_HEADS // SWA_KV_HEADS
WINDOW = 128
D_FF = 5632
ROPE_THETA = 10000.0
EPS = 1e-6

D_FOX = FOX_HEADS * HEAD_DIM
D_SWA_Q = SWA_Q_HEADS * HEAD_DIM
D_SWA_KV = SWA_KV_HEADS * HEAD_DIM
N_GATE = FOX_HEADS

LANES = 128
MXU_TILE = 256
VMEM_LIMIT = 60 * 1024 * 1024

C_FQ = 0
C_FK = C_FQ + D_FOX
C_FV = C_FK + D_FOX
C_SQ = C_FV + D_FOX
C_SK = C_SQ + D_SWA_Q
C_SV = C_SK + D_SWA_KV
C_GATE = C_SV + D_SWA_KV
D_IN_PAD = C_GATE + LANES

SWA_HEAD_ORDER = tuple(8 * p + 4 * side + j for p in range(2) for j in range(SWA_GROUP) for side in range(2))

FFN_TM = 1024
FFN_TF = 512
ROW_TM = 512
INPROJ_TN = 512
FOX_TQ = 256
CUM_BLK = 256

LOG2E = 1.4426950408889634

AUG_PARTS = 3
ONE_LANE = N_GATE


def _rms_rows(x, g):
    return x * lax.rsqrt(jnp.mean(x * x, axis=-1, keepdims=True) + EPS) * g


def _ffn_kernel(x_ref, g_ref, wg_ref, wu_ref, wd_ref, *refs, n_cast):
    cast_in, o_ref, cast_out, u_ref = refs[:n_cast], refs[n_cast], refs[n_cast + 1:-1], refs[-1]
    j = pl.program_id(1)

    @pl.when(j == 0)
    def _():
        x = x_ref[...]
        u_ref[...] = _rms_rows(x, g_ref[...]).astype(BF16)
        o_ref[...] = x

    u = u_ref[...]
    gate = jnp.dot(u, wg_ref[...], preferred_element_type=F32)
    up = jnp.dot(u, wu_ref[...], preferred_element_type=F32)
    a = (gate * jax.nn.sigmoid(gate) * up * 0.5).astype(BF16)
    o_ref[...] += jnp.dot(a, wd_ref[...], preferred_element_type=F32)

    for src, dst in zip(cast_in, cast_out):
        dst[...] = src[...].astype(BF16)


def _ffn(x, g, wg, wu, wd, cast=()):
    t, d = x.shape
    f = wg.shape[1]
    ni, nj = t // FFN_TM, f // FFN_TF
    cast_specs = []
    for w in cast:
        if w.shape[0] % ni == 0 and w.shape[1] % nj == 0 and (w.shape[1] // nj) % LANES == 0:
            cast_specs.append(pl.BlockSpec((w.shape[0] // ni, w.shape[1] // nj), lambda i, j: (i, j)))
        else:
            assert w.shape[0] % nj == 0 and w.shape[1] % ni == 0, w.shape
            cast_specs.append(pl.BlockSpec((w.shape[0] // nj, w.shape[1] // ni), lambda i, j: (j, i)))
    outs = pl.pallas_call(
        functools.partial(_ffn_kernel, n_cast=len(cast)),
        grid=(ni, nj),
        in_specs=[
            pl.BlockSpec((FFN_TM, d), lambda i, j: (i, 0)),
            pl.BlockSpec((1, d), lambda i, j: (0, 0)),
            pl.BlockSpec((d, FFN_TF), lambda i, j: (0, j)),
            pl.BlockSpec((d, FFN_TF), lambda i, j: (0, j)),
            pl.BlockSpec((FFN_TF, d), lambda i, j: (j, 0)),
        ] + cast_specs,
        out_specs=[pl.BlockSpec((FFN_TM, d), lambda i, j: (i, 0))] + cast_specs,
        out_shape=[jax.ShapeDtypeStruct((t, d), F32)] + [jax.ShapeDtypeStruct(w.shape, BF16) for w in cast],
        scratch_shapes=[pltpu.VMEM((FFN_TM, d), BF16)],
        compiler_params=pltpu.CompilerParams(
            dimension_semantics=("parallel", "arbitrary"), vmem_limit_bytes=VMEM_LIMIT),
        name="ffn_cast" if cast else "ffn",
    )(x, g, wg, wu, wd, *cast)
    return outs[0], tuple(outs[1:])


def _head_rms(y, gain, bd):
    n = y.shape[1]
    y2 = (y * y).astype(BF16)
    parts = [jnp.dot(y2[:, c * MXU_TILE:(c + 1) * MXU_TILE], bd, preferred_element_type=F32)
             for c in range(n // MXU_TILE)]
    ss = parts[0] if len(parts) == 1 else jnp.concatenate(parts, axis=1)
    return y * lax.rsqrt(ss * (1.0 / HEAD_DIM) + EPS) * gain


def _rope(y, cos, sin_signed, first_half):
    outs = []
    for c in range(y.shape[1] // LANES):
        yc = y[:, c * LANES:(c + 1) * LANES]
        partner = jnp.where(first_half, pltpu.roll(yc, LANES - HEAD_DIM // 2, 1),
                            pltpu.roll(yc, HEAD_DIM // 2, 1))
        outs.append(yc * cos + partner * sin_signed)
    return outs[0] if len(outs) == 1 else jnp.concatenate(outs, axis=1)


def _inproj_kernel(h_ref, g_ref, w_ref, pos_ref, invf_ref, gfq_ref, gfk_ref, gsq_ref, gsk_ref, bf_ref,
                   qf_ref, kf_ref, vf_ref, qs_ref, ks_ref, vs_ref, lf_ref):
    u = _rms_rows(h_ref[...], g_ref[...]).astype(BF16)

    ang = pos_ref[...].astype(F32) * invf_ref[...]
    lane = lax.broadcasted_iota(jnp.int32, (1, LANES), 1)
    first_half = (lane % HEAD_DIM) < (HEAD_DIM // 2)
    cos = jnp.cos(ang)
    sin = jnp.sin(ang)
    sin_signed = jnp.where(first_half, -sin, sin)

    r = lax.broadcasted_iota(jnp.int32, (MXU_TILE, MXU_TILE), 0) // HEAD_DIM
    c = lax.broadcasted_iota(jnp.int32, (MXU_TILE, MXU_TILE), 1) // HEAD_DIM
    bd = jnp.where(r == c, 1.0, 0.0).astype(BF16)

    scale = HEAD_DIM ** -0.5

    def fox_q(y, lo, hi):
        qf_ref[:, lo:hi] = (_head_rms(y, gfq_ref[:, lo:hi], bd) * (scale * LOG2E)).astype(BF16)

    def fox_k(y, lo, hi):
        kf_ref[:, lo:hi] = _head_rms(y, gfk_ref[:, lo:hi], bd).astype(BF16)

    def fox_v(y, lo, hi):
        vf_ref[:, lo:hi] = y.astype(BF16)

    def swa_q(y, lo, hi):
        y = _rope(_head_rms(y, gsq_ref[:, lo:hi], bd), cos, sin_signed, first_half)
        qs_ref[:, lo:hi] = (y * (scale * LOG2E)).astype(BF16)

    def swa_k(y, lo, hi):
        ks_ref[:, lo:hi] = _rope(_head_rms(y, gsk_ref[:, lo:hi], bd), cos, sin_signed, first_half).astype(BF16)

    def swa_v(y, lo, hi):
        vs_ref[:, lo:hi] = y.astype(BF16)

    def gate(y, lo, hi):
        z = y + bf_ref[...]
        lf_ref[...] = jnp.minimum(z, 0.0) - jnp.log1p(jnp.exp(-jnp.abs(z)))

    stages = []
    for start, end, epilogue in ((C_FQ, C_FK, fox_q), (C_FK, C_FV, fox_k), (C_FV, C_SQ, fox_v),
                                 (C_SQ, C_SK, swa_q), (C_SK, C_SV, swa_k), (C_SV, C_GATE, swa_v),
                                 (C_GATE, D_IN_PAD, gate)):
        for lo in range(start, end, INPROJ_TN):
            hi = min(lo + INPROJ_TN, end)
            stages.append((lo, hi, lo - start, hi - start, epilogue))
    pending = None
    for lo, hi, olo, ohi, epilogue in stages:
        y = jnp.dot(u, w_ref[:, lo:hi], preferred_element_type=F32)
        if pending is not None:
            pending[0](*pending[1:])
        pending = (epilogue, y, olo, ohi)
    pending[0](*pending[1:])


def _inproj(h, g, w, pos, invf, gfq, gfk, gsq, gsk, bf):
    t, d = h.shape
    row = lambda n: pl.BlockSpec((ROW_TM, n), lambda i: (i, 0))
    const = lambda shape: pl.BlockSpec(shape, lambda i: (0, 0))
    return pl.pallas_call(
        _inproj_kernel,
        grid=(t // ROW_TM,),
        in_specs=[
            row(d),
            const((1, d)),
            pl.BlockSpec((d, D_IN_PAD), lambda i: (0, 0), pipeline_mode=pl.Buffered(1)),
            row(1),
            const((1, LANES)),
            const((1, D_FOX)), const((1, D_FOX)), const((1, D_SWA_Q)), const((1, D_SWA_KV)),
            const((1, LANES)),
        ],
        out_specs=[row(D_FOX), row(D_FOX), row(D_FOX), row(D_SWA_Q), row(D_SWA_KV), row(D_SWA_KV), row(LANES)],
        out_shape=[
            jax.ShapeDtypeStruct((t, D_FOX), BF16), jax.ShapeDtypeStruct((t, D_FOX), BF16),
            jax.ShapeDtypeStruct((t, D_FOX), BF16), jax.ShapeDtypeStruct((t, D_SWA_Q), BF16),
            jax.ShapeDtypeStruct((t, D_SWA_KV), BF16), jax.ShapeDtypeStruct((t, D_SWA_KV), BF16),
            jax.ShapeDtypeStruct((t, LANES), F32),
        ],
        compiler_params=pltpu.CompilerParams(dimension_semantics=("parallel",), vmem_limit_bytes=VMEM_LIMIT),
        name="inproj",
    )(h, g, w, pos, invf, gfq, gfk, gsq, gsk, bf)


def _split3(x):
    p1 = x.astype(BF16)
    r1 = x - p1.astype(F32)
    p2 = r1.astype(BF16)
    p3 = (r1 - p2.astype(F32)).astype(BF16)
    return p1, p2, p3


def _cumsum_kernel(lf_ref, pk_ref, pq_ref, ak_ref, aq_ref):
    s = lf_ref.shape[0]
    r = lax.broadcasted_iota(jnp.int32, (CUM_BLK, CUM_BLK), 0)
    c = lax.broadcasted_iota(jnp.int32, (CUM_BLK, CUM_BLK), 1)
    tri = jnp.where(r >= c, 1.0, 0.0).astype(BF16)
    lane = lax.broadcasted_iota(jnp.int32, (1, LANES), 1)
    is_gate = lane < N_GATE
    one = jnp.where(lane == ONE_LANE, 1.0, 0.0).astype(BF16)
    carry = jnp.zeros((1, LANES), F32)
    for b in range(s // CUM_BLK):
        rows = slice(b * CUM_BLK, (b + 1) * CUM_BLK)
        p1, p2, p3 = _split3(lf_ref[rows, :])
        cb = (jnp.dot(tri, p1, preferred_element_type=F32)
              + jnp.dot(tri, p2, preferred_element_type=F32)
              + jnp.dot(tri, p3, preferred_element_type=F32)) + carry
        carry = cb[CUM_BLK - 1:CUM_BLK, :]
        c1, c2, c3 = _split3(cb * LOG2E)
        zero = jnp.zeros_like(c1)
        c1 = jnp.where(is_gate, c1, one)
        x = jnp.concatenate([c1, jnp.where(is_gate, c2, zero), jnp.where(is_gate, c3, zero)], axis=1)
        ak_ref[rows, :] = jnp.dot(x, pk_ref[...], preferred_element_type=F32).astype(BF16)
        aq_ref[rows, :] = jnp.dot(c1, pq_ref[...], preferred_element_type=F32).astype(BF16)


def _aug_maps():
    pk = np.zeros((AUG_PARTS * LANES, D_FOX), np.float32)
    pq = np.zeros((LANES, D_FOX), np.float32)
    for h in range(FOX_HEADS):
        base = (h // 2) * LANES + (HEAD_DIM if h % 2 == 0 else 0)
        for i in range(AUG_PARTS):
            pk[i * LANES + h, base + i] = -1.0
            pq[ONE_LANE, base + i] = 1.0
        pk[ONE_LANE, base + AUG_PARTS] = 1.0
        pq[h, base + AUG_PARTS] = 1.0
    return jnp.asarray(pk, BF16), jnp.asarray(pq, BF16)


def _cumsum(lf, batch, seq):
    pk, pq = _aug_maps()
    const = lambda a: pl.BlockSpec(a.shape, lambda b: (0, 0))
    out = pl.BlockSpec((seq, D_FOX), lambda b: (b, 0))
    return pl.pallas_call(
        _cumsum_kernel,
        grid=(batch,),
        in_specs=[pl.BlockSpec((seq, LANES), lambda b: (b, 0)), const(pk), const(pq)],
        out_specs=[out, out],
        out_shape=[jax.ShapeDtypeStruct((batch * seq, D_FOX), BF16)] * 2,
        compiler_params=pltpu.CompilerParams(dimension_semantics=("parallel",)),
        name="gate_cumsum",
    )(lf, pk, pq)


def _fox_kernel(q_ref, k_ref, v_ref, aq_ref, ak_ref, o_ref, qa_ref, ka_ref):
    seq = q_ref.shape[0]
    lane = lax.broadcasted_iota(jnp.int32, (1, LANES), 1)
    lo = lane < HEAD_DIM
    q, k, aq, ak = q_ref[...], k_ref[...], aq_ref[...], ak_ref[...]
    qa_ref[0] = jnp.where(lo, q, aq)
    qa_ref[1] = jnp.where(lo, aq, q)
    ka_ref[0] = jnp.where(lo, k, ak)
    ka_ref[1] = jnp.where(lo, ak, k)

    qry = lax.broadcasted_iota(jnp.int32, (FOX_TQ, FOX_TQ), 0)
    key = lax.broadcasted_iota(jnp.int32, (FOX_TQ, FOX_TQ), 1)
    causal = key <= qry
    nt = (((1,), (1,)), ((), ()))

    def scores(i, side):
        rows = slice(i * FOX_TQ, (i + 1) * FOX_TQ)
        return lax.dot_general(qa_ref[side, rows, :], ka_ref[side, 0:(i + 1) * FOX_TQ, :], nt,
                               preferred_element_type=F32)

    def softmax(i, s):
        w = (i + 1) * FOX_TQ
        diag = jnp.where(causal, s[:, w - FOX_TQ:], -jnp.inf)
        s = diag if i == 0 else jnp.concatenate([s[:, :w - FOX_TQ], diag], axis=1)
        m = jnp.max(s, axis=1, keepdims=True)
        p = jnp.exp2(s - m)
        return p.astype(BF16), jnp.sum(p, axis=1, keepdims=True)

    def values(i, p, l):
        o = jnp.dot(p, v_ref[0:(i + 1) * FOX_TQ, :], preferred_element_type=F32)
        return o / l

    chains = [(i, side) for i in reversed(range(seq // FOX_TQ)) for side in range(2)]
    s_buf, p_buf, done = {}, {}, {}
    for t in range(len(chains) + 2):
        if t < len(chains):
            s_buf[t] = scores(*chains[t])
        if 1 <= t <= len(chains):
            p_buf[t - 1] = softmax(chains[t - 1][0], s_buf.pop(t - 1))
        if t >= 2:
            i, side = chains[t - 2]
            done[(i, side)] = values(i, *p_buf.pop(t - 2))
            if side == 1:
                rows = slice(i * FOX_TQ, (i + 1) * FOX_TQ)
                o_ref[rows, :] = jnp.where(lo, done.pop((i, 0)), done.pop((i, 1))).astype(BF16)


def _fox(q, k, v, aq, ak, batch, seq):
    blk = pl.BlockSpec((seq, LANES), lambda b, p: (b, p))
    return pl.pallas_call(
        _fox_kernel,
        grid=(batch, D_FOX // LANES),
        in_specs=[blk] * 5,
        out_specs=blk,
        out_shape=jax.ShapeDtypeStruct((batch * seq, D_FOX), BF16),
        scratch_shapes=[pltpu.VMEM((2, seq, LANES), BF16), pltpu.VMEM((2, seq, LANES), BF16)],
        compiler_params=pltpu.CompilerParams(
            dimension_semantics=("parallel", "arbitrary"), vmem_limit_bytes=VMEM_LIMIT),
        name="fox_attn",
    )(q, k, v, aq, ak)


SWA_ROWS = SWA_GROUP * WINDOW
SWA_CHAINS = tuple((p, side) for p in range(D_SWA_KV // LANES) for side in range(2))


def _swa_kernel(sink_ref, q_ref, k_ref, v_ref, o_ref, bias_ref, sink_col_ref):
    seq = q_ref.shape[0]
    nblk = seq // WINDOW
    lane = lax.broadcasted_iota(jnp.int32, (1, LANES), 1)
    lo = lane < HEAD_DIM
    nt = (((1,), (1,)), ((), ()))

    r = lax.broadcasted_iota(jnp.int32, (SWA_ROWS, 2 * WINDOW), 0) % WINDOW
    c = lax.broadcasted_iota(jnp.int32, (SWA_ROWS, 2 * WINDOW), 1)
    bias_ref[...] = jnp.where((c > r) & (c <= r + WINDOW), 0.0, -jnp.inf)
    head = lax.broadcasted_iota(jnp.int32, (SWA_ROWS, LANES), 0) // WINDOW
    for n, (p, side) in enumerate(SWA_CHAINS):
        col = jnp.zeros((SWA_ROWS, LANES), F32)
        for j in range(SWA_GROUP):
            col = jnp.where(head == j, sink_ref[8 * p + 4 * side + j] * LOG2E, col)
        sink_col_ref[n] = col

    def block(q_rows, k_rows, first):
        cols = slice(WINDOW, 2 * WINDOW) if first else slice(0, 2 * WINDOW)

        def scores(n):
            p, side = SWA_CHAINS[n]
            tiles = [q_ref[q_rows, (p * SWA_GROUP + j) * LANES:(p * SWA_GROUP + j + 1) * LANES]
                     for j in range(SWA_GROUP)]
            qst = jnp.concatenate(tiles, axis=0)
            qm = jnp.where(lo == (side == 0), qst, jnp.zeros_like(qst))
            kw = k_ref[k_rows, p * LANES:(p + 1) * LANES]
            return lax.dot_general(qm, kw, nt, preferred_element_type=F32)

        def softmax(n, s):
            s = s + bias_ref[:, cols]
            sk = sink_col_ref[n]
            m = jnp.maximum(jnp.max(s, axis=1, keepdims=True), sk)
            e = jnp.exp2(s - jnp.concatenate([m] * (s.shape[1] // LANES), axis=1))
            l = jnp.sum(e, axis=1, keepdims=True) + jnp.exp2(sk - m)
            return e.astype(BF16), l

        def values(n, e, l):
            p, _ = SWA_CHAINS[n]
            vw = v_ref[k_rows, p * LANES:(p + 1) * LANES]
            return jnp.dot(e, vw, preferred_element_type=F32) / l

        s_buf, e_buf, done = {}, {}, {}
        for t in range(len(SWA_CHAINS) + 2):
            if t < len(SWA_CHAINS):
                s_buf[t] = scores(t)
            if 1 <= t <= len(SWA_CHAINS):
                e_buf[t - 1] = softmax(t - 1, s_buf.pop(t - 1))
            if t >= 2:
                p, side = SWA_CHAINS[t - 2]
                done[side] = values(t - 2, *e_buf.pop(t - 2))
                if side == 1:
                    o_both = jnp.where(lo, done.pop(0), done.pop(1)).astype(BF16)
                    for j in range(SWA_GROUP):
                        tile = p * SWA_GROUP + j
                        o_ref[q_rows, tile * LANES:(tile + 1) * LANES] = o_both[j * WINDOW:(j + 1) * WINDOW, :]

    block(slice(0, WINDOW), slice(0, WINDOW), True)

    def body(n, carry):
        q0 = pl.multiple_of(n * WINDOW, WINDOW)
        k0 = pl.multiple_of((n - 1) * WINDOW, WINDOW)
        block(pl.ds(q0, WINDOW), pl.ds(k0, 2 * WINDOW), False)
        return carry

    lax.fori_loop(1, nblk, body, 0)


def _swa(sinks, q, k, v, batch, seq):
    return pl.pallas_call(
        _swa_kernel,
        grid=(batch,),
        in_specs=[pl.BlockSpec(memory_space=pltpu.SMEM),
                  pl.BlockSpec((seq, D_SWA_Q), lambda b: (b, 0)),
                  pl.BlockSpec((seq, D_SWA_KV), lambda b: (b, 0)),
                  pl.BlockSpec((seq, D_SWA_KV), lambda b: (b, 0))],
        out_specs=pl.BlockSpec((seq, D_SWA_Q), lambda b: (b, 0)),
        out_shape=jax.ShapeDtypeStruct((batch * seq, D_SWA_Q), BF16),
        scratch_shapes=[pltpu.VMEM((SWA_ROWS, 2 * WINDOW), F32),
                        pltpu.VMEM((len(SWA_CHAINS), SWA_ROWS, LANES), F32)],
        compiler_params=pltpu.CompilerParams(dimension_semantics=("parallel",), vmem_limit_bytes=VMEM_LIMIT),
        name="swa_attn",
    )(sinks, q, k, v)


def _outproj_kernel(of_ref, os_ref, h_ref, gf_ref, gs_ref, wf_ref, ws_ref, o_ref):
    nf = _rms_rows(of_ref[...].astype(F32), gf_ref[...]).astype(BF16)
    ns = _rms_rows(os_ref[...].astype(F32), gs_ref[...]).astype(BF16)
    acc = jnp.dot(nf, wf_ref[...], preferred_element_type=F32)
    acc = acc + jnp.dot(ns, ws_ref[...], preferred_element_type=F32)
    o_ref[...] = h_ref[...] + acc


def _outproj(of, os_, h, gf, gs, wf, ws):
    t, d = h.shape
    row = lambda n: pl.BlockSpec((ROW_TM, n), lambda i: (i, 0))
    const = lambda shape: pl.BlockSpec(shape, lambda i: (0, 0))
    resident = lambda a: pl.BlockSpec(a.shape, lambda i: (0, 0), pipeline_mode=pl.Buffered(1))
    return pl.pallas_call(
        _outproj_kernel,
        grid=(t // ROW_TM,),
        in_specs=[row(D_FOX), row(D_SWA_Q), row(d), const((1, D_FOX)), const((1, D_SWA_Q)),
                  resident(wf), resident(ws)],
        out_specs=row(d),
        out_shape=jax.ShapeDtypeStruct((t, d), F32),
        compiler_params=pltpu.CompilerParams(dimension_semantics=("parallel",), vmem_limit_bytes=VMEM_LIMIT),
        name="outproj",
    )(of, os_, h, gf, gs, wf, ws)


def _swa_reorder(a, axis):
    shape = a.shape
    a = a.reshape(shape[:axis] + (2, 2, SWA_GROUP, HEAD_DIM) + shape[axis + 1:])
    a = jnp.swapaxes(a, axis + 1, axis + 2)
    return a.reshape(shape)


WPREP_ROWS = 256


def _w_in_prep_kernel(w_ref, o_ref):
    x = w_ref[...]
    gate0 = 3 * D_FOX
    sq0 = gate0 + N_GATE
    sq = x[:, sq0:sq0 + D_SWA_Q]
    pieces = [x[:, :gate0]]
    pieces += [sq[:, h * HEAD_DIM:(h + 1) * HEAD_DIM] for h in SWA_HEAD_ORDER]
    pieces += [x[:, sq0 + D_SWA_Q:], x[:, gate0:sq0], jnp.zeros((x.shape[0], LANES - N_GATE), x.dtype)]
    o_ref[...] = jnp.concatenate(pieces, axis=1).astype(BF16)


def _w_in_prep(w_in):
    d, n = w_in.shape
    return pl.pallas_call(
        _w_in_prep_kernel,
        grid=(d // WPREP_ROWS,),
        in_specs=[pl.BlockSpec((WPREP_ROWS, n), lambda i: (i, 0))],
        out_specs=pl.BlockSpec((WPREP_ROWS, D_IN_PAD), lambda i: (i, 0)),
        out_shape=jax.ShapeDtypeStruct((d, D_IN_PAD), BF16),
        compiler_params=pltpu.CompilerParams(dimension_semantics=("parallel",)),
        name="w_in_prep",
    )(w_in)


def _layer(h, pos, invf, p):
    (norm_ffn1_g, w_gate1, w_up1, w_down1, norm_mix_g, w_in, b_forget, fox_q_g, fox_k_g, swa_q_g, swa_k_g,
     sinks, out_fox_g, out_swa_g, w_out, norm_ffn2_g, w_gate2, w_up2, w_down2) = p
    batch_seq, d = h.shape
    batch, seq = pos.shape
    row = lambda v: v.reshape(1, -1).astype(F32)
    tile = lambda v, n: jnp.tile(v.astype(F32), n).reshape(1, -1)

    w_in_r = _w_in_prep(w_in)
    bf_pad = jnp.concatenate([b_forget.astype(F32), jnp.zeros((LANES - N_GATE,), F32)]).reshape(1, LANES)
    w_out_b = w_out.astype(BF16)
    w_out_fox = w_out_b[:D_FOX]
    w_out_swa = _swa_reorder(w_out_b[D_FOX:], 0)

    h1, (w_gate2_b, w_up2_b, w_down2_b) = _ffn(
        h, row(norm_ffn1_g), w_gate1.astype(BF16), w_up1.astype(BF16), w_down1.astype(BF16),
        cast=(w_gate2, w_up2, w_down2))

    qf, kf, vf, qs, ks, vs, lf = _inproj(
        h1, row(norm_mix_g), w_in_r, pos.reshape(batch_seq, 1), invf,
        tile(fox_q_g, FOX_HEADS), tile(fox_k_g, FOX_HEADS), tile(swa_q_g, SWA_Q_HEADS),
        tile(swa_k_g, SWA_KV_HEADS), bf_pad)

    ak, aq = _cumsum(lf, batch, seq)
    o_fox = _fox(qf, kf, vf, aq, ak, batch, seq)
    o_swa = _swa(sinks.astype(F32), qs, ks, vs, batch, seq)

    h2 = _outproj(o_fox, o_swa, h1, row(out_fox_g), row(_swa_reorder(out_swa_g, 0)), w_out_fox, w_out_swa)
    return _ffn(h2, row(norm_ffn2_g), w_gate2_b, w_up2_b, w_down2_b)[0]


def kernel(x, positions, norm_ffn1_g, ffn1_w_gate, ffn1_w_up, ffn1_w_down, norm_mix_g, w_in, b_forget, fox_q_norm_g, fox_k_norm_g, swa_q_norm_g, swa_k_norm_g, swa_sinks, out_norm_fox_g, out_norm_swa_g, w_out, norm_ffn2_g, ffn2_w_gate, ffn2_w_up, ffn2_w_down):
    batch, seq, d = x.shape
    half = jnp.arange(0, HEAD_DIM, 2, dtype=F32)
    inv_freq = ROPE_THETA ** (-half / HEAD_DIM)
    invf = jnp.tile(inv_freq, LANES // (HEAD_DIM // 2)).reshape(1, LANES)
    stacks = (norm_ffn1_g, ffn1_w_gate, ffn1_w_up, ffn1_w_down, norm_mix_g, w_in, b_forget, fox_q_norm_g,
              fox_k_norm_g, swa_q_norm_g, swa_k_norm_g, swa_sinks, out_norm_fox_g, out_norm_swa_g, w_out,
              norm_ffn2_g, ffn2_w_gate, ffn2_w_up, ffn2_w_down)
    h = x.reshape(batch * seq, d)
    for layer in range(norm_ffn1_g.shape[0]):
        h = _layer(h, positions, invf, tuple(s[layer] for s in stacks))
    return h.reshape(batch, seq, d)
```

```python
import functools

import jax
import jax.numpy as jnp
import numpy as np
from jax import lax
from jax.experimental import pallas as pl
from jax.experimental.pallas import tpu as pltpu

F32 = jnp.float32
BF16 = jnp.bfloat16

D_MODEL = 2048
HEAD_DIM = 64
FOX_HEADS = 16
SWA_Q_HEADS = 16
SWA_KV_HEADS = 4
SWA_GROUP = SWA_Q_HEADS // SWA_KV_HEADS
WINDOW = 128
D_FF = 5632
ROPE_THETA = 10000.0
EPS = 1e-6

D_FOX = FOX_HEADS * HEAD_DIM
D_SWA_Q = SWA_Q_HEADS * HEAD_DIM
D_SWA_KV = SWA_KV_HEADS * HEAD_DIM
N_GATE = FOX_HEADS

LANES = 128
MXU_TILE = 256
VMEM_LIMIT = 60 * 1024 * 1024

C_FQ = 0
C_FK = C_FQ + D_FOX
C_FV = C_FK + D_FOX
C_SQ = C_FV + D_FOX
C_SK = C_SQ + D_SWA_Q
C_SV = C_SK + D_SWA_KV
C_GATE = C_SV + D_SWA_KV
D_IN_PAD = C_GATE + LANES

SWA_HEAD_ORDER = tuple(8 * p + 4 * side + j for p in range(2) for j in range(SWA_GROUP) for side in range(2))

FFN_TM = 1024
FFN_TF = 512
HEAD_TF = 256
HEAD_CHUNKS = 8
ROW_TM = 512
INPROJ_TN = 512
FOX_TQ = 256
CUM_BLK = 256

LOG2E = 1.4426950408889634

AUG_PARTS = 3
ONE_LANE = AUG_PARTS * N_GATE


def _rms_rows(x, g):
    return x * lax.rsqrt(jnp.mean(x * x, axis=-1, keepdims=True) + EPS) * g


def _ffn_begin(x_ref, g_ref, u_ref, o_ref):
    x = x_ref[...]
    u_ref[...] = _rms_rows(x, g_ref[...]).astype(BF16)
    o_ref[...] = x


def _ffn_step(u_ref, wg, wu, wd, o_ref):
    u = u_ref[...]
    gate = jnp.dot(u, wg, preferred_element_type=F32)
    up = jnp.dot(u, wu, preferred_element_type=F32)
    a = (gate * jax.nn.sigmoid(gate) * up * 0.5).astype(BF16)
    o_ref[...] += jnp.dot(a, wd, preferred_element_type=F32)


def _ffn_kernel(x_ref, g_ref, wg_ref, wu_ref, wd_ref, *refs, n_cast, has_head):
    head_ref = refs[0] if has_head else None
    refs = refs[1:] if has_head else refs
    cast_in, o_ref, cast_out, u_ref = refs[:n_cast], refs[n_cast], refs[n_cast + 1:-1], refs[-1]
    i, j = pl.program_id(0), pl.program_id(1)

    def side_job():
        for src, dst in zip(cast_in, cast_out):
            dst[...] = src[...].astype(BF16)

    if has_head:
        @pl.when(i == 0)
        def _():
            @pl.when(j < HEAD_CHUNKS)
            def _():
                rows = FFN_TM // HEAD_CHUNKS
                o_ref[pl.ds(pl.multiple_of(j * rows, rows), rows), :] = head_ref[...]

            side_job()

        @pl.when((i > 0) & (j == 0))
        def _():
            _ffn_begin(x_ref, g_ref, u_ref, o_ref)

        @pl.when(i > 0)
        def _():
            _ffn_step(u_ref, wg_ref[...], wu_ref[...], wd_ref[...], o_ref)
            side_job()
    else:
        @pl.when(j == 0)
        def _():
            _ffn_begin(x_ref, g_ref, u_ref, o_ref)

        _ffn_step(u_ref, wg_ref[...], wu_ref[...], wd_ref[...], o_ref)
        side_job()


def _ffn(x, g, wg, wu, wd, cast=(), head=None):
    t, d = x.shape
    f = wg.shape[1]
    ni, nj = t // FFN_TM, f // FFN_TF
    has_head = head is not None
    live = (lambda i: jnp.minimum(i, 1)) if has_head else (lambda i: 1)
    row = (lambda i: jnp.maximum(i, 1)) if has_head else (lambda i: i)
    cast_specs = []
    for w in cast:
        if w.shape[0] % ni == 0 and w.shape[1] % nj == 0 and (w.shape[1] // nj) % LANES == 0:
            cast_specs.append(pl.BlockSpec((w.shape[0] // ni, w.shape[1] // nj), lambda i, j: (i, j)))
        else:
            assert w.shape[0] % nj == 0 and w.shape[1] % ni == 0, w.shape
            cast_specs.append(pl.BlockSpec((w.shape[0] // nj, w.shape[1] // ni), lambda i, j: (j, i)))
    head_specs, head_args = [], []
    if has_head:
        last = HEAD_CHUNKS - 1
        head_specs = [pl.BlockSpec((FFN_TM // HEAD_CHUNKS, d),
                                   lambda i, j: (jnp.where(i == 0, jnp.minimum(j, last), last), 0))]
        head_args = [head]
    outs = pl.pallas_call(
        functools.partial(_ffn_kernel, n_cast=len(cast), has_head=has_head),
        grid=(ni, nj),
        in_specs=[
            pl.BlockSpec((FFN_TM, d), lambda i, j: (row(i), 0)),
            pl.BlockSpec((1, d), lambda i, j: (0, 0)),
            pl.BlockSpec((d, FFN_TF), lambda i, j: (0, j * live(i))),
            pl.BlockSpec((d, FFN_TF), lambda i, j: (0, j * live(i))),
            pl.BlockSpec((FFN_TF, d), lambda i, j: (j * live(i), 0)),
        ] + head_specs + cast_specs,
        out_specs=[pl.BlockSpec((FFN_TM, d), lambda i, j: (i, 0))] + cast_specs,
        out_shape=[jax.ShapeDtypeStruct((t, d), F32)] + [jax.ShapeDtypeStruct(w.shape, BF16) for w in cast],
        scratch_shapes=[pltpu.VMEM((FFN_TM, d), BF16)],
        compiler_params=pltpu.CompilerParams(
            dimension_semantics=("parallel", "arbitrary"), vmem_limit_bytes=VMEM_LIMIT),
        name="ffn_cast" if cast else "ffn",
    )(x, g, wg, wu, wd, *head_args, *cast)
    return outs[0], tuple(outs[1:])


def _ffn_head_kernel(x_ref, g_ref, wg_ref, wu_ref, wd_ref, o_ref, wgb_ref, wub_ref, wdb_ref, u_ref):
    @pl.when(pl.program_id(0) == 0)
    def _():
        _ffn_begin(x_ref, g_ref, u_ref, o_ref)

    wgb_ref[...] = wg_ref[...].astype(BF16)
    wub_ref[...] = wu_ref[...].astype(BF16)
    wdb_ref[...] = wd_ref[...].astype(BF16)
    _ffn_step(u_ref, wgb_ref[...], wub_ref[...], wdb_ref[...], o_ref)


def _ffn_head(x, g, wg, wu, wd):
    t, d = x.shape
    f = wg.shape[1]
    once = lambda shape: pl.BlockSpec(shape, lambda j: (0, 0), pipeline_mode=pl.Buffered(1))
    col = pl.BlockSpec((d, HEAD_TF), lambda j: (0, j))
    rowb = pl.BlockSpec((HEAD_TF, d), lambda j: (j, 0))
    return pl.pallas_call(
        _ffn_head_kernel,
        grid=(f // HEAD_TF,),
        in_specs=[once((FFN_TM, d)), once((1, d)), col, col, rowb],
        out_specs=[pl.BlockSpec((FFN_TM, d), lambda j: (0, 0)), col, col, rowb],
        out_shape=[jax.ShapeDtypeStruct((FFN_TM, d), F32), jax.ShapeDtypeStruct(wg.shape, BF16),
                   jax.ShapeDtypeStruct(wu.shape, BF16), jax.ShapeDtypeStruct(wd.shape, BF16)],
        scratch_shapes=[pltpu.VMEM((FFN_TM, d), BF16)],
        compiler_params=pltpu.CompilerParams(dimension_semantics=("arbitrary",), vmem_limit_bytes=VMEM_LIMIT),
        name="ffn_head",
    )(x, g, wg, wu, wd)


def _head_rms(y, gain, bd):
    n = y.shape[1]
    y2 = (y * y).astype(BF16)
    parts = [jnp.dot(y2[:, c * MXU_TILE:(c + 1) * MXU_TILE], bd, preferred_element_type=F32)
             for c in range(n // MXU_TILE)]
    ss = parts[0] if len(parts) == 1 else jnp.concatenate(parts, axis=1)
    return y * lax.rsqrt(ss * (1.0 / HEAD_DIM) + EPS) * gain


def _rope(y, cos, sin_signed, first_half):
    outs = []
    for c in range(y.shape[1] // LANES):
        yc = y[:, c * LANES:(c + 1) * LANES]
        partner = jnp.where(first_half, pltpu.roll(yc, LANES - HEAD_DIM // 2, 1),
                            pltpu.roll(yc, HEAD_DIM // 2, 1))
        outs.append(yc * cos + partner * sin_signed)
    return outs[0] if len(outs) == 1 else jnp.concatenate(outs, axis=1)


def _inproj_kernel(h_ref, g_ref, w_ref, pos_ref, invf_ref, gfq_ref, gfk_ref, gsq_ref, gsk_ref, bf_ref,
                   qf_ref, kf_ref, vf_ref, qs_ref, ks_ref, vs_ref, lf_ref):
    u = _rms_rows(h_ref[...], g_ref[...]).astype(BF16)

    ang = pos_ref[...].astype(F32) * invf_ref[...]
    lane = lax.broadcasted_iota(jnp.int32, (1, LANES), 1)
    first_half = (lane % HEAD_DIM) < (HEAD_DIM // 2)
    cos = jnp.cos(ang)
    sin = jnp.sin(ang)
    sin_signed = jnp.where(first_half, -sin, sin)

    r = lax.broadcasted_iota(jnp.int32, (MXU_TILE, MXU_TILE), 0) // HEAD_DIM
    c = lax.broadcasted_iota(jnp.int32, (MXU_TILE, MXU_TILE), 1) // HEAD_DIM
    bd = jnp.where(r == c, 1.0, 0.0).astype(BF16)

    scale = HEAD_DIM ** -0.5

    def fox_q(y, lo, hi):
        qf_ref[:, lo:hi] = (_head_rms(y, gfq_ref[:, lo:hi], bd) * (scale * LOG2E)).astype(BF16)

    def fox_k(y, lo, hi):
        kf_ref[:, lo:hi] = _head_rms(y, gfk_ref[:, lo:hi], bd).astype(BF16)

    def fox_v(y, lo, hi):
        vf_ref[:, lo:hi] = y.astype(BF16)

    def swa_q(y, lo, hi):
        y = _rope(_head_rms(y, gsq_ref[:, lo:hi], bd), cos, sin_signed, first_half)
        qs_ref[:, lo:hi] = (y * (scale * LOG2E)).astype(BF16)

    def swa_k(y, lo, hi):
        ks_ref[:, lo:hi] = _rope(_head_rms(y, gsk_ref[:, lo:hi], bd), cos, sin_signed, first_half).astype(BF16)

    def swa_v(y, lo, hi):
        vs_ref[:, lo:hi] = y.astype(BF16)

    def gate(y, lo, hi):
        z = y + bf_ref[...]
        lf_ref[...] = jnp.minimum(z, 0.0) - jnp.log1p(jnp.exp(-jnp.abs(z)))

    stages = []
    for start, end, epilogue in ((C_FQ, C_FK, fox_q), (C_FK, C_FV, fox_k), (C_FV, C_SQ, fox_v),
                                 (C_SQ, C_SK, swa_q), (C_SK, C_SV, swa_k), (C_SV, C_GATE, swa_v),
                                 (C_GATE, D_IN_PAD, gate)):
        for lo in range(start, end, INPROJ_TN):
            hi = min(lo + INPROJ_TN, end)
            stages.append((lo, hi, lo - start, hi - start, epilogue))
    pending = None
    for lo, hi, olo, ohi, epilogue in stages:
        y = jnp.dot(u, w_ref[:, lo:hi], preferred_element_type=F32)
        if pending is not None:
            pending[0](*pending[1:])
        pending = (epilogue, y, olo, ohi)
    pending[0](*pending[1:])


def _inproj(h, g, w, pos, invf, gfq, gfk, gsq, gsk, bf):
    t, d = h.shape
    row = lambda n: pl.BlockSpec((ROW_TM, n), lambda i: (i, 0))
    const = lambda shape: pl.BlockSpec(shape, lambda i: (0, 0))
    return pl.pallas_call(
        _inproj_kernel,
        grid=(t // ROW_TM,),
        in_specs=[
            row(d),
            const((1, d)),
            pl.BlockSpec((d, D_IN_PAD), lambda i: (0, 0), pipeline_mode=pl.Buffered(1)),
            row(1),
            const((1, LANES)),
            const((1, D_FOX)), const((1, D_FOX)), const((1, D_SWA_Q)), const((1, D_SWA_KV)),
            const((1, LANES)),
        ],
        out_specs=[row(D_FOX), row(D_FOX), row(D_FOX), row(D_SWA_Q), row(D_SWA_KV), row(D_SWA_KV), row(LANES)],
        out_shape=[
            jax.ShapeDtypeStruct((t, D_FOX), BF16), jax.ShapeDtypeStruct((t, D_FOX), BF16),
            jax.ShapeDtypeStruct((t, D_FOX), BF16), jax.ShapeDtypeStruct((t, D_SWA_Q), BF16),
            jax.ShapeDtypeStruct((t, D_SWA_KV), BF16), jax.ShapeDtypeStruct((t, D_SWA_KV), BF16),
            jax.ShapeDtypeStruct((t, LANES), F32),
        ],
        compiler_params=pltpu.CompilerParams(dimension_semantics=("parallel",), vmem_limit_bytes=VMEM_LIMIT),
        name="inproj",
    )(h, g, w, pos, invf, gfq, gfk, gsq, gsk, bf)


def _split3(x):
    p1 = x.astype(BF16)
    r1 = x - p1.astype(F32)
    p2 = r1.astype(BF16)
    p3 = (r1 - p2.astype(F32)).astype(BF16)
    return p1, p2, p3


def _cumsum_kernel(lf_ref, pmap_ref, ak_ref, aq_ref):
    s = lf_ref.shape[0]
    r = lax.broadcasted_iota(jnp.int32, (CUM_BLK, CUM_BLK), 0)
    c = lax.broadcasted_iota(jnp.int32, (CUM_BLK, CUM_BLK), 1)
    tri = jnp.where(r >= c, 1.0, 0.0).astype(BF16)
    lane = lax.broadcasted_iota(jnp.int32, (1, LANES), 1)
    is_gate = lane < N_GATE
    one = jnp.where(lane == ONE_LANE, 1.0, 0.0)
    carry = jnp.zeros((1, LANES), F32)
    for b in range(s // CUM_BLK):
        rows = slice(b * CUM_BLK, (b + 1) * CUM_BLK)
        p1, p2, p3 = _split3(lf_ref[rows, :])
        cb = (jnp.dot(tri, p1, preferred_element_type=F32)
              + jnp.dot(tri, p2, preferred_element_type=F32)
              + jnp.dot(tri, p3, preferred_element_type=F32)) + carry
        carry = cb[CUM_BLK - 1:CUM_BLK, :]
        terms = [jnp.where(is_gate, t.astype(F32), 0.0) for t in _split3(cb * LOG2E)]
        packed = one + terms[0]
        for i in range(1, AUG_PARTS):
            packed = packed + pltpu.roll(terms[i], i * N_GATE, 1)
        placed = jnp.dot(packed.astype(BF16), pmap_ref[...], preferred_element_type=F32)
        ak_ref[rows, :] = placed[:, :D_FOX].astype(BF16)
        aq_ref[rows, :] = placed[:, D_FOX:].astype(BF16)


def _aug_map():
    pmap = np.zeros((LANES, 2 * D_FOX), np.float32)
    for h in range(FOX_HEADS):
        base = (h // 2) * LANES + (HEAD_DIM if h % 2 == 0 else 0)
        for i in range(AUG_PARTS):
            pmap[i * N_GATE + h, base + i] = -1.0
            pmap[ONE_LANE, D_FOX + base + i] = 1.0
        pmap[ONE_LANE, base + AUG_PARTS] = 1.0
        pmap[h, D_FOX + base + AUG_PARTS] = 1.0
    return jnp.asarray(pmap, BF16)


def _cumsum(lf, batch, seq):
    pmap = _aug_map()
    out = pl.BlockSpec((seq, D_FOX), lambda b: (b, 0))
    return pl.pallas_call(
        _cumsum_kernel,
        grid=(batch,),
        in_specs=[pl.BlockSpec((seq, LANES), lambda b: (b, 0)), pl.BlockSpec(pmap.shape, lambda b: (0, 0))],
        out_specs=[out, out],
        out_shape=[jax.ShapeDtypeStruct((batch * seq, D_FOX), BF16)] * 2,
        compiler_params=pltpu.CompilerParams(dimension_semantics=("parallel",)),
        name="gate_cumsum",
    )(lf, pmap)


FOLD_ROWS = 64


def _fold_rows(x, op):
    slabs = x.reshape(x.shape[0] // FOLD_ROWS, FOLD_ROWS, x.shape[1])
    return op(op(slabs, axis=0), axis=0, keepdims=True)


def _fox_kernel(q_ref, k_ref, v_ref, aq_ref, ak_ref, o_ref, qa_ref, ka_ref, vt_ref):
    seq = q_ref.shape[0]
    lane = lax.broadcasted_iota(jnp.int32, (1, LANES), 1)
    lo = lane < HEAD_DIM
    q, k, aq, ak = q_ref[...], k_ref[...], aq_ref[...], ak_ref[...]
    qa_ref[0] = jnp.where(lo, q, aq)
    qa_ref[1] = jnp.where(lo, aq, q)
    ka_ref[0] = jnp.where(lo, k, ak)
    ka_ref[1] = jnp.where(lo, ak, k)
    vt_ref[...] = jnp.transpose(v_ref[...].astype(F32)).astype(BF16)

    key = lax.broadcasted_iota(jnp.int32, (FOX_TQ, FOX_TQ), 0)
    qry = lax.broadcasted_iota(jnp.int32, (FOX_TQ, FOX_TQ), 1)
    causal = key <= qry
    nt = (((1,), (1,)), ((), ()))

    def scores(i, side):
        rows = slice(i * FOX_TQ, (i + 1) * FOX_TQ)
        return lax.dot_general(ka_ref[side, 0:(i + 1) * FOX_TQ, :], qa_ref[side, rows, :], nt,
                               preferred_element_type=F32)

    def softmax(i, s):
        w = (i + 1) * FOX_TQ
        diag = jnp.where(causal, s[w - FOX_TQ:, :], -jnp.inf)
        s = diag if i == 0 else jnp.concatenate([s[:w - FOX_TQ, :], diag], axis=0)
        m = _fold_rows(s, jnp.max)
        p = jnp.exp2(s - m)
        return p.astype(BF16), _fold_rows(p, jnp.sum)

    def values(i, p, l):
        o = jnp.dot(vt_ref[:, 0:(i + 1) * FOX_TQ], p, preferred_element_type=F32)
        return o / l

    chains = [(i, side) for i in reversed(range(seq // FOX_TQ)) for side in range(2)]
    s_buf, p_buf, done = {}, {}, {}
    for t in range(len(chains) + 2):
        if t < len(chains):
            s_buf[t] = scores(*chains[t])
        if 1 <= t <= len(chains):
            p_buf[t - 1] = softmax(chains[t - 1][0], s_buf.pop(t - 1))
        if t >= 2:
            i, side = chains[t - 2]
            done[(i, side)] = values(i, *p_buf.pop(t - 2))
            if side == 1:
                rows = slice(i * FOX_TQ, (i + 1) * FOX_TQ)
                ot = jnp.concatenate([done.pop((i, 0))[:HEAD_DIM], done.pop((i, 1))[HEAD_DIM:]], axis=0)
                o_ref[rows, :] = jnp.transpose(ot).astype(BF16)


def _fox(q, k, v, aq, ak, batch, seq):
    blk = pl.BlockSpec((seq, LANES), lambda b, p: (b, p))
    return pl.pallas_call(
        _fox_kernel,
        grid=(batch, D_FOX // LANES),
        in_specs=[blk] * 5,
        out_specs=blk,
        out_shape=jax.ShapeDtypeStruct((batch * seq, D_FOX), BF16),
        scratch_shapes=[pltpu.VMEM((2, seq, LANES), BF16), pltpu.VMEM((2, seq, LANES), BF16),
                        pltpu.VMEM((LANES, seq), BF16)],
        compiler_params=pltpu.CompilerParams(
            dimension_semantics=("parallel", "arbitrary"), vmem_limit_bytes=VMEM_LIMIT),
        name="fox_attn",
    )(q, k, v, aq, ak)


SWA_ROWS = SWA_GROUP * WINDOW
SWA_CHAINS = tuple((p, side) for p in range(D_SWA_KV // LANES) for side in range(2))


def _swa_kernel(sink_ref, q_ref, k_ref, v_ref, o_ref, bias_ref, sink_col_ref):
    seq = q_ref.shape[0]
    nblk = seq // WINDOW
    lane = lax.broadcasted_iota(jnp.int32, (1, LANES), 1)
    lo = lane < HEAD_DIM
    nt = (((1,), (1,)), ((), ()))

    r = lax.broadcasted_iota(jnp.int32, (SWA_ROWS, 2 * WINDOW), 0) % WINDOW
    c = lax.broadcasted_iota(jnp.int32, (SWA_ROWS, 2 * WINDOW), 1)
    bias_ref[...] = jnp.where((c > r) & (c <= r + WINDOW), 0.0, -jnp.inf)
    head = lax.broadcasted_iota(jnp.int32, (SWA_ROWS, LANES), 0) // WINDOW
    for n, (p, side) in enumerate(SWA_CHAINS):
        col = jnp.zeros((SWA_ROWS, LANES), F32)
        for j in range(SWA_GROUP):
            col = jnp.where(head == j, sink_ref[8 * p + 4 * side + j] * LOG2E, col)
        sink_col_ref[n] = col

    def block(q_rows, k_rows, first):
        cols = slice(WINDOW, 2 * WINDOW) if first else slice(0, 2 * WINDOW)

        def scores(n):
            p, side = SWA_CHAINS[n]
            tiles = [q_ref[q_rows, (p * SWA_GROUP + j) * LANES:(p * SWA_GROUP + j + 1) * LANES]
                     for j in range(SWA_GROUP)]
            qst = jnp.concatenate(tiles, axis=0)
            qm = jnp.where(lo == (side == 0), qst, jnp.zeros_like(qst))
            kw = k_ref[k_rows, p * LANES:(p + 1) * LANES]
            return lax.dot_general(qm, kw, nt, preferred_element_type=F32)

        def softmax(n, s):
            s = s + bias_ref[:, cols]
            sk = sink_col_ref[n]
            m = jnp.maximum(jnp.max(s, axis=1, keepdims=True), sk)
            e = jnp.exp2(s - jnp.concatenate([m] * (s.shape[1] // LANES), axis=1))
            l = jnp.sum(e, axis=1, keepdims=True) + jnp.exp2(sk - m)
            return e.astype(BF16), l

        def values(n, e, l):
            p, _ = SWA_CHAINS[n]
            vw = v_ref[k_rows, p * LANES:(p + 1) * LANES]
            return jnp.dot(e, vw, preferred_element_type=F32) / l

        s_buf, e_buf, done = {}, {}, {}
        for t in range(len(SWA_CHAINS) + 2):
            if t < len(SWA_CHAINS):
                s_buf[t] = scores(t)
            if 1 <= t <= len(SWA_CHAINS):
                e_buf[t - 1] = softmax(t - 1, s_buf.pop(t - 1))
            if t >= 2:
                p, side = SWA_CHAINS[t - 2]
                done[side] = values(t - 2, *e_buf.pop(t - 2))
                if side == 1:
                    o_both = jnp.where(lo, done.pop(0), done.pop(1)).astype(BF16)
                    for j in range(SWA_GROUP):
                        tile = p * SWA_GROUP + j
                        o_ref[q_rows, tile * LANES:(tile + 1) * LANES] = o_both[j * WINDOW:(j + 1) * WINDOW, :]

    block(slice(0, WINDOW), slice(0, WINDOW), True)

    def body(n, carry):
        q0 = pl.multiple_of(n * WINDOW, WINDOW)
        k0 = pl.multiple_of((n - 1) * WINDOW, WINDOW)
        block(pl.ds(q0, WINDOW), pl.ds(k0, 2 * WINDOW), False)
        return carry

    lax.fori_loop(1, nblk, body, 0)


def _swa(sinks, q, k, v, batch, seq):
    return pl.pallas_call(
        _swa_kernel,
        grid=(batch,),
        in_specs=[pl.BlockSpec(memory_space=pltpu.SMEM),
                  pl.BlockSpec((seq, D_SWA_Q), lambda b: (b, 0)),
                  pl.BlockSpec((seq, D_SWA_KV), lambda b: (b, 0)),
                  pl.BlockSpec((seq, D_SWA_KV), lambda b: (b, 0))],
        out_specs=pl.BlockSpec((seq, D_SWA_Q), lambda b: (b, 0)),
        out_shape=jax.ShapeDtypeStruct((batch * seq, D_SWA_Q), BF16),
        scratch_shapes=[pltpu.VMEM((SWA_ROWS, 2 * WINDOW), F32),
                        pltpu.VMEM((len(SWA_CHAINS), SWA_ROWS, LANES), F32)],
        compiler_params=pltpu.CompilerParams(dimension_semantics=("parallel",), vmem_limit_bytes=VMEM_LIMIT),
        name="swa_attn",
    )(sinks, q, k, v)


def _outproj_kernel(of_ref, os_ref, h_ref, gf_ref, gs_ref, wf_ref, ws_ref, o_ref):
    nf = _rms_rows(of_ref[...].astype(F32), gf_ref[...]).astype(BF16)
    ns = _rms_rows(os_ref[...].astype(F32), gs_ref[...]).astype(BF16)
    acc = jnp.dot(nf, wf_ref[...], preferred_element_type=F32)
    acc = acc + jnp.dot(ns, ws_ref[...], preferred_element_type=F32)
    o_ref[...] = h_ref[...] + acc


def _outproj(of, os_, h, gf, gs, wf, ws):
    t, d = h.shape
    row = lambda n: pl.BlockSpec((ROW_TM, n), lambda i: (i, 0))
    const = lambda shape: pl.BlockSpec(shape, lambda i: (0, 0))
    resident = lambda a: pl.BlockSpec(a.shape, lambda i: (0, 0), pipeline_mode=pl.Buffered(1))
    return pl.pallas_call(
        _outproj_kernel,
        grid=(t // ROW_TM,),
        in_specs=[row(D_FOX), row(D_SWA_Q), row(d), const((1, D_FOX)), const((1, D_SWA_Q)),
                  resident(wf), resident(ws)],
        out_specs=row(d),
        out_shape=jax.ShapeDtypeStruct((t, d), F32),
        compiler_params=pltpu.CompilerParams(dimension_semantics=("parallel",), vmem_limit_bytes=VMEM_LIMIT),
        name="outproj",
    )(of, os_, h, gf, gs, wf, ws)


def _swa_reorder(a, axis):
    shape = a.shape
    a = a.reshape(shape[:axis] + (2, 2, SWA_GROUP, HEAD_DIM) + shape[axis + 1:])
    a = jnp.swapaxes(a, axis + 1, axis + 2)
    return a.reshape(shape)


WPREP_ROWS = 256


def _w_in_prep_kernel(w_ref, o_ref):
    x = w_ref[...]
    gate0 = 3 * D_FOX
    sq0 = gate0 + N_GATE
    sq = x[:, sq0:sq0 + D_SWA_Q]
    pieces = [x[:, :gate0]]
    pieces += [sq[:, h * HEAD_DIM:(h + 1) * HEAD_DIM] for h in SWA_HEAD_ORDER]
    pieces += [x[:, sq0 + D_SWA_Q:], x[:, gate0:sq0], jnp.zeros((x.shape[0], LANES - N_GATE), x.dtype)]
    o_ref[...] = jnp.concatenate(pieces, axis=1).astype(BF16)


def _w_in_prep(w_in_stack, layer):
    _, d, n = w_in_stack.shape
    return pl.pallas_call(
        _w_in_prep_kernel,
        grid=(d // WPREP_ROWS,),
        in_specs=[pl.BlockSpec((None, WPREP_ROWS, n), lambda i: (layer, i, 0))],
        out_specs=pl.BlockSpec((WPREP_ROWS, D_IN_PAD), lambda i: (i, 0)),
        out_shape=jax.ShapeDtypeStruct((d, D_IN_PAD), BF16),
        compiler_params=pltpu.CompilerParams(dimension_semantics=("parallel",)),
        name="w_in_prep",
    )(w_in_stack)


def _layer(h, pos, invf, p, w_in_stack, layer):
    (norm_ffn1_g, w_gate1, w_up1, w_down1, norm_mix_g, b_forget, fox_q_g, fox_k_g, swa_q_g, swa_k_g,
     sinks, out_fox_g, out_swa_g, w_out, norm_ffn2_g, w_gate2, w_up2, w_down2) = p
    batch_seq, d = h.shape
    batch, seq = pos.shape
    row = lambda v: v.reshape(1, -1).astype(F32)
    tile = lambda v, n: jnp.tile(v.astype(F32), n).reshape(1, -1)

    w_in_r = _w_in_prep(w_in_stack, layer)
    bf_pad = jnp.concatenate([b_forget.astype(F32), jnp.zeros((LANES - N_GATE,), F32)]).reshape(1, LANES)
    w_out_b = w_out.astype(BF16)
    w_out_fox = w_out_b[:D_FOX]
    w_out_swa = _swa_reorder(w_out_b[D_FOX:], 0)

    h1_head, w_gate1_b, w_up1_b, w_down1_b = _ffn_head(h, row(norm_ffn1_g), w_gate1, w_up1, w_down1)
    h1, (w_gate2_b, w_up2_b, w_down2_b) = _ffn(
        h, row(norm_ffn1_g), w_gate1_b, w_up1_b, w_down1_b, cast=(w_gate2, w_up2, w_down2), head=h1_head)

    qf, kf, vf, qs, ks, vs, lf = _inproj(
        h1, row(norm_mix_g), w_in_r, pos.reshape(batch_seq, 1), invf,
        tile(fox_q_g, FOX_HEADS), tile(fox_k_g, FOX_HEADS), tile(swa_q_g, SWA_Q_HEADS),
        tile(swa_k_g, SWA_KV_HEADS), bf_pad)

    ak, aq = _cumsum(lf, batch, seq)
    o_fox = _fox(qf, kf, vf, aq, ak, batch, seq)
    o_swa = _swa(sinks.astype(F32), qs, ks, vs, batch, seq)

    h2 = _outproj(o_fox, o_swa, h1, row(out_fox_g), row(_swa_reorder(out_swa_g, 0)), w_out_fox, w_out_swa)
    return _ffn(h2, row(norm_ffn2_g), w_gate2_b, w_up2_b, w_down2_b)[0]


def kernel(x, positions, norm_ffn1_g, ffn1_w_gate, ffn1_w_up, ffn1_w_down, norm_mix_g, w_in, b_forget, fox_q_norm_g, fox_k_norm_g, swa_q_norm_g, swa_k_norm_g, swa_sinks, out_norm_fox_g, out_norm_swa_g, w_out, norm_ffn2_g, ffn2_w_gate, ffn2_w_up, ffn2_w_down):
    batch, seq, d = x.shape
    half = jnp.arange(0, HEAD_DIM, 2, dtype=F32)
    inv_freq = ROPE_THETA ** (-half / HEAD_DIM)
    invf = jnp.tile(inv_freq, LANES // (HEAD_DIM // 2)).reshape(1, LANES)
    stacks = (norm_ffn1_g, ffn1_w_gate, ffn1_w_up, ffn1_w_down, norm_mix_g, b_forget, fox_q_norm_g,
              fox_k_norm_g, swa_q_norm_g, swa_k_norm_g, swa_sinks, out_norm_fox_g, out_norm_swa_g, w_out,
              norm_ffn2_g, ffn2_w_gate, ffn2_w_up, ffn2_w_down)
    h = x.reshape(batch * seq, d)
    for layer in range(norm_ffn1_g.shape[0]):
        h = _layer(h, positions, invf, tuple(s[layer] for s in stacks), w_in, layer)
    return h.reshape(batch, seq, d)
```

```python
import functools

import jax
import jax.numpy as jnp
import numpy as np
from jax import lax
from jax.experimental import pallas as pl
from jax.experimental.pallas import tpu as pltpu

F32 = jnp.float32
BF16 = jnp.bfloat16

D_MODEL = 2048
HEAD_DIM = 64
FOX_HEADS = 16
SWA_Q_HEADS = 16
SWA_KV_HEADS = 4
SWA_GROUP = SWA_Q_HEADS // SWA_KV_HEADS
WINDOW = 128
D_FF = 5632
ROPE_THETA = 10000.0
EPS = 1e-6

D_FOX = FOX_HEADS * HEAD_DIM
D_SWA_Q = SWA_Q_HEADS * HEAD_DIM
D_SWA_KV = SWA_KV_HEADS * HEAD_DIM
N_GATE = FOX_HEADS

LANES = 128
SUBLANES = 8
MXU_TILE = 256
VMEM_LIMIT = 60 * 1024 * 1024

C_FQ = 0
C_FK = C_FQ + D_FOX
C_FV = C_FK + D_FOX
C_SQ = C_FV + D_FOX
C_SK = C_SQ + D_SWA_Q
C_SV = C_SK + D_SWA_KV
C_GATE = C_SV + D_SWA_KV
D_IN_PAD = C_GATE + LANES

SWA_HEAD_ORDER = tuple(8 * p + 4 * side + j for p in range(2) for j in range(SWA_GROUP) for side in range(2))

FFN_TM = 1024
FFN_TF = 512
HEAD_TF = 256
HEAD_CHUNKS = 8
ROW_TM = 512
INPROJ_TN = 512
FOX_TQ = 256
CUM_BLK = 256

LOG2E = 1.4426950408889634

AUG_PARTS = 3
ONE_LANE = AUG_PARTS * N_GATE


def _rms_rows(x, g):
    return x * lax.rsqrt(jnp.mean(x * x, axis=-1, keepdims=True) + EPS) * g


def _ffn_begin(x_ref, g_ref, u_ref, o_ref):
    x = x_ref[...]
    u_ref[...] = _rms_rows(x, g_ref[...]).astype(BF16)
    o_ref[...] = x


def _ffn_step(u_ref, wg, wu, wd, o_ref):
    u = u_ref[...]
    gate = jnp.dot(u, wg, preferred_element_type=F32)
    up = jnp.dot(u, wu, preferred_element_type=F32)
    a = (gate * jax.nn.sigmoid(gate) * up * 0.5).astype(BF16)
    o_ref[...] += jnp.dot(a, wd, preferred_element_type=F32)


def _ffn_kernel(x_ref, g_ref, wg_ref, wu_ref, wd_ref, *refs, n_cast, has_head):
    head_ref = refs[0] if has_head else None
    refs = refs[1:] if has_head else refs
    cast_in, o_ref, cast_out, u_ref = refs[:n_cast], refs[n_cast], refs[n_cast + 1:-1], refs[-1]
    i, j = pl.program_id(0), pl.program_id(1)

    def side_job():
        for src, dst in zip(cast_in, cast_out):
            dst[...] = src[...].astype(BF16)

    if has_head:
        @pl.when(i == 0)
        def _():
            @pl.when(j < HEAD_CHUNKS)
            def _():
                rows = FFN_TM // HEAD_CHUNKS
                o_ref[pl.ds(pl.multiple_of(j * rows, rows), rows), :] = head_ref[...]

            side_job()

        @pl.when((i > 0) & (j == 0))
        def _():
            _ffn_begin(x_ref, g_ref, u_ref, o_ref)

        @pl.when(i > 0)
        def _():
            _ffn_step(u_ref, wg_ref[...], wu_ref[...], wd_ref[...], o_ref)
            side_job()
    else:
        @pl.when(j == 0)
        def _():
            _ffn_begin(x_ref, g_ref, u_ref, o_ref)

        _ffn_step(u_ref, wg_ref[...], wu_ref[...], wd_ref[...], o_ref)
        side_job()


def _ffn(x, g, wg, wu, wd, cast=(), head=None):
    t, d = x.shape
    f = wg.shape[1]
    ni, nj = t // FFN_TM, f // FFN_TF
    has_head = head is not None
    live = (lambda i: jnp.minimum(i, 1)) if has_head else (lambda i: 1)
    row = (lambda i: jnp.maximum(i, 1)) if has_head else (lambda i: i)
    cast_specs = []
    for w in cast:
        r, c = w.shape
        if r % ni == 0 and c % nj == 0 and (c // nj) % LANES == 0:
            cast_specs.append(pl.BlockSpec((r // ni, c // nj), lambda i, j: (i, j)))
        elif r % nj == 0 and c % ni == 0 and (c // ni) % LANES == 0:
            cast_specs.append(pl.BlockSpec((r // nj, c // ni), lambda i, j: (j, i)))
        else:
            assert r % ni == 0, w.shape
            cast_specs.append(pl.BlockSpec((r // ni, c), lambda i, j: (i, 0)))
    head_specs, head_args = [], []
    if has_head:
        last = HEAD_CHUNKS - 1
        head_specs = [pl.BlockSpec((FFN_TM // HEAD_CHUNKS, d),
                                   lambda i, j: (jnp.where(i == 0, jnp.minimum(j, last), last), 0))]
        head_args = [head]
    outs = pl.pallas_call(
        functools.partial(_ffn_kernel, n_cast=len(cast), has_head=has_head),
        grid=(ni, nj),
        in_specs=[
            pl.BlockSpec((FFN_TM, d), lambda i, j: (row(i), 0)),
            pl.BlockSpec((1, d), lambda i, j: (0, 0)),
            pl.BlockSpec((d, FFN_TF), lambda i, j: (0, j * live(i))),
            pl.BlockSpec((d, FFN_TF), lambda i, j: (0, j * live(i))),
            pl.BlockSpec((FFN_TF, d), lambda i, j: (j * live(i), 0)),
        ] + head_specs + cast_specs,
        out_specs=[pl.BlockSpec((FFN_TM, d), lambda i, j: (i, 0))] + cast_specs,
        out_shape=[jax.ShapeDtypeStruct((t, d), F32)] + [jax.ShapeDtypeStruct(w.shape, BF16) for w in cast],
        scratch_shapes=[pltpu.VMEM((FFN_TM, d), BF16)],
        compiler_params=pltpu.CompilerParams(
            dimension_semantics=("parallel", "arbitrary"), vmem_limit_bytes=VMEM_LIMIT),
        name="ffn_cast" if cast else "ffn",
    )(x, g, wg, wu, wd, *head_args, *cast)
    return outs[0], tuple(outs[1:])


def _ffn_head_kernel(x_ref, g_ref, wg_ref, wu_ref, wd_ref, o_ref, wgb_ref, wub_ref, wdb_ref, u_ref):
    @pl.when(pl.program_id(0) == 0)
    def _():
        _ffn_begin(x_ref, g_ref, u_ref, o_ref)

    wgb_ref[...] = wg_ref[...].astype(BF16)
    wub_ref[...] = wu_ref[...].astype(BF16)
    wdb_ref[...] = wd_ref[...].astype(BF16)
    _ffn_step(u_ref, wgb_ref[...], wub_ref[...], wdb_ref[...], o_ref)


def _ffn_head(x, g, wg, wu, wd):
    t, d = x.shape
    f = wg.shape[1]
    once = lambda shape: pl.BlockSpec(shape, lambda j: (0, 0), pipeline_mode=pl.Buffered(1))
    col = pl.BlockSpec((d, HEAD_TF), lambda j: (0, j))
    rowb = pl.BlockSpec((HEAD_TF, d), lambda j: (j, 0))
    return pl.pallas_call(
        _ffn_head_kernel,
        grid=(f // HEAD_TF,),
        in_specs=[once((FFN_TM, d)), once((1, d)), col, col, rowb],
        out_specs=[pl.BlockSpec((FFN_TM, d), lambda j: (0, 0)), col, col, rowb],
        out_shape=[jax.ShapeDtypeStruct((FFN_TM, d), F32), jax.ShapeDtypeStruct(wg.shape, BF16),
                   jax.ShapeDtypeStruct(wu.shape, BF16), jax.ShapeDtypeStruct(wd.shape, BF16)],
        scratch_shapes=[pltpu.VMEM((FFN_TM, d), BF16)],
        compiler_params=pltpu.CompilerParams(dimension_semantics=("arbitrary",), vmem_limit_bytes=VMEM_LIMIT),
        name="ffn_head",
    )(x, g, wg, wu, wd)


def _head_rms(y, gain, bd):
    n = y.shape[1]
    y2 = (y * y).astype(BF16)
    parts = [jnp.dot(y2[:, c * MXU_TILE:(c + 1) * MXU_TILE], bd, preferred_element_type=F32)
             for c in range(n // MXU_TILE)]
    ss = parts[0] if len(parts) == 1 else jnp.concatenate(parts, axis=1)
    return y * lax.rsqrt(ss * (1.0 / HEAD_DIM) + EPS) * gain


def _rope(y, cos, sin_signed, first_half):
    outs = []
    for c in range(y.shape[1] // LANES):
        yc = y[:, c * LANES:(c + 1) * LANES]
        partner = jnp.where(first_half, pltpu.roll(yc, LANES - HEAD_DIM // 2, 1),
                            pltpu.roll(yc, HEAD_DIM // 2, 1))
        outs.append(yc * cos + partner * sin_signed)
    return outs[0] if len(outs) == 1 else jnp.concatenate(outs, axis=1)


def _inproj_kernel(h_ref, g_ref, w_ref, pos_ref, invf_ref, gfq_ref, gfk_ref, gsq_ref, gsk_ref, bf_ref,
                   qf_ref, kf_ref, vf_ref, qs_ref, ks_ref, vs_ref, lf_ref):
    u = _rms_rows(h_ref[...], g_ref[...]).astype(BF16)

    ang = pos_ref[...].astype(F32) * invf_ref[...]
    lane = lax.broadcasted_iota(jnp.int32, (1, LANES), 1)
    first_half = (lane % HEAD_DIM) < (HEAD_DIM // 2)
    cos = jnp.cos(ang)
    sin = jnp.sin(ang)
    sin_signed = jnp.where(first_half, -sin, sin)

    r = lax.broadcasted_iota(jnp.int32, (MXU_TILE, MXU_TILE), 0) // HEAD_DIM
    c = lax.broadcasted_iota(jnp.int32, (MXU_TILE, MXU_TILE), 1) // HEAD_DIM
    bd = jnp.where(r == c, 1.0, 0.0).astype(BF16)

    scale = HEAD_DIM ** -0.5

    def fox_q(y, lo, hi):
        qf_ref[:, lo:hi] = (_head_rms(y, gfq_ref[:, lo:hi], bd) * (scale * LOG2E)).astype(BF16)

    def fox_k(y, lo, hi):
        kf_ref[:, lo:hi] = _head_rms(y, gfk_ref[:, lo:hi], bd).astype(BF16)

    def fox_v(y, lo, hi):
        vf_ref[:, lo:hi] = y.astype(BF16)

    def swa_q(y, lo, hi):
        y = _rope(_head_rms(y, gsq_ref[:, lo:hi], bd), cos, sin_signed, first_half)
        qs_ref[:, lo:hi] = (y * (scale * LOG2E)).astype(BF16)

    def swa_k(y, lo, hi):
        ks_ref[:, lo:hi] = _rope(_head_rms(y, gsk_ref[:, lo:hi], bd), cos, sin_signed, first_half).astype(BF16)

    def swa_v(y, lo, hi):
        vs_ref[:, lo:hi] = y.astype(BF16)

    def gate(y, lo, hi):
        z = y + bf_ref[...]
        lf_ref[...] = jnp.minimum(z, 0.0) - jnp.log1p(jnp.exp(-jnp.abs(z)))

    stages = []
    for start, end, epilogue in ((C_FQ, C_FK, fox_q), (C_FK, C_FV, fox_k), (C_FV, C_SQ, fox_v),
                                 (C_SQ, C_SK, swa_q), (C_SK, C_SV, swa_k), (C_SV, C_GATE, swa_v),
                                 (C_GATE, D_IN_PAD, gate)):
        for lo in range(start, end, INPROJ_TN):
            hi = min(lo + INPROJ_TN, end)
            stages.append((lo, hi, lo - start, hi - start, epilogue))
    pending = None
    for lo, hi, olo, ohi, epilogue in stages:
        y = jnp.dot(u, w_ref[:, lo:hi], preferred_element_type=F32)
        if pending is not None:
            pending[0](*pending[1:])
        pending = (epilogue, y, olo, ohi)
    pending[0](*pending[1:])


def _inproj(h, g, w, pos, invf, gfq, gfk, gsq, gsk, bf):
    t, d = h.shape
    row = lambda n: pl.BlockSpec((ROW_TM, n), lambda i: (i, 0))
    const = lambda shape: pl.BlockSpec(shape, lambda i: (0, 0))
    return pl.pallas_call(
        _inproj_kernel,
        grid=(t // ROW_TM,),
        in_specs=[
            row(d),
            const((1, d)),
            pl.BlockSpec((d, D_IN_PAD), lambda i: (0, 0), pipeline_mode=pl.Buffered(1)),
            row(1),
            const((1, LANES)),
            const((1, D_FOX)), const((1, D_FOX)), const((1, D_SWA_Q)), const((1, D_SWA_KV)),
            const((1, LANES)),
        ],
        out_specs=[row(D_FOX), row(D_FOX), row(D_FOX), row(D_SWA_Q), row(D_SWA_KV), row(D_SWA_KV), row(LANES)],
        out_shape=[
            jax.ShapeDtypeStruct((t, D_FOX), BF16), jax.ShapeDtypeStruct((t, D_FOX), BF16),
            jax.ShapeDtypeStruct((t, D_FOX), BF16), jax.ShapeDtypeStruct((t, D_SWA_Q), BF16),
            jax.ShapeDtypeStruct((t, D_SWA_KV), BF16), jax.ShapeDtypeStruct((t, D_SWA_KV), BF16),
            jax.ShapeDtypeStruct((t, LANES), F32),
        ],
        compiler_params=pltpu.CompilerParams(dimension_semantics=("parallel",), vmem_limit_bytes=VMEM_LIMIT),
        name="inproj",
    )(h, g, w, pos, invf, gfq, gfk, gsq, gsk, bf)


def _split3(x):
    p1 = x.astype(BF16)
    r1 = x - p1.astype(F32)
    p2 = r1.astype(BF16)
    p3 = (r1 - p2.astype(F32)).astype(BF16)
    return p1, p2, p3


def _cumsum_kernel(lf_ref, pmap_ref, ak_ref, aq_ref):
    s = lf_ref.shape[0]
    r = lax.broadcasted_iota(jnp.int32, (CUM_BLK, CUM_BLK), 0)
    c = lax.broadcasted_iota(jnp.int32, (CUM_BLK, CUM_BLK), 1)
    tri = jnp.where(r >= c, 1.0, 0.0).astype(BF16)
    lane = lax.broadcasted_iota(jnp.int32, (1, LANES), 1)
    is_gate = lane < N_GATE
    one = jnp.where(lane == ONE_LANE, 1.0, 0.0)
    carry = jnp.zeros((1, LANES), F32)
    for b in range(s // CUM_BLK):
        rows = slice(b * CUM_BLK, (b + 1) * CUM_BLK)
        p1, p2, p3 = _split3(lf_ref[rows, :])
        cb = (jnp.dot(tri, p1, preferred_element_type=F32)
              + jnp.dot(tri, p2, preferred_element_type=F32)
              + jnp.dot(tri, p3, preferred_element_type=F32)) + carry
        carry = cb[CUM_BLK - 1:CUM_BLK, :]
        terms = [jnp.where(is_gate, t.astype(F32), 0.0) for t in _split3(cb * LOG2E)]
        packed = one + terms[0]
        for i in range(1, AUG_PARTS):
            packed = packed + pltpu.roll(terms[i], i * N_GATE, 1)
        placed = jnp.dot(packed.astype(BF16), pmap_ref[...], preferred_element_type=F32)
        ak_ref[rows, :] = placed[:, :D_FOX].astype(BF16)
        aq_ref[rows, :] = placed[:, D_FOX:].astype(BF16)


def _aug_map():
    pmap = np.zeros((LANES, 2 * D_FOX), np.float32)
    for h in range(FOX_HEADS):
        base = (h // 2) * LANES + (HEAD_DIM if h % 2 == 0 else 0)
        for i in range(AUG_PARTS):
            pmap[i * N_GATE + h, base + i] = -1.0
            pmap[ONE_LANE, D_FOX + base + i] = 1.0
        pmap[ONE_LANE, base + AUG_PARTS] = 1.0
        pmap[h, D_FOX + base + AUG_PARTS] = 1.0
    return jnp.asarray(pmap, BF16)


def _cumsum(lf, batch, seq):
    pmap = _aug_map()
    out = pl.BlockSpec((seq, D_FOX), lambda b: (b, 0))
    return pl.pallas_call(
        _cumsum_kernel,
        grid=(batch,),
        in_specs=[pl.BlockSpec((seq, LANES), lambda b: (b, 0)), pl.BlockSpec(pmap.shape, lambda b: (0, 0))],
        out_specs=[out, out],
        out_shape=[jax.ShapeDtypeStruct((batch * seq, D_FOX), BF16)] * 2,
        compiler_params=pltpu.CompilerParams(dimension_semantics=("parallel",)),
        name="gate_cumsum",
    )(lf, pmap)


def _fox_kernel(q_ref, k_ref, v_ref, aq_ref, ak_ref, o_ref, qa_ref, ka_ref):
    seq = q_ref.shape[0]
    lane = lax.broadcasted_iota(jnp.int32, (1, LANES), 1)
    lo = lane < HEAD_DIM
    q, k, aq, ak = q_ref[...], k_ref[...], aq_ref[...], ak_ref[...]
    qa_ref[0] = jnp.where(lo, q, aq)
    qa_ref[1] = jnp.where(lo, aq, q)
    ka_ref[0] = jnp.where(lo, k, ak)
    ka_ref[1] = jnp.where(lo, ak, k)

    qry = lax.broadcasted_iota(jnp.int32, (FOX_TQ, FOX_TQ), 0)
    key = lax.broadcasted_iota(jnp.int32, (FOX_TQ, FOX_TQ), 1)
    causal = key <= qry
    nt = (((1,), (1,)), ((), ()))

    def scores(i, side):
        rows = slice(i * FOX_TQ, (i + 1) * FOX_TQ)
        return lax.dot_general(qa_ref[side, rows, :], ka_ref[side, 0:(i + 1) * FOX_TQ, :], nt,
                               preferred_element_type=F32)

    def softmax(i, s):
        w = (i + 1) * FOX_TQ
        diag = jnp.where(causal, s[:, w - FOX_TQ:], -jnp.inf)
        s = diag if i == 0 else jnp.concatenate([s[:, :w - FOX_TQ], diag], axis=1)
        m = jnp.max(s, axis=1, keepdims=True)
        p = jnp.exp2(s - m)
        return p.astype(BF16), jnp.sum(p, axis=1, keepdims=True)

    def values(i, p, l):
        o = jnp.dot(p, v_ref[0:(i + 1) * FOX_TQ, :], preferred_element_type=F32)
        return o / l

    chains = [(i, side) for i in reversed(range(seq // FOX_TQ)) for side in range(2)]
    s_buf, p_buf, done = {}, {}, {}
    for t in range(len(chains) + 2):
        if t < len(chains):
            s_buf[t] = scores(*chains[t])
        if 1 <= t <= len(chains):
            p_buf[t - 1] = softmax(chains[t - 1][0], s_buf.pop(t - 1))
        if t >= 2:
            i, side = chains[t - 2]
            done[(i, side)] = values(i, *p_buf.pop(t - 2))
            if side == 1:
                rows = slice(i * FOX_TQ, (i + 1) * FOX_TQ)
                o_ref[rows, :] = jnp.where(lo, done.pop((i, 0)), done.pop((i, 1))).astype(BF16)


def _fox(q, k, v, aq, ak, batch, seq):
    blk = pl.BlockSpec((seq, LANES), lambda b, p: (b, p))
    return pl.pallas_call(
        _fox_kernel,
        grid=(batch, D_FOX // LANES),
        in_specs=[blk] * 5,
        out_specs=blk,
        out_shape=jax.ShapeDtypeStruct((batch * seq, D_FOX), BF16),
        scratch_shapes=[pltpu.VMEM((2, seq, LANES), BF16), pltpu.VMEM((2, seq, LANES), BF16)],
        compiler_params=pltpu.CompilerParams(
            dimension_semantics=("parallel", "arbitrary"), vmem_limit_bytes=VMEM_LIMIT),
        name="fox_attn",
    )(q, k, v, aq, ak)


SWA_ROWS = SWA_GROUP * WINDOW
SWA_CHAINS = tuple((p, side) for p in range(D_SWA_KV // LANES) for side in range(2))


def _swa_kernel(sink_ref, q_ref, k_ref, v_ref, o_ref, bias_ref, sink_col_ref):
    seq = q_ref.shape[0]
    nblk = seq // WINDOW
    lane = lax.broadcasted_iota(jnp.int32, (1, LANES), 1)
    lo = lane < HEAD_DIM
    nt = (((1,), (1,)), ((), ()))

    r = lax.broadcasted_iota(jnp.int32, (SWA_ROWS, 2 * WINDOW), 0) % WINDOW
    c = lax.broadcasted_iota(jnp.int32, (SWA_ROWS, 2 * WINDOW), 1)
    bias_ref[...] = jnp.where((c > r) & (c <= r + WINDOW), 0.0, -jnp.inf)
    head = lax.broadcasted_iota(jnp.int32, (SWA_ROWS, LANES), 0) // WINDOW
    for n, (p, side) in enumerate(SWA_CHAINS):
        col = jnp.zeros((SWA_ROWS, LANES), F32)
        for j in range(SWA_GROUP):
            col = jnp.where(head == j, sink_ref[8 * p + 4 * side + j] * LOG2E, col)
        sink_col_ref[n] = col

    def block(q_rows, k_rows, first):
        cols = slice(WINDOW, 2 * WINDOW) if first else slice(0, 2 * WINDOW)

        def scores(n):
            p, side = SWA_CHAINS[n]
            tiles = [q_ref[q_rows, (p * SWA_GROUP + j) * LANES:(p * SWA_GROUP + j + 1) * LANES]
                     for j in range(SWA_GROUP)]
            qst = jnp.concatenate(tiles, axis=0)
            qm = jnp.where(lo == (side == 0), qst, jnp.zeros_like(qst))
            kw = k_ref[k_rows, p * LANES:(p + 1) * LANES]
            return lax.dot_general(qm, kw, nt, preferred_element_type=F32)

        def softmax(n, s):
            s = s + bias_ref[:, cols]
            sk = sink_col_ref[n]
            m = jnp.maximum(jnp.max(s, axis=1, keepdims=True), sk)
            e = jnp.exp2(s - jnp.concatenate([m] * (s.shape[1] // LANES), axis=1))
            l = jnp.sum(e, axis=1, keepdims=True) + jnp.exp2(sk - m)
            return e.astype(BF16), l

        def values(n, e, l):
            p, _ = SWA_CHAINS[n]
            vw = v_ref[k_rows, p * LANES:(p + 1) * LANES]
            return jnp.dot(e, vw, preferred_element_type=F32) / l

        s_buf, e_buf, done = {}, {}, {}
        for t in range(len(SWA_CHAINS) + 2):
            if t < len(SWA_CHAINS):
                s_buf[t] = scores(t)
            if 1 <= t <= len(SWA_CHAINS):
                e_buf[t - 1] = softmax(t - 1, s_buf.pop(t - 1))
            if t >= 2:
                p, side = SWA_CHAINS[t - 2]
                done[side] = values(t - 2, *e_buf.pop(t - 2))
                if side == 1:
                    o_both = jnp.where(lo, done.pop(0), done.pop(1)).astype(BF16)
                    for j in range(SWA_GROUP):
                        tile = p * SWA_GROUP + j
                        o_ref[q_rows, tile * LANES:(tile + 1) * LANES] = o_both[j * WINDOW:(j + 1) * WINDOW, :]

    block(slice(0, WINDOW), slice(0, WINDOW), True)

    def body(n, carry):
        q0 = pl.multiple_of(n * WINDOW, WINDOW)
        k0 = pl.multiple_of((n - 1) * WINDOW, WINDOW)
        block(pl.ds(q0, WINDOW), pl.ds(k0, 2 * WINDOW), False)
        return carry

    lax.fori_loop(1, nblk, body, 0)


def _swa(sinks, q, k, v, batch, seq):
    return pl.pallas_call(
        _swa_kernel,
        grid=(batch,),
        in_specs=[pl.BlockSpec(memory_space=pltpu.SMEM),
                  pl.BlockSpec((seq, D_SWA_Q), lambda b: (b, 0)),
                  pl.BlockSpec((seq, D_SWA_KV), lambda b: (b, 0)),
                  pl.BlockSpec((seq, D_SWA_KV), lambda b: (b, 0))],
        out_specs=pl.BlockSpec((seq, D_SWA_Q), lambda b: (b, 0)),
        out_shape=jax.ShapeDtypeStruct((batch * seq, D_SWA_Q), BF16),
        scratch_shapes=[pltpu.VMEM((SWA_ROWS, 2 * WINDOW), F32),
                        pltpu.VMEM((len(SWA_CHAINS), SWA_ROWS, LANES), F32)],
        compiler_params=pltpu.CompilerParams(dimension_semantics=("parallel",), vmem_limit_bytes=VMEM_LIMIT),
        name="swa_attn",
    )(sinks, q, k, v)


def _outproj_kernel(of_ref, os_ref, h_ref, gf_ref, gs_ref, wf_ref, ws_ref, o_ref, wsp_ref):
    @pl.when(pl.program_id(0) == 0)
    def _():
        for k, h in enumerate(SWA_HEAD_ORDER):
            wsp_ref[k * HEAD_DIM:(k + 1) * HEAD_DIM, :] = ws_ref[h * HEAD_DIM:(h + 1) * HEAD_DIM, :]

    nf = _rms_rows(of_ref[...].astype(F32), gf_ref[...]).astype(BF16)
    ns = _rms_rows(os_ref[...].astype(F32), gs_ref[...]).astype(BF16)
    acc = jnp.dot(nf, wf_ref[...], preferred_element_type=F32)
    acc = acc + jnp.dot(ns, wsp_ref[...], preferred_element_type=F32)
    o_ref[...] = h_ref[...] + acc


def _outproj(of, os_, h, gf, gs, w):
    t, d = h.shape
    row = lambda n: pl.BlockSpec((ROW_TM, n), lambda i: (i, 0))
    const = lambda shape: pl.BlockSpec(shape, lambda i: (0, 0))
    half = lambda n: pl.BlockSpec((D_FOX, d), lambda i: (n, 0), pipeline_mode=pl.Buffered(1))
    return pl.pallas_call(
        _outproj_kernel,
        grid=(t // ROW_TM,),
        in_specs=[row(D_FOX), row(D_SWA_Q), row(d), const((1, D_FOX)), const((1, D_SWA_Q)), half(0), half(1)],
        out_specs=row(d),
        out_shape=jax.ShapeDtypeStruct((t, d), F32),
        scratch_shapes=[pltpu.VMEM((D_SWA_Q, d), BF16)],
        compiler_params=pltpu.CompilerParams(dimension_semantics=("arbitrary",), vmem_limit_bytes=VMEM_LIMIT),
        name="outproj",
    )(of, os_, h, gf, gs, w, w)


def _swa_reorder(a, axis):
    shape = a.shape
    a = a.reshape(shape[:axis] + (2, 2, SWA_GROUP, HEAD_DIM) + shape[axis + 1:])
    a = jnp.swapaxes(a, axis + 1, axis + 2)
    return a.reshape(shape)


def _w_in_head_rows():
    gate0 = 3 * D_FOX
    sq0 = gate0 + N_GATE
    starts = [c for c in range(0, gate0, HEAD_DIM)]
    starts += [sq0 + h * HEAD_DIM for h in SWA_HEAD_ORDER]
    starts += [sq0 + D_SWA_Q + c for c in range(0, 2 * D_SWA_KV, HEAD_DIM)]
    starts += [gate0, gate0]
    assert len(starts) * HEAD_DIM == D_IN_PAD
    return np.asarray(starts, np.int32)


def _w_in_prep_kernel(rows_ref, wt_ref, o_ref):
    t = pl.program_id(0)
    pieces = [wt_ref[pl.ds(pl.multiple_of(rows_ref[2 * t + half], 2 * SUBLANES), HEAD_DIM), :]
              for half in range(2)]
    x = jnp.concatenate(pieces, axis=0)
    row = lax.broadcasted_iota(jnp.int32, (LANES, 1), 0)
    x = jnp.where((t == pl.num_programs(0) - 1) & (row >= N_GATE), 0.0, x)
    o_ref[...] = jnp.transpose(x).astype(BF16)


def _w_in_prep(w_in_stack, layer):
    _, d, n = w_in_stack.shape
    wt = jnp.swapaxes(w_in_stack, 1, 2)
    return pl.pallas_call(
        _w_in_prep_kernel,
        grid=(D_IN_PAD // LANES,),
        in_specs=[pl.BlockSpec(memory_space=pltpu.SMEM),
                  pl.BlockSpec((None, n, d), lambda t: (layer, 0, 0), pipeline_mode=pl.Buffered(1))],
        out_specs=pl.BlockSpec((d, LANES), lambda t: (0, t)),
        out_shape=jax.ShapeDtypeStruct((d, D_IN_PAD), BF16),
        compiler_params=pltpu.CompilerParams(dimension_semantics=("arbitrary",), vmem_limit_bytes=VMEM_LIMIT),
        name="w_in_prep",
    )(jnp.asarray(_w_in_head_rows()), wt)


def _layer(h, pos, invf, p, w_in_stack, layer):
    (norm_ffn1_g, w_gate1, w_up1, w_down1, norm_mix_g, b_forget, fox_q_g, fox_k_g, swa_q_g, swa_k_g,
     sinks, out_fox_g, out_swa_g, w_out, norm_ffn2_g, w_gate2, w_up2, w_down2) = p
    batch_seq, d = h.shape
    batch, seq = pos.shape
    row = lambda v: v.reshape(1, -1).astype(F32)
    tile = lambda v, n: jnp.tile(v.astype(F32), n).reshape(1, -1)

    w_in_r = _w_in_prep(w_in_stack, layer)
    bf_pad = jnp.concatenate([b_forget.astype(F32), jnp.zeros((LANES - N_GATE,), F32)]).reshape(1, LANES)

    h1_head, w_gate1_b, w_up1_b, w_down1_b = _ffn_head(h, row(norm_ffn1_g), w_gate1, w_up1, w_down1)
    h1, (w_gate2_b, w_up2_b, w_down2_b) = _ffn(
        h, row(norm_ffn1_g), w_gate1_b, w_up1_b, w_down1_b, cast=(w_gate2, w_up2, w_down2), head=h1_head)

    qf, kf, vf, qs, ks, vs, lf = _inproj(
        h1, row(norm_mix_g), w_in_r, pos.reshape(batch_seq, 1), invf,
        tile(fox_q_g, FOX_HEADS), tile(fox_k_g, FOX_HEADS), tile(swa_q_g, SWA_Q_HEADS),
        tile(swa_k_g, SWA_KV_HEADS), bf_pad)

    ak, aq = _cumsum(lf, batch, seq)
    o_fox = _fox(qf, kf, vf, aq, ak, batch, seq)
    o_swa = _swa(sinks.astype(F32), qs, ks, vs, batch, seq)

    h2 = _outproj(o_fox, o_swa, h1, row(out_fox_g), row(_swa_reorder(out_swa_g, 0)), w_out.astype(BF16))
    return _ffn(h2, row(norm_ffn2_g), w_gate2_b, w_up2_b, w_down2_b)[0]


def kernel(x, positions, norm_ffn1_g, ffn1_w_gate, ffn1_w_up, ffn1_w_down, norm_mix_g, w_in, b_forget, fox_q_norm_g, fox_k_norm_g, swa_q_norm_g, swa_k_norm_g, swa_sinks, out_norm_fox_g, out_norm_swa_g, w_out, norm_ffn2_g, ffn2_w_gate, ffn2_w_up, ffn2_w_down):
    batch, seq, d = x.shape
    half = jnp.arange(0, HEAD_DIM, 2, dtype=F32)
    inv_freq = ROPE_THETA ** (-half / HEAD_DIM)
    invf = jnp.tile(inv_freq, LANES // (HEAD_DIM // 2)).reshape(1, LANES)
    stacks = (norm_ffn1_g, ffn1_w_gate, ffn1_w_up, ffn1_w_down, norm_mix_g, b_forget, fox_q_norm_g,
              fox_k_norm_g, swa_q_norm_g, swa_k_norm_g, swa_sinks, out_norm_fox_g, out_norm_swa_g, w_out,
              norm_ffn2_g, ffn2_w_gate, ffn2_w_up, ffn2_w_down)
    h = x.reshape(batch * seq, d)
    for layer in range(norm_ffn1_g.shape[0]):
        h = _layer(h, positions, invf, tuple(s[layer] for s in stacks), w_in, layer)
    return h.reshape(batch, seq, d)
```

```python
import functools

import jax
import jax.numpy as jnp
import numpy as np
from jax import lax
from jax.experimental import pallas as pl
from jax.experimental.pallas import tpu as pltpu

F32 = jnp.float32
BF16 = jnp.bfloat16

D_MODEL = 2048
HEAD_DIM = 64
FOX_HEADS = 16
SWA_Q_HEADS = 16
SWA_KV_HEADS = 4
SWA_GROUP = SWA_Q_HEADS // SWA_KV_HEADS
WINDOW = 128
D_FF = 5632
ROPE_THETA = 10000.0
EPS = 1e-6

D_FOX = FOX_HEADS * HEAD_DIM
D_SWA_Q = SWA_Q_HEADS * HEAD_DIM
D_SWA_KV = SWA_KV_HEADS * HEAD_DIM
N_GATE = FOX_HEADS

LANES = 128
SUBLANES = 8
MXU_TILE = 256
VMEM_LIMIT = 60 * 1024 * 1024

C_FQ = 0
C_FK = C_FQ + D_FOX
C_FV = C_FK + D_FOX
C_SQ = C_FV + D_FOX
C_SK = C_SQ + D_SWA_Q
C_SV = C_SK + D_SWA_KV
C_GATE = C_SV + D_SWA_KV
D_IN_PAD = C_GATE + LANES

SWA_HEAD_ORDER = tuple(8 * p + 4 * side + j for p in range(2) for j in range(SWA_GROUP) for side in range(2))

FFN_TM = 1024
FFN_TF = 512
HEAD_TF = 256
HEAD_CHUNKS = 8
ROW_TM = 512
INPROJ_TN = 512
FOX_TQ = 256
CUM_BLK = 256

LOG2E = 1.4426950408889634

AUG_PARTS = 3
ONE_LANE = AUG_PARTS * N_GATE


def _rms_rows(x, g):
    return x * lax.rsqrt(jnp.mean(x * x, axis=-1, keepdims=True) + EPS) * g


def _swiglu_tile(u, wg, wu, wd):
    gate = jnp.dot(u, wg, preferred_element_type=F32)
    up = jnp.dot(u, wu, preferred_element_type=F32)
    a = (gate * jax.nn.sigmoid(gate) * up * 0.5).astype(BF16)
    return jnp.dot(a, wd, preferred_element_type=F32)


def _ffn_first(x_ref, g_ref, u_ref, wg, wu, wd, o_ref):
    x = x_ref[...]
    u = _rms_rows(x, g_ref[...]).astype(BF16)
    u_ref[...] = u
    o_ref[...] = x + _swiglu_tile(u, wg, wu, wd)


def _ffn_step(u_ref, wg, wu, wd, o_ref):
    o_ref[...] += _swiglu_tile(u_ref[...], wg, wu, wd)


def _ffn_kernel(x_ref, g_ref, wg_ref, wu_ref, wd_ref, *refs, n_cast, has_head):
    head_ref = refs[0] if has_head else None
    refs = refs[1:] if has_head else refs
    cast_in, o_ref, cast_out, u_ref = refs[:n_cast], refs[n_cast], refs[n_cast + 1:-1], refs[-1]
    i, j = pl.program_id(0), pl.program_id(1)

    def side_job():
        for src, dst in zip(cast_in, cast_out):
            dst[...] = src[...].astype(BF16)

    if has_head:
        @pl.when(i == 0)
        def _():
            @pl.when(j < HEAD_CHUNKS)
            def _():
                rows = FFN_TM // HEAD_CHUNKS
                o_ref[pl.ds(pl.multiple_of(j * rows, rows), rows), :] = head_ref[...]

            side_job()

        computing = i > 0
    else:
        computing = True

    @pl.when(computing & (j == 0))
    def _():
        _ffn_first(x_ref, g_ref, u_ref, wg_ref[...], wu_ref[...], wd_ref[...], o_ref)
        side_job()

    @pl.when(computing & (j > 0))
    def _():
        _ffn_step(u_ref, wg_ref[...], wu_ref[...], wd_ref[...], o_ref)
        side_job()


def _ffn(x, g, wg, wu, wd, cast=(), head=None):
    t, d = x.shape
    f = wg.shape[1]
    ni, nj = t // FFN_TM, f // FFN_TF
    has_head = head is not None
    live = (lambda i: jnp.minimum(i, 1)) if has_head else (lambda i: 1)
    row = (lambda i: jnp.maximum(i, 1)) if has_head else (lambda i: i)
    cast_specs = []
    for w in cast:
        r, c = w.shape
        if r % ni == 0 and c % nj == 0 and (c // nj) % LANES == 0:
            cast_specs.append(pl.BlockSpec((r // ni, c // nj), lambda i, j: (i, j)))
        elif r % nj == 0 and c % ni == 0 and (c // ni) % LANES == 0:
            cast_specs.append(pl.BlockSpec((r // nj, c // ni), lambda i, j: (j, i)))
        else:
            assert r % ni == 0, w.shape
            cast_specs.append(pl.BlockSpec((r // ni, c), lambda i, j: (i, 0)))
    head_specs, head_args = [], []
    if has_head:
        last = HEAD_CHUNKS - 1
        head_specs = [pl.BlockSpec((FFN_TM // HEAD_CHUNKS, d),
                                   lambda i, j: (jnp.where(i == 0, jnp.minimum(j, last), last), 0))]
        head_args = [head]
    outs = pl.pallas_call(
        functools.partial(_ffn_kernel, n_cast=len(cast), has_head=has_head),
        grid=(ni, nj),
        in_specs=[
            pl.BlockSpec((FFN_TM, d), lambda i, j: (row(i), 0)),
            pl.BlockSpec((1, d), lambda i, j: (0, 0)),
            pl.BlockSpec((d, FFN_TF), lambda i, j: (0, j * live(i))),
            pl.BlockSpec((d, FFN_TF), lambda i, j: (0, j * live(i))),
            pl.BlockSpec((FFN_TF, d), lambda i, j: (j * live(i), 0)),
        ] + head_specs + cast_specs,
        out_specs=[pl.BlockSpec((FFN_TM, d), lambda i, j: (i, 0))] + cast_specs,
        out_shape=[jax.ShapeDtypeStruct((t, d), F32)] + [jax.ShapeDtypeStruct(w.shape, BF16) for w in cast],
        scratch_shapes=[pltpu.VMEM((FFN_TM, d), BF16)],
        compiler_params=pltpu.CompilerParams(
            dimension_semantics=("parallel", "arbitrary"), vmem_limit_bytes=VMEM_LIMIT),
        name="ffn_cast" if cast else "ffn",
    )(x, g, wg, wu, wd, *head_args, *cast)
    return outs[0], tuple(outs[1:])


def _ffn_head_kernel(x_ref, g_ref, wg_ref, wu_ref, wd_ref, o_ref, wgb_ref, wub_ref, wdb_ref, u_ref):
    wgb_ref[...] = wg_ref[...].astype(BF16)
    wub_ref[...] = wu_ref[...].astype(BF16)
    wdb_ref[...] = wd_ref[...].astype(BF16)

    @pl.when(pl.program_id(0) == 0)
    def _():
        _ffn_first(x_ref, g_ref, u_ref, wgb_ref[...], wub_ref[...], wdb_ref[...], o_ref)

    @pl.when(pl.program_id(0) > 0)
    def _():
        _ffn_step(u_ref, wgb_ref[...], wub_ref[...], wdb_ref[...], o_ref)


def _ffn_head(x, g, wg, wu, wd):
    t, d = x.shape
    f = wg.shape[1]
    once = lambda shape: pl.BlockSpec(shape, lambda j: (0, 0), pipeline_mode=pl.Buffered(1))
    col = pl.BlockSpec((d, HEAD_TF), lambda j: (0, j))
    rowb = pl.BlockSpec((HEAD_TF, d), lambda j: (j, 0))
    return pl.pallas_call(
        _ffn_head_kernel,
        grid=(f // HEAD_TF,),
        in_specs=[once((FFN_TM, d)), once((1, d)), col, col, rowb],
        out_specs=[pl.BlockSpec((FFN_TM, d), lambda j: (0, 0)), col, col, rowb],
        out_shape=[jax.ShapeDtypeStruct((FFN_TM, d), F32), jax.ShapeDtypeStruct(wg.shape, BF16),
                   jax.ShapeDtypeStruct(wu.shape, BF16), jax.ShapeDtypeStruct(wd.shape, BF16)],
        scratch_shapes=[pltpu.VMEM((FFN_TM, d), BF16)],
        compiler_params=pltpu.CompilerParams(dimension_semantics=("arbitrary",), vmem_limit_bytes=VMEM_LIMIT),
        name="ffn_head",
    )(x, g, wg, wu, wd)


def _head_rms(y, gain, bd):
    n = y.shape[1]
    y2 = (y * y).astype(BF16)
    parts = [jnp.dot(y2[:, c * MXU_TILE:(c + 1) * MXU_TILE], bd, preferred_element_type=F32)
             for c in range(n // MXU_TILE)]
    ss = parts[0] if len(parts) == 1 else jnp.concatenate(parts, axis=1)
    return y * lax.rsqrt(ss * (1.0 / HEAD_DIM) + EPS) * gain


def _rope(y, cos, sin_signed, first_half):
    outs = []
    for c in range(y.shape[1] // LANES):
        yc = y[:, c * LANES:(c + 1) * LANES]
        partner = jnp.where(first_half, pltpu.roll(yc, LANES - HEAD_DIM // 2, 1),
                            pltpu.roll(yc, HEAD_DIM // 2, 1))
        outs.append(yc * cos + partner * sin_signed)
    return outs[0] if len(outs) == 1 else jnp.concatenate(outs, axis=1)


def _inproj_kernel(h_ref, g_ref, w_ref, pos_ref, invf_ref, gfq_ref, gfk_ref, gsq_ref, gsk_ref, bf_ref,
                   qf_ref, kf_ref, vf_ref, qs_ref, ks_ref, vs_ref, lf_ref):
    u = _rms_rows(h_ref[...], g_ref[...]).astype(BF16)

    ang = pos_ref[...].astype(F32) * invf_ref[...]
    lane = lax.broadcasted_iota(jnp.int32, (1, LANES), 1)
    first_half = (lane % HEAD_DIM) < (HEAD_DIM // 2)
    cos = jnp.cos(ang)
    sin = jnp.sin(ang)
    sin_signed = jnp.where(first_half, -sin, sin)

    r = lax.broadcasted_iota(jnp.int32, (MXU_TILE, MXU_TILE), 0) // HEAD_DIM
    c = lax.broadcasted_iota(jnp.int32, (MXU_TILE, MXU_TILE), 1) // HEAD_DIM
    bd = jnp.where(r == c, 1.0, 0.0).astype(BF16)

    scale = HEAD_DIM ** -0.5

    def fox_q(y, lo, hi):
        qf_ref[:, lo:hi] = (_head_rms(y, gfq_ref[:, lo:hi], bd) * (scale * LOG2E)).astype(BF16)

    def fox_k(y, lo, hi):
        kf_ref[:, lo:hi] = _head_rms(y, gfk_ref[:, lo:hi], bd).astype(BF16)

    def fox_v(y, lo, hi):
        vf_ref[:, lo:hi] = y.astype(BF16)

    def swa_q(y, lo, hi):
        y = _rope(_head_rms(y, gsq_ref[:, lo:hi], bd), cos, sin_signed, first_half)
        qs_ref[:, lo:hi] = (y * (scale * LOG2E)).astype(BF16)

    def swa_k(y, lo, hi):
        ks_ref[:, lo:hi] = _rope(_head_rms(y, gsk_ref[:, lo:hi], bd), cos, sin_signed, first_half).astype(BF16)

    def swa_v(y, lo, hi):
        vs_ref[:, lo:hi] = y.astype(BF16)

    def gate(y, lo, hi):
        z = y + bf_ref[...]
        lf_ref[...] = jnp.minimum(z, 0.0) - jnp.log1p(jnp.exp(-jnp.abs(z)))

    stages = []
    for start, end, epilogue in ((C_FQ, C_FK, fox_q), (C_FK, C_FV, fox_k), (C_FV, C_SQ, fox_v),
                                 (C_SQ, C_SK, swa_q), (C_SK, C_SV, swa_k), (C_SV, C_GATE, swa_v),
                                 (C_GATE, D_IN_PAD, gate)):
        for lo in range(start, end, INPROJ_TN):
            hi = min(lo + INPROJ_TN, end)
            stages.append((lo, hi, lo - start, hi - start, epilogue))
    pending = None
    for lo, hi, olo, ohi, epilogue in stages:
        y = jnp.dot(u, w_ref[:, lo:hi], preferred_element_type=F32)
        if pending is not None:
            pending[0](*pending[1:])
        pending = (epilogue, y, olo, ohi)
    pending[0](*pending[1:])


def _inproj(h, g, w, pos, invf, gfq, gfk, gsq, gsk, bf):
    t, d = h.shape
    row = lambda n: pl.BlockSpec((ROW_TM, n), lambda i: (i, 0))
    const = lambda shape: pl.BlockSpec(shape, lambda i: (0, 0))
    return pl.pallas_call(
        _inproj_kernel,
        grid=(t // ROW_TM,),
        in_specs=[
            row(d),
            const((1, d)),
            pl.BlockSpec((d, D_IN_PAD), lambda i: (0, 0), pipeline_mode=pl.Buffered(1)),
            row(1),
            const((1, LANES)),
            const((1, D_FOX)), const((1, D_FOX)), const((1, D_SWA_Q)), const((1, D_SWA_KV)),
            const((1, LANES)),
        ],
        out_specs=[row(D_FOX), row(D_FOX), row(D_FOX), row(D_SWA_Q), row(D_SWA_KV), row(D_SWA_KV), row(LANES)],
        out_shape=[
            jax.ShapeDtypeStruct((t, D_FOX), BF16), jax.ShapeDtypeStruct((t, D_FOX), BF16),
            jax.ShapeDtypeStruct((t, D_FOX), BF16), jax.ShapeDtypeStruct((t, D_SWA_Q), BF16),
            jax.ShapeDtypeStruct((t, D_SWA_KV), BF16), jax.ShapeDtypeStruct((t, D_SWA_KV), BF16),
            jax.ShapeDtypeStruct((t, LANES), F32),
        ],
        compiler_params=pltpu.CompilerParams(dimension_semantics=("parallel",), vmem_limit_bytes=VMEM_LIMIT),
        name="inproj",
    )(h, g, w, pos, invf, gfq, gfk, gsq, gsk, bf)


def _split3(x):
    p1 = x.astype(BF16)
    r1 = x - p1.astype(F32)
    p2 = r1.astype(BF16)
    p3 = (r1 - p2.astype(F32)).astype(BF16)
    return p1, p2, p3


def _cumsum_kernel(lf_ref, pmap_ref, ak_ref, aq_ref):
    s = lf_ref.shape[0]
    r = lax.broadcasted_iota(jnp.int32, (CUM_BLK, CUM_BLK), 0)
    c = lax.broadcasted_iota(jnp.int32, (CUM_BLK, CUM_BLK), 1)
    tri = jnp.where(r >= c, 1.0, 0.0).astype(BF16)
    lane = lax.broadcasted_iota(jnp.int32, (1, LANES), 1)
    is_gate = lane < N_GATE
    one = jnp.where(lane == ONE_LANE, 1.0, 0.0)
    carry = jnp.zeros((1, LANES), F32)
    for b in range(s // CUM_BLK):
        rows = slice(b * CUM_BLK, (b + 1) * CUM_BLK)
        p1, p2, p3 = _split3(lf_ref[rows, :])
        cb = (jnp.dot(tri, p1, preferred_element_type=F32)
              + jnp.dot(tri, p2, preferred_element_type=F32)
              + jnp.dot(tri, p3, preferred_element_type=F32)) + carry
        carry = cb[CUM_BLK - 1:CUM_BLK, :]
        terms = [jnp.where(is_gate, t.astype(F32), 0.0) for t in _split3(cb * LOG2E)]
        packed = one + terms[0]
        for i in range(1, AUG_PARTS):
            packed = packed + pltpu.roll(terms[i], i * N_GATE, 1)
        placed = jnp.dot(packed.astype(BF16), pmap_ref[...], preferred_element_type=F32)
        ak_ref[rows, :] = placed[:, :D_FOX].astype(BF16)
        aq_ref[rows, :] = placed[:, D_FOX:].astype(BF16)


def _aug_map():
    pmap = np.zeros((LANES, 2 * D_FOX), np.float32)
    for h in range(FOX_HEADS):
        base = (h // 2) * LANES + (HEAD_DIM if h % 2 == 0 else 0)
        for i in range(AUG_PARTS):
            pmap[i * N_GATE + h, base + i] = -1.0
            pmap[ONE_LANE, D_FOX + base + i] = 1.0
        pmap[ONE_LANE, base + AUG_PARTS] = 1.0
        pmap[h, D_FOX + base + AUG_PARTS] = 1.0
    return jnp.asarray(pmap, BF16)


def _cumsum(lf, batch, seq):
    pmap = _aug_map()
    out = pl.BlockSpec((seq, D_FOX), lambda b: (b, 0))
    return pl.pallas_call(
        _cumsum_kernel,
        grid=(batch,),
        in_specs=[pl.BlockSpec((seq, LANES), lambda b: (b, 0)), pl.BlockSpec(pmap.shape, lambda b: (0, 0))],
        out_specs=[out, out],
        out_shape=[jax.ShapeDtypeStruct((batch * seq, D_FOX), BF16)] * 2,
        compiler_params=pltpu.CompilerParams(dimension_semantics=("parallel",)),
        name="gate_cumsum",
    )(lf, pmap)


def _fox_kernel(q_ref, k_ref, v_ref, aq_ref, ak_ref, o_ref, qa_ref, ka_ref):
    seq = q_ref.shape[0]
    lane = lax.broadcasted_iota(jnp.int32, (1, LANES), 1)
    lo = lane < HEAD_DIM
    q, k, aq, ak = q_ref[...], k_ref[...], aq_ref[...], ak_ref[...]
    qa_ref[0] = jnp.where(lo, q, aq)
    qa_ref[1] = jnp.where(lo, aq, q)
    ka_ref[0] = jnp.where(lo, k, ak)
    ka_ref[1] = jnp.where(lo, ak, k)

    qry = lax.broadcasted_iota(jnp.int32, (FOX_TQ, FOX_TQ), 0)
    key = lax.broadcasted_iota(jnp.int32, (FOX_TQ, FOX_TQ), 1)
    causal = key <= qry
    nt = (((1,), (1,)), ((), ()))

    def scores(i, side):
        rows = slice(i * FOX_TQ, (i + 1) * FOX_TQ)
        return lax.dot_general(qa_ref[side, rows, :], ka_ref[side, 0:(i + 1) * FOX_TQ, :], nt,
                               preferred_element_type=F32)

    def softmax(i, s):
        w = (i + 1) * FOX_TQ
        diag = jnp.where(causal, s[:, w - FOX_TQ:], -jnp.inf)
        s = diag if i == 0 else jnp.concatenate([s[:, :w - FOX_TQ], diag], axis=1)
        m = jnp.max(s, axis=1, keepdims=True)
        p = jnp.exp2(s - m)
        return p.astype(BF16), jnp.sum(p, axis=1, keepdims=True)

    def values(i, p, l):
        o = jnp.dot(p, v_ref[0:(i + 1) * FOX_TQ, :], preferred_element_type=F32)
        return o / l

    chains = [(i, side) for i in reversed(range(seq // FOX_TQ)) for side in range(2)]
    s_buf, p_buf, done = {}, {}, {}
    for t in range(len(chains) + 2):
        if t < len(chains):
            s_buf[t] = scores(*chains[t])
        if 1 <= t <= len(chains):
            p_buf[t - 1] = softmax(chains[t - 1][0], s_buf.pop(t - 1))
        if t >= 2:
            i, side = chains[t - 2]
            done[(i, side)] = values(i, *p_buf.pop(t - 2))
            if side == 1:
                rows = slice(i * FOX_TQ, (i + 1) * FOX_TQ)
                o_ref[rows, :] = jnp.where(lo, done.pop((i, 0)), done.pop((i, 1))).astype(BF16)


def _fox(q, k, v, aq, ak, batch, seq):
    blk = pl.BlockSpec((seq, LANES), lambda b, p: (b, p))
    return pl.pallas_call(
        _fox_kernel,
        grid=(batch, D_FOX // LANES),
        in_specs=[blk] * 5,
        out_specs=blk,
        out_shape=jax.ShapeDtypeStruct((batch * seq, D_FOX), BF16),
        scratch_shapes=[pltpu.VMEM((2, seq, LANES), BF16), pltpu.VMEM((2, seq, LANES), BF16)],
        compiler_params=pltpu.CompilerParams(
            dimension_semantics=("parallel", "arbitrary"), vmem_limit_bytes=VMEM_LIMIT),
        name="fox_attn",
    )(q, k, v, aq, ak)


SWA_ROWS = SWA_GROUP * WINDOW
SWA_CHAINS = tuple((p, side) for p in range(D_SWA_KV // LANES) for side in range(2))


def _swa_kernel(sink_ref, q_ref, k_ref, v_ref, o_ref, bias_ref, sink_col_ref):
    seq = q_ref.shape[0]
    nblk = seq // WINDOW
    lane = lax.broadcasted_iota(jnp.int32, (1, LANES), 1)
    lo = lane < HEAD_DIM
    nt = (((1,), (1,)), ((), ()))

    r = lax.broadcasted_iota(jnp.int32, (SWA_ROWS, 2 * WINDOW), 0) % WINDOW
    c = lax.broadcasted_iota(jnp.int32, (SWA_ROWS, 2 * WINDOW), 1)
    bias_ref[...] = jnp.where((c > r) & (c <= r + WINDOW), 0.0, -jnp.inf)
    head = lax.broadcasted_iota(jnp.int32, (SWA_ROWS, LANES), 0) // WINDOW
    for n, (p, side) in enumerate(SWA_CHAINS):
        col = jnp.zeros((SWA_ROWS, LANES), F32)
        for j in range(SWA_GROUP):
            col = jnp.where(head == j, sink_ref[8 * p + 4 * side + j] * LOG2E, col)
        sink_col_ref[n] = col

    def block(q_rows, k_rows, first):
        cols = slice(WINDOW, 2 * WINDOW) if first else slice(0, 2 * WINDOW)

        def scores(n):
            p, side = SWA_CHAINS[n]
            tiles = [q_ref[q_rows, (p * SWA_GROUP + j) * LANES:(p * SWA_GROUP + j + 1) * LANES]
                     for j in range(SWA_GROUP)]
            qst = jnp.concatenate(tiles, axis=0)
            qm = jnp.where(lo == (side == 0), qst, jnp.zeros_like(qst))
            kw = k_ref[k_rows, p * LANES:(p + 1) * LANES]
            return lax.dot_general(qm, kw, nt, preferred_element_type=F32)

        def softmax(n, s):
            s = s + bias_ref[:, cols]
            sk = sink_col_ref[n]
            m = jnp.maximum(jnp.max(s, axis=1, keepdims=True), sk)
            e = jnp.exp2(s - jnp.concatenate([m] * (s.shape[1] // LANES), axis=1))
            l = jnp.sum(e, axis=1, keepdims=True) + jnp.exp2(sk - m)
            return e.astype(BF16), l

        def values(n, e, l):
            p, _ = SWA_CHAINS[n]
            vw = v_ref[k_rows, p * LANES:(p + 1) * LANES]
            return jnp.dot(e, vw, preferred_element_type=F32) / l

        s_buf, e_buf, done = {}, {}, {}
        for t in range(len(SWA_CHAINS) + 2):
            if t < len(SWA_CHAINS):
                s_buf[t] = scores(t)
            if 1 <= t <= len(SWA_CHAINS):
                e_buf[t - 1] = softmax(t - 1, s_buf.pop(t - 1))
            if t >= 2:
                p, side = SWA_CHAINS[t - 2]
                done[side] = values(t - 2, *e_buf.pop(t - 2))
                if side == 1:
                    o_both = jnp.where(lo, done.pop(0), done.pop(1)).astype(BF16)
                    for j in range(SWA_GROUP):
                        tile = p * SWA_GROUP + j
                        o_ref[q_rows, tile * LANES:(tile + 1) * LANES] = o_both[j * WINDOW:(j + 1) * WINDOW, :]

    block(slice(0, WINDOW), slice(0, WINDOW), True)

    def body(n, carry):
        q0 = pl.multiple_of(n * WINDOW, WINDOW)
        k0 = pl.multiple_of((n - 1) * WINDOW, WINDOW)
        block(pl.ds(q0, WINDOW), pl.ds(k0, 2 * WINDOW), False)
        return carry

    lax.fori_loop(1, nblk, body, 0)


def _swa(sinks, q, k, v, batch, seq):
    return pl.pallas_call(
        _swa_kernel,
        grid=(batch,),
        in_specs=[pl.BlockSpec(memory_space=pltpu.SMEM),
                  pl.BlockSpec((seq, D_SWA_Q), lambda b: (b, 0)),
                  pl.BlockSpec((seq, D_SWA_KV), lambda b: (b, 0)),
                  pl.BlockSpec((seq, D_SWA_KV), lambda b: (b, 0))],
        out_specs=pl.BlockSpec((seq, D_SWA_Q), lambda b: (b, 0)),
        out_shape=jax.ShapeDtypeStruct((batch * seq, D_SWA_Q), BF16),
        scratch_shapes=[pltpu.VMEM((SWA_ROWS, 2 * WINDOW), F32),
                        pltpu.VMEM((len(SWA_CHAINS), SWA_ROWS, LANES), F32)],
        compiler_params=pltpu.CompilerParams(dimension_semantics=("parallel",), vmem_limit_bytes=VMEM_LIMIT),
        name="swa_attn",
    )(sinks, q, k, v)


def _outproj_kernel(of_ref, os_ref, h_ref, gf_ref, gs_ref, wf_ref, ws_ref, o_ref, wsp_ref):
    @pl.when(pl.program_id(0) == 0)
    def _():
        for k, h in enumerate(SWA_HEAD_ORDER):
            wsp_ref[k * HEAD_DIM:(k + 1) * HEAD_DIM, :] = ws_ref[h * HEAD_DIM:(h + 1) * HEAD_DIM, :]

    nf = _rms_rows(of_ref[...].astype(F32), gf_ref[...]).astype(BF16)
    ns = _rms_rows(os_ref[...].astype(F32), gs_ref[...]).astype(BF16)
    acc = jnp.dot(nf, wf_ref[...], preferred_element_type=F32)
    acc = acc + jnp.dot(ns, wsp_ref[...], preferred_element_type=F32)
    o_ref[...] = h_ref[...] + acc


def _outproj(of, os_, h, gf, gs, w):
    t, d = h.shape
    row = lambda n: pl.BlockSpec((ROW_TM, n), lambda i: (i, 0))
    const = lambda shape: pl.BlockSpec(shape, lambda i: (0, 0))
    half = lambda n: pl.BlockSpec((D_FOX, d), lambda i: (n, 0), pipeline_mode=pl.Buffered(1))
    return pl.pallas_call(
        _outproj_kernel,
        grid=(t // ROW_TM,),
        in_specs=[row(D_FOX), row(D_SWA_Q), row(d), const((1, D_FOX)), const((1, D_SWA_Q)), half(0), half(1)],
        out_specs=row(d),
        out_shape=jax.ShapeDtypeStruct((t, d), F32),
        scratch_shapes=[pltpu.VMEM((D_SWA_Q, d), BF16)],
        compiler_params=pltpu.CompilerParams(dimension_semantics=("arbitrary",), vmem_limit_bytes=VMEM_LIMIT),
        name="outproj",
    )(of, os_, h, gf, gs, w, w)


def _swa_reorder(a, axis):
    shape = a.shape
    a = a.reshape(shape[:axis] + (2, 2, SWA_GROUP, HEAD_DIM) + shape[axis + 1:])
    a = jnp.swapaxes(a, axis + 1, axis + 2)
    return a.reshape(shape)


def _w_in_head_rows():
    gate0 = 3 * D_FOX
    sq0 = gate0 + N_GATE
    starts = [c for c in range(0, gate0, HEAD_DIM)]
    starts += [sq0 + h * HEAD_DIM for h in SWA_HEAD_ORDER]
    starts += [sq0 + D_SWA_Q + c for c in range(0, 2 * D_SWA_KV, HEAD_DIM)]
    starts += [gate0, gate0]
    assert len(starts) * HEAD_DIM == D_IN_PAD
    return np.asarray(starts, np.int32)


def _w_in_prep_kernel(rows_ref, wt_ref, o_ref):
    t = pl.program_id(0)
    pieces = [wt_ref[pl.ds(pl.multiple_of(rows_ref[2 * t + half], 2 * SUBLANES), HEAD_DIM), :]
              for half in range(2)]
    x = jnp.concatenate(pieces, axis=0)
    row = lax.broadcasted_iota(jnp.int32, (LANES, 1), 0)
    x = jnp.where((t == pl.num_programs(0) - 1) & (row >= N_GATE), 0.0, x)
    o_ref[...] = jnp.transpose(x).astype(BF16)


def _w_in_prep(w_in_stack, layer):
    _, d, n = w_in_stack.shape
    wt = jnp.swapaxes(w_in_stack, 1, 2)
    return pl.pallas_call(
        _w_in_prep_kernel,
        grid=(D_IN_PAD // LANES,),
        in_specs=[pl.BlockSpec(memory_space=pltpu.SMEM),
                  pl.BlockSpec((None, n, d), lambda t: (layer, 0, 0), pipeline_mode=pl.Buffered(1))],
        out_specs=pl.BlockSpec((d, LANES), lambda t: (0, t)),
        out_shape=jax.ShapeDtypeStruct((d, D_IN_PAD), BF16),
        compiler_params=pltpu.CompilerParams(dimension_semantics=("arbitrary",), vmem_limit_bytes=VMEM_LIMIT),
        name="w_in_prep",
    )(jnp.asarray(_w_in_head_rows()), wt)


def _layer(h, pos, invf, p, w_in_stack, layer):
    (norm_ffn1_g, w_gate1, w_up1, w_down1, norm_mix_g, b_forget, fox_q_g, fox_k_g, swa_q_g, swa_k_g,
     sinks, out_fox_g, out_swa_g, w_out, norm_ffn2_g, w_gate2, w_up2, w_down2) = p
    batch_seq, d = h.shape
    batch, seq = pos.shape
    row = lambda v: v.reshape(1, -1).astype(F32)
    tile = lambda v, n: jnp.tile(v.astype(F32), n).reshape(1, -1)

    w_in_r = _w_in_prep(w_in_stack, layer)
    bf_pad = jnp.concatenate([b_forget.astype(F32), jnp.zeros((LANES - N_GATE,), F32)]).reshape(1, LANES)

    h1_head, w_gate1_b, w_up1_b, w_down1_b = _ffn_head(h, row(norm_ffn1_g), w_gate1, w_up1, w_down1)
    h1, (w_gate2_b, w_up2_b, w_down2_b) = _ffn(
        h, row(norm_ffn1_g), w_gate1_b, w_up1_b, w_down1_b, cast=(w_gate2, w_up2, w_down2), head=h1_head)

    qf, kf, vf, qs, ks, vs, lf = _inproj(
        h1, row(norm_mix_g), w_in_r, pos.reshape(batch_seq, 1), invf,
        tile(fox_q_g, FOX_HEADS), tile(fox_k_g, FOX_HEADS), tile(swa_q_g, SWA_Q_HEADS),
        tile(swa_k_g, SWA_KV_HEADS), bf_pad)

    ak, aq = _cumsum(lf, batch, seq)
    o_fox = _fox(qf, kf, vf, aq, ak, batch, seq)
    o_swa = _swa(sinks.astype(F32), qs, ks, vs, batch, seq)

    h2 = _outproj(o_fox, o_swa, h1, row(out_fox_g), row(_swa_reorder(out_swa_g, 0)), w_out.astype(BF16))
    return _ffn(h2, row(norm_ffn2_g), w_gate2_b, w_up2_b, w_down2_b)[0]


def kernel(x, positions, norm_ffn1_g, ffn1_w_gate, ffn1_w_up, ffn1_w_down, norm_mix_g, w_in, b_forget, fox_q_norm_g, fox_k_norm_g, swa_q_norm_g, swa_k_norm_g, swa_sinks, out_norm_fox_g, out_norm_swa_g, w_out, norm_ffn2_g, ffn2_w_gate, ffn2_w_up, ffn2_w_down):
    batch, seq, d = x.shape
    half = jnp.arange(0, HEAD_DIM, 2, dtype=F32)
    inv_freq = ROPE_THETA ** (-half / HEAD_DIM)
    invf = jnp.tile(inv_freq, LANES // (HEAD_DIM // 2)).reshape(1, LANES)
    stacks = (norm_ffn1_g, ffn1_w_gate, ffn1_w_up, ffn1_w_down, norm_mix_g, b_forget, fox_q_norm_g,
              fox_k_norm_g, swa_q_norm_g, swa_k_norm_g, swa_sinks, out_norm_fox_g, out_norm_swa_g, w_out,
              norm_ffn2_g, ffn2_w_gate, ffn2_w_up, ffn2_w_down)
    h = x.reshape(batch * seq, d)
    for layer in range(norm_ffn1_g.shape[0]):
        h = _layer(h, positions, invf, tuple(s[layer] for s in stacks), w_in, layer)
    return h.reshape(batch, seq, d)
```

```python
import functools

import jax
import jax.numpy as jnp
import numpy as np
from jax import lax
from jax.experimental import pallas as pl
from jax.experimental.pallas import tpu as pltpu

F32 = jnp.float32
BF16 = jnp.bfloat16

D_MODEL = 2048
HEAD_DIM = 64
FOX_HEADS = 16
SWA_Q_HEADS = 16
SWA_KV_HEADS = 4
SWA_GROUP = SWA_Q_HEADS // SWA_KV_HEADS
WINDOW = 128
D_FF = 5632
ROPE_THETA = 10000.0
EPS = 1e-6

D_FOX = FOX_HEADS * HEAD_DIM
D_SWA_Q = SWA_Q_HEADS * HEAD_DIM
D_SWA_KV = SWA_KV_HEADS * HEAD_DIM
N_GATE = FOX_HEADS

LANES = 128
SUBLANES = 8
MXU_TILE = 256
VMEM_LIMIT = 60 * 1024 * 1024

C_FQ = 0
C_FK = C_FQ + D_FOX
C_FV = C_FK + D_FOX
C_SQ = C_FV + D_FOX
C_SK = C_SQ + D_SWA_Q
C_SV = C_SK + D_SWA_KV
C_GATE = C_SV + D_SWA_KV
D_IN_PAD = C_GATE + LANES

SWA_HEAD_ORDER = tuple(8 * p + 4 * side + j for p in range(2) for j in range(SWA_GROUP) for side in range(2))

FFN_TM = 1024
FFN_TF = 512
HEAD_TF = 256
HEAD_CHUNKS = 8
ROW_TM = 512
INPROJ_TN = 512
INPROJ_DEPTH = 1
FOX_TQ = 256
FOX_LAG_S, FOX_LAG_V = 1, 2
CUM_BLK = 256

LOG2E = 1.4426950408889634

AUG_PARTS = 3
ONE_LANE = AUG_PARTS * N_GATE


def _rms_rows(x, g):
    return x * lax.rsqrt(jnp.mean(x * x, axis=-1, keepdims=True) + EPS) * g


def _swiglu_tile(u, wg, wu, wd):
    gate = jnp.dot(u, wg, preferred_element_type=F32)
    up = jnp.dot(u, wu, preferred_element_type=F32)
    a = (gate * jax.nn.sigmoid(gate) * up * 0.5).astype(BF16)
    return jnp.dot(a, wd, preferred_element_type=F32)


def _ffn_first(x_ref, g_ref, u_ref, wg, wu, wd, o_ref):
    x = x_ref[...]
    u = _rms_rows(x, g_ref[...]).astype(BF16)
    u_ref[...] = u
    o_ref[...] = x + _swiglu_tile(u, wg, wu, wd)


def _ffn_step(u_ref, wg, wu, wd, o_ref):
    o_ref[...] += _swiglu_tile(u_ref[...], wg, wu, wd)


def _ffn_kernel(x_ref, g_ref, wg_ref, wu_ref, wd_ref, *refs, n_cast, has_head):
    head_ref = refs[0] if has_head else None
    refs = refs[1:] if has_head else refs
    cast_in, o_ref, cast_out, u_ref = refs[:n_cast], refs[n_cast], refs[n_cast + 1:-1], refs[-1]
    i, j = pl.program_id(0), pl.program_id(1)

    def side_job():
        for src, dst in zip(cast_in, cast_out):
            dst[...] = src[...].astype(BF16)

    if has_head:
        @pl.when(i == 0)
        def _():
            @pl.when(j < HEAD_CHUNKS)
            def _():
                rows = FFN_TM // HEAD_CHUNKS
                o_ref[pl.ds(pl.multiple_of(j * rows, rows), rows), :] = head_ref[...]

            side_job()

        computing = i > 0
    else:
        computing = True

    @pl.when(computing & (j == 0))
    def _():
        _ffn_first(x_ref, g_ref, u_ref, wg_ref[...], wu_ref[...], wd_ref[...], o_ref)
        side_job()

    @pl.when(computing & (j > 0))
    def _():
        _ffn_step(u_ref, wg_ref[...], wu_ref[...], wd_ref[...], o_ref)
        side_job()


def _ffn(x, g, wg, wu, wd, cast=(), head=None):
    t, d = x.shape
    f = wg.shape[1]
    ni, nj = t // FFN_TM, f // FFN_TF
    has_head = head is not None
    live = (lambda i: jnp.minimum(i, 1)) if has_head else (lambda i: 1)
    row = (lambda i: jnp.maximum(i, 1)) if has_head else (lambda i: i)
    cast_specs = []
    for w in cast:
        r, c = w.shape
        if r % ni == 0 and c % nj == 0 and (c // nj) % LANES == 0:
            cast_specs.append(pl.BlockSpec((r // ni, c // nj), lambda i, j: (i, j)))
        elif r % nj == 0 and c % ni == 0 and (c // ni) % LANES == 0:
            cast_specs.append(pl.BlockSpec((r // nj, c // ni), lambda i, j: (j, i)))
        else:
            assert r % ni == 0, w.shape
            cast_specs.append(pl.BlockSpec((r // ni, c), lambda i, j: (i, 0)))
    head_specs, head_args = [], []
    if has_head:
        last = HEAD_CHUNKS - 1
        head_specs = [pl.BlockSpec((FFN_TM // HEAD_CHUNKS, d),
                                   lambda i, j: (jnp.where(i == 0, jnp.minimum(j, last), last), 0))]
        head_args = [head]
    outs = pl.pallas_call(
        functools.partial(_ffn_kernel, n_cast=len(cast), has_head=has_head),
        grid=(ni, nj),
        in_specs=[
            pl.BlockSpec((FFN_TM, d), lambda i, j: (row(i), 0)),
            pl.BlockSpec((1, d), lambda i, j: (0, 0)),
            pl.BlockSpec((d, FFN_TF), lambda i, j: (0, j * live(i))),
            pl.BlockSpec((d, FFN_TF), lambda i, j: (0, j * live(i))),
            pl.BlockSpec((FFN_TF, d), lambda i, j: (j * live(i), 0)),
        ] + head_specs + cast_specs,
        out_specs=[pl.BlockSpec((FFN_TM, d), lambda i, j: (i, 0))] + cast_specs,
        out_shape=[jax.ShapeDtypeStruct((t, d), F32)] + [jax.ShapeDtypeStruct(w.shape, BF16) for w in cast],
        scratch_shapes=[pltpu.VMEM((FFN_TM, d), BF16)],
        compiler_params=pltpu.CompilerParams(
            dimension_semantics=("parallel", "arbitrary"), vmem_limit_bytes=VMEM_LIMIT),
        name="ffn_cast" if cast else "ffn",
    )(x, g, wg, wu, wd, *head_args, *cast)
    return outs[0], tuple(outs[1:])


def _ffn_head_kernel(x_ref, g_ref, wg_ref, wu_ref, wd_ref, o_ref, wgb_ref, wub_ref, wdb_ref, u_ref):
    wgb_ref[...] = wg_ref[...].astype(BF16)
    wub_ref[...] = wu_ref[...].astype(BF16)
    wdb_ref[...] = wd_ref[...].astype(BF16)

    @pl.when(pl.program_id(0) == 0)
    def _():
        _ffn_first(x_ref, g_ref, u_ref, wgb_ref[...], wub_ref[...], wdb_ref[...], o_ref)

    @pl.when(pl.program_id(0) > 0)
    def _():
        _ffn_step(u_ref, wgb_ref[...], wub_ref[...], wdb_ref[...], o_ref)


def _ffn_head(x, g, wg, wu, wd):
    t, d = x.shape
    f = wg.shape[1]
    once = lambda shape: pl.BlockSpec(shape, lambda j: (0, 0), pipeline_mode=pl.Buffered(1))
    col = pl.BlockSpec((d, HEAD_TF), lambda j: (0, j))
    rowb = pl.BlockSpec((HEAD_TF, d), lambda j: (j, 0))
    return pl.pallas_call(
        _ffn_head_kernel,
        grid=(f // HEAD_TF,),
        in_specs=[once((FFN_TM, d)), once((1, d)), col, col, rowb],
        out_specs=[pl.BlockSpec((FFN_TM, d), lambda j: (0, 0)), col, col, rowb],
        out_shape=[jax.ShapeDtypeStruct((FFN_TM, d), F32), jax.ShapeDtypeStruct(wg.shape, BF16),
                   jax.ShapeDtypeStruct(wu.shape, BF16), jax.ShapeDtypeStruct(wd.shape, BF16)],
        scratch_shapes=[pltpu.VMEM((FFN_TM, d), BF16)],
        compiler_params=pltpu.CompilerParams(dimension_semantics=("arbitrary",), vmem_limit_bytes=VMEM_LIMIT),
        name="ffn_head",
    )(x, g, wg, wu, wd)


def _head_rms(y, gain, bd):
    n = y.shape[1]
    y2 = (y * y).astype(BF16)
    parts = [jnp.dot(y2[:, c * MXU_TILE:(c + 1) * MXU_TILE], bd, preferred_element_type=F32)
             for c in range(n // MXU_TILE)]
    ss = parts[0] if len(parts) == 1 else jnp.concatenate(parts, axis=1)
    return y * lax.rsqrt(ss * (1.0 / HEAD_DIM) + EPS) * gain


def _rope(y, cos, sin_signed, first_half):
    outs = []
    for c in range(y.shape[1] // LANES):
        yc = y[:, c * LANES:(c + 1) * LANES]
        partner = jnp.where(first_half, pltpu.roll(yc, LANES - HEAD_DIM // 2, 1),
                            pltpu.roll(yc, HEAD_DIM // 2, 1))
        outs.append(yc * cos + partner * sin_signed)
    return outs[0] if len(outs) == 1 else jnp.concatenate(outs, axis=1)


def _inproj_kernel(h_ref, g_ref, w_ref, pos_ref, invf_ref, gfq_ref, gfk_ref, gsq_ref, gsk_ref, bf_ref,
                   qf_ref, kf_ref, vf_ref, qs_ref, ks_ref, vs_ref, lf_ref):
    u = _rms_rows(h_ref[...], g_ref[...]).astype(BF16)

    lane = lax.broadcasted_iota(jnp.int32, (1, LANES), 1)
    first_half = (lane % HEAD_DIM) < (HEAD_DIM // 2)
    rope_tables = []

    def rope_table():
        if not rope_tables:
            ang = pos_ref[...].astype(F32) * invf_ref[...]
            sin = jnp.sin(ang)
            rope_tables.append((jnp.cos(ang), jnp.where(first_half, -sin, sin)))
        return rope_tables[0]

    r = lax.broadcasted_iota(jnp.int32, (MXU_TILE, MXU_TILE), 0) // HEAD_DIM
    c = lax.broadcasted_iota(jnp.int32, (MXU_TILE, MXU_TILE), 1) // HEAD_DIM
    bd = jnp.where(r == c, 1.0, 0.0).astype(BF16)

    scale = HEAD_DIM ** -0.5

    def fox_q(y, lo, hi):
        qf_ref[:, lo:hi] = (_head_rms(y, gfq_ref[:, lo:hi], bd) * (scale * LOG2E)).astype(BF16)

    def fox_k(y, lo, hi):
        kf_ref[:, lo:hi] = _head_rms(y, gfk_ref[:, lo:hi], bd).astype(BF16)

    def fox_v(y, lo, hi):
        vf_ref[:, lo:hi] = y.astype(BF16)
        rope_table()

    def swa_q(y, lo, hi):
        y = _rope(_head_rms(y, gsq_ref[:, lo:hi], bd), *rope_table(), first_half)
        qs_ref[:, lo:hi] = (y * (scale * LOG2E)).astype(BF16)

    def swa_k(y, lo, hi):
        ks_ref[:, lo:hi] = _rope(_head_rms(y, gsk_ref[:, lo:hi], bd), *rope_table(), first_half).astype(BF16)

    def swa_v(y, lo, hi):
        vs_ref[:, lo:hi] = y.astype(BF16)

    def gate(y, lo, hi):
        z = y + bf_ref[...]
        lf_ref[...] = jnp.minimum(z, 0.0) - jnp.log1p(jnp.exp(-jnp.abs(z)))

    stages = []
    for start, end, epilogue in ((C_FQ, C_FK, fox_q), (C_FK, C_FV, fox_k), (C_FV, C_SQ, fox_v),
                                 (C_SQ, C_SK, swa_q), (C_SK, C_SV, swa_k), (C_SV, C_GATE, swa_v),
                                 (C_GATE, D_IN_PAD, gate)):
        for lo in range(start, end, INPROJ_TN):
            hi = min(lo + INPROJ_TN, end)
            stages.append((lo, hi, lo - start, hi - start, epilogue))
    pending = []
    for lo, hi, olo, ohi, epilogue in stages:
        y = jnp.dot(u, w_ref[:, lo:hi], preferred_element_type=F32)
        if len(pending) >= INPROJ_DEPTH:
            fn, *args = pending.pop(0)
            fn(*args)
        pending.append((epilogue, y, olo, ohi))
    for fn, *args in pending:
        fn(*args)


def _inproj(h, g, w, pos, invf, gfq, gfk, gsq, gsk, bf):
    t, d = h.shape
    row = lambda n: pl.BlockSpec((ROW_TM, n), lambda i: (i, 0))
    const = lambda shape: pl.BlockSpec(shape, lambda i: (0, 0))
    return pl.pallas_call(
        _inproj_kernel,
        grid=(t // ROW_TM,),
        in_specs=[
            row(d),
            const((1, d)),
            pl.BlockSpec((d, D_IN_PAD), lambda i: (0, 0), pipeline_mode=pl.Buffered(1)),
            row(1),
            const((1, LANES)),
            const((1, D_FOX)), const((1, D_FOX)), const((1, D_SWA_Q)), const((1, D_SWA_KV)),
            const((1, LANES)),
        ],
        out_specs=[row(D_FOX), row(D_FOX), row(D_FOX), row(D_SWA_Q), row(D_SWA_KV), row(D_SWA_KV), row(LANES)],
        out_shape=[
            jax.ShapeDtypeStruct((t, D_FOX), BF16), jax.ShapeDtypeStruct((t, D_FOX), BF16),
            jax.ShapeDtypeStruct((t, D_FOX), BF16), jax.ShapeDtypeStruct((t, D_SWA_Q), BF16),
            jax.ShapeDtypeStruct((t, D_SWA_KV), BF16), jax.ShapeDtypeStruct((t, D_SWA_KV), BF16),
            jax.ShapeDtypeStruct((t, LANES), F32),
        ],
        compiler_params=pltpu.CompilerParams(dimension_semantics=("parallel",), vmem_limit_bytes=VMEM_LIMIT),
        name="inproj",
    )(h, g, w, pos, invf, gfq, gfk, gsq, gsk, bf)


def _split3(x):
    p1 = x.astype(BF16)
    r1 = x - p1.astype(F32)
    p2 = r1.astype(BF16)
    p3 = (r1 - p2.astype(F32)).astype(BF16)
    return p1, p2, p3


def _cumsum_kernel(lf_ref, pmap_ref, ak_ref, aq_ref):
    s = lf_ref.shape[0]
    r = lax.broadcasted_iota(jnp.int32, (CUM_BLK, CUM_BLK), 0)
    c = lax.broadcasted_iota(jnp.int32, (CUM_BLK, CUM_BLK), 1)
    tri = jnp.where(r >= c, 1.0, 0.0).astype(BF16)
    lane = lax.broadcasted_iota(jnp.int32, (1, LANES), 1)
    is_gate = lane < N_GATE
    one = jnp.where(lane == ONE_LANE, 1.0, 0.0)
    local = []
    for b in range(s // CUM_BLK):
        p1, p2, p3 = _split3(lf_ref[b * CUM_BLK:(b + 1) * CUM_BLK, :])
        local.append(jnp.dot(tri, p1, preferred_element_type=F32)
                     + jnp.dot(tri, p2, preferred_element_type=F32)
                     + jnp.dot(tri, p3, preferred_element_type=F32))
    carry = jnp.zeros((1, LANES), F32)
    for b in range(s // CUM_BLK):
        rows = slice(b * CUM_BLK, (b + 1) * CUM_BLK)
        cb = local[b] + carry
        carry = cb[CUM_BLK - 1:CUM_BLK, :]
        terms = [jnp.where(is_gate, t.astype(F32), 0.0) for t in _split3(cb * LOG2E)]
        packed = one + terms[0]
        for i in range(1, AUG_PARTS):
            packed = packed + pltpu.roll(terms[i], i * N_GATE, 1)
        placed = jnp.dot(packed.astype(BF16), pmap_ref[...], preferred_element_type=F32)
        ak_ref[rows, :] = placed[:, :D_FOX].astype(BF16)
        aq_ref[rows, :] = placed[:, D_FOX:].astype(BF16)


def _aug_map():
    pmap = np.zeros((LANES, 2 * D_FOX), np.float32)
    for h in range(FOX_HEADS):
        base = (h // 2) * LANES + (HEAD_DIM if h % 2 == 0 else 0)
        for i in range(AUG_PARTS):
            pmap[i * N_GATE + h, base + i] = -1.0
            pmap[ONE_LANE, D_FOX + base + i] = 1.0
        pmap[ONE_LANE, base + AUG_PARTS] = 1.0
        pmap[h, D_FOX + base + AUG_PARTS] = 1.0
    return jnp.asarray(pmap, BF16)


def _cumsum(lf, batch, seq):
    pmap = _aug_map()
    out = pl.BlockSpec((seq, D_FOX), lambda b: (b, 0))
    return pl.pallas_call(
        _cumsum_kernel,
        grid=(batch,),
        in_specs=[pl.BlockSpec((seq, LANES), lambda b: (b, 0)), pl.BlockSpec(pmap.shape, lambda b: (0, 0))],
        out_specs=[out, out],
        out_shape=[jax.ShapeDtypeStruct((batch * seq, D_FOX), BF16)] * 2,
        compiler_params=pltpu.CompilerParams(dimension_semantics=("parallel",)),
        name="gate_cumsum",
    )(lf, pmap)


def _fox_kernel(q_ref, k_ref, v_ref, aq_ref, ak_ref, o_ref, qa_ref, ka_ref):
    seq = q_ref.shape[0]
    lane = lax.broadcasted_iota(jnp.int32, (1, LANES), 1)
    lo = lane < HEAD_DIM
    q, k, aq, ak = q_ref[...], k_ref[...], aq_ref[...], ak_ref[...]
    qa_ref[0] = jnp.where(lo, q, aq)
    qa_ref[1] = jnp.where(lo, aq, q)
    ka_ref[0] = jnp.where(lo, k, ak)
    ka_ref[1] = jnp.where(lo, ak, k)

    qry = lax.broadcasted_iota(jnp.int32, (FOX_TQ, FOX_TQ), 0)
    key = lax.broadcasted_iota(jnp.int32, (FOX_TQ, FOX_TQ), 1)
    causal = key <= qry
    nt = (((1,), (1,)), ((), ()))

    def scores(i, side):
        rows = slice(i * FOX_TQ, (i + 1) * FOX_TQ)
        return lax.dot_general(qa_ref[side, rows, :], ka_ref[side, 0:(i + 1) * FOX_TQ, :], nt,
                               preferred_element_type=F32)

    def softmax(i, s):
        w = (i + 1) * FOX_TQ
        diag = jnp.where(causal, s[:, w - FOX_TQ:], -jnp.inf)
        s = diag if i == 0 else jnp.concatenate([s[:, :w - FOX_TQ], diag], axis=1)
        m = jnp.max(s, axis=1, keepdims=True)
        p = jnp.exp2(s - m)
        return p.astype(BF16), jnp.sum(p, axis=1, keepdims=True)

    def values(i, p, l):
        o = jnp.dot(p, v_ref[0:(i + 1) * FOX_TQ, :], preferred_element_type=F32)
        return o / l

    chains = [(i, side) for i in reversed(range(seq // FOX_TQ)) for side in range(2)]
    s_buf, p_buf, done = {}, {}, {}
    for t in range(len(chains) + FOX_LAG_V):
        if t < len(chains):
            s_buf[t] = scores(*chains[t])
        if FOX_LAG_S <= t < len(chains) + FOX_LAG_S:
            p_buf[t - FOX_LAG_S] = softmax(chains[t - FOX_LAG_S][0], s_buf.pop(t - FOX_LAG_S))
        if t >= FOX_LAG_V:
            i, side = chains[t - FOX_LAG_V]
            done[(i, side)] = values(i, *p_buf.pop(t - FOX_LAG_V))
            if side == 1:
                rows = slice(i * FOX_TQ, (i + 1) * FOX_TQ)
                o_ref[rows, :] = jnp.where(lo, done.pop((i, 0)), done.pop((i, 1))).astype(BF16)


def _fox(q, k, v, aq, ak, batch, seq):
    blk = pl.BlockSpec((seq, LANES), lambda b, p: (b, p))
    return pl.pallas_call(
        _fox_kernel,
        grid=(batch, D_FOX // LANES),
        in_specs=[blk] * 5,
        out_specs=blk,
        out_shape=jax.ShapeDtypeStruct((batch * seq, D_FOX), BF16),
        scratch_shapes=[pltpu.VMEM((2, seq, LANES), BF16), pltpu.VMEM((2, seq, LANES), BF16)],
        compiler_params=pltpu.CompilerParams(
            dimension_semantics=("parallel", "arbitrary"), vmem_limit_bytes=VMEM_LIMIT),
        name="fox_attn",
    )(q, k, v, aq, ak)


SWA_ROWS = SWA_GROUP * WINDOW
SWA_UNROLL = 5
SWA_CHAINS = tuple((p, side) for p in range(D_SWA_KV // LANES) for side in range(2))


def _swa_kernel(sink_ref, q_ref, k_ref, v_ref, o_ref, bias_ref, sink_col_ref):
    seq = q_ref.shape[0]
    nblk = seq // WINDOW
    lane = lax.broadcasted_iota(jnp.int32, (1, LANES), 1)
    lo = lane < HEAD_DIM
    nt = (((1,), (1,)), ((), ()))

    r = lax.broadcasted_iota(jnp.int32, (SWA_ROWS, 2 * WINDOW), 0) % WINDOW
    c = lax.broadcasted_iota(jnp.int32, (SWA_ROWS, 2 * WINDOW), 1)
    bias_ref[...] = jnp.where((c > r) & (c <= r + WINDOW), 0.0, -jnp.inf)
    head = lax.broadcasted_iota(jnp.int32, (SWA_ROWS, LANES), 0) // WINDOW
    for n, (p, side) in enumerate(SWA_CHAINS):
        col = jnp.zeros((SWA_ROWS, LANES), F32)
        for j in range(SWA_GROUP):
            col = jnp.where(head == j, sink_ref[8 * p + 4 * side + j] * LOG2E, col)
        sink_col_ref[n] = col

    def block(q_rows, k_rows, first):
        cols = slice(WINDOW, 2 * WINDOW) if first else slice(0, 2 * WINDOW)

        def scores(n):
            p, side = SWA_CHAINS[n]
            tiles = [q_ref[q_rows, (p * SWA_GROUP + j) * LANES:(p * SWA_GROUP + j + 1) * LANES]
                     for j in range(SWA_GROUP)]
            qst = jnp.concatenate(tiles, axis=0)
            qm = jnp.where(lo == (side == 0), qst, jnp.zeros_like(qst))
            kw = k_ref[k_rows, p * LANES:(p + 1) * LANES]
            return lax.dot_general(qm, kw, nt, preferred_element_type=F32)

        def softmax(n, s):
            s = s + bias_ref[:, cols]
            sk = sink_col_ref[n]
            m = jnp.maximum(jnp.max(s, axis=1, keepdims=True), sk)
            e = jnp.exp2(s - jnp.concatenate([m] * (s.shape[1] // LANES), axis=1))
            l = jnp.sum(e, axis=1, keepdims=True) + jnp.exp2(sk - m)
            return e.astype(BF16), l

        def values(n, e, l):
            p, _ = SWA_CHAINS[n]
            vw = v_ref[k_rows, p * LANES:(p + 1) * LANES]
            return jnp.dot(e, vw, preferred_element_type=F32) / l

        s_buf, e_buf, done = {}, {}, {}
        for t in range(len(SWA_CHAINS) + 2):
            if t < len(SWA_CHAINS):
                s_buf[t] = scores(t)
            if 1 <= t <= len(SWA_CHAINS):
                e_buf[t - 1] = softmax(t - 1, s_buf.pop(t - 1))
            if t >= 2:
                p, side = SWA_CHAINS[t - 2]
                done[side] = values(t - 2, *e_buf.pop(t - 2))
                if side == 1:
                    o_both = jnp.where(lo, done.pop(0), done.pop(1)).astype(BF16)
                    for j in range(SWA_GROUP):
                        tile = p * SWA_GROUP + j
                        o_ref[q_rows, tile * LANES:(tile + 1) * LANES] = o_both[j * WINDOW:(j + 1) * WINDOW, :]

    block(slice(0, WINDOW), slice(0, WINDOW), True)

    def body(n, carry):
        q0 = pl.multiple_of(n * WINDOW, WINDOW)
        k0 = pl.multiple_of((n - 1) * WINDOW, WINDOW)
        block(pl.ds(q0, WINDOW), pl.ds(k0, 2 * WINDOW), False)
        return carry

    lax.fori_loop(1, nblk, body, 0, unroll=SWA_UNROLL)


def _swa(sinks, q, k, v, batch, seq):
    return pl.pallas_call(
        _swa_kernel,
        grid=(batch,),
        in_specs=[pl.BlockSpec(memory_space=pltpu.SMEM),
                  pl.BlockSpec((seq, D_SWA_Q), lambda b: (b, 0)),
                  pl.BlockSpec((seq, D_SWA_KV), lambda b: (b, 0)),
                  pl.BlockSpec((seq, D_SWA_KV), lambda b: (b, 0))],
        out_specs=pl.BlockSpec((seq, D_SWA_Q), lambda b: (b, 0)),
        out_shape=jax.ShapeDtypeStruct((batch * seq, D_SWA_Q), BF16),
        scratch_shapes=[pltpu.VMEM((SWA_ROWS, 2 * WINDOW), F32),
                        pltpu.VMEM((len(SWA_CHAINS), SWA_ROWS, LANES), F32)],
        compiler_params=pltpu.CompilerParams(dimension_semantics=("parallel",), vmem_limit_bytes=VMEM_LIMIT),
        name="swa_attn",
    )(sinks, q, k, v)


def _outproj_kernel(of_ref, os_ref, h_ref, gf_ref, gs_ref, wf_ref, ws_ref, o_ref, wsp_ref):
    @pl.when(pl.program_id(0) == 0)
    def _():
        for k, h in enumerate(SWA_HEAD_ORDER):
            wsp_ref[k * HEAD_DIM:(k + 1) * HEAD_DIM, :] = ws_ref[h * HEAD_DIM:(h + 1) * HEAD_DIM, :]

    nf = _rms_rows(of_ref[...].astype(F32), gf_ref[...]).astype(BF16)
    ns = _rms_rows(os_ref[...].astype(F32), gs_ref[...]).astype(BF16)
    acc = jnp.dot(nf, wf_ref[...], preferred_element_type=F32)
    acc = acc + jnp.dot(ns, wsp_ref[...], preferred_element_type=F32)
    o_ref[...] = h_ref[...] + acc


def _outproj(of, os_, h, gf, gs, w):
    t, d = h.shape
    row = lambda n: pl.BlockSpec((ROW_TM, n), lambda i: (i, 0))
    const = lambda shape: pl.BlockSpec(shape, lambda i: (0, 0))
    half = lambda n: pl.BlockSpec((D_FOX, d), lambda i: (n, 0), pipeline_mode=pl.Buffered(1))
    return pl.pallas_call(
        _outproj_kernel,
        grid=(t // ROW_TM,),
        in_specs=[row(D_FOX), row(D_SWA_Q), row(d), const((1, D_FOX)), const((1, D_SWA_Q)), half(0), half(1)],
        out_specs=row(d),
        out_shape=jax.ShapeDtypeStruct((t, d), F32),
        scratch_shapes=[pltpu.VMEM((D_SWA_Q, d), BF16)],
        compiler_params=pltpu.CompilerParams(dimension_semantics=("arbitrary",), vmem_limit_bytes=VMEM_LIMIT),
        name="outproj",
    )(of, os_, h, gf, gs, w, w)


def _swa_reorder(a, axis):
    shape = a.shape
    a = a.reshape(shape[:axis] + (2, 2, SWA_GROUP, HEAD_DIM) + shape[axis + 1:])
    a = jnp.swapaxes(a, axis + 1, axis + 2)
    return a.reshape(shape)


def _w_in_head_rows():
    gate0 = 3 * D_FOX
    sq0 = gate0 + N_GATE
    starts = [c for c in range(0, gate0, HEAD_DIM)]
    starts += [sq0 + h * HEAD_DIM for h in SWA_HEAD_ORDER]
    starts += [sq0 + D_SWA_Q + c for c in range(0, 2 * D_SWA_KV, HEAD_DIM)]
    starts += [gate0, gate0]
    assert len(starts) * HEAD_DIM == D_IN_PAD
    return np.asarray(starts, np.int32)


def _w_in_prep_kernel(rows_ref, wt_ref, o_ref):
    t = pl.program_id(0)
    pieces = [wt_ref[pl.ds(pl.multiple_of(rows_ref[2 * t + half], 2 * SUBLANES), HEAD_DIM), :]
              for half in range(2)]
    x = jnp.concatenate(pieces, axis=0)
    row = lax.broadcasted_iota(jnp.int32, (LANES, 1), 0)
    x = jnp.where((t == pl.num_programs(0) - 1) & (row >= N_GATE), 0.0, x)
    o_ref[...] = jnp.transpose(x).astype(BF16)


def _w_in_prep(w_in_stack, layer):
    _, d, n = w_in_stack.shape
    wt = jnp.swapaxes(w_in_stack, 1, 2)
    return pl.pallas_call(
        _w_in_prep_kernel,
        grid=(D_IN_PAD // LANES,),
        in_specs=[pl.BlockSpec(memory_space=pltpu.SMEM),
                  pl.BlockSpec((None, n, d), lambda t: (layer, 0, 0), pipeline_mode=pl.Buffered(1))],
        out_specs=pl.BlockSpec((d, LANES), lambda t: (0, t)),
        out_shape=jax.ShapeDtypeStruct((d, D_IN_PAD), BF16),
        compiler_params=pltpu.CompilerParams(dimension_semantics=("arbitrary",), vmem_limit_bytes=VMEM_LIMIT),
        name="w_in_prep",
    )(jnp.asarray(_w_in_head_rows()), wt)


def _layer(h, pos, invf, p, w_in_stack, layer):
    (norm_ffn1_g, w_gate1, w_up1, w_down1, norm_mix_g, b_forget, fox_q_g, fox_k_g, swa_q_g, swa_k_g,
     sinks, out_fox_g, out_swa_g, w_out, norm_ffn2_g, w_gate2, w_up2, w_down2) = p
    batch_seq, d = h.shape
    batch, seq = pos.shape
    row = lambda v: v.reshape(1, -1).astype(F32)
    tile = lambda v, n: jnp.tile(v.astype(F32), n).reshape(1, -1)

    w_in_r = _w_in_prep(w_in_stack, layer)
    bf_pad = jnp.concatenate([b_forget.astype(F32), jnp.zeros((LANES - N_GATE,), F32)]).reshape(1, LANES)

    h1_head, w_gate1_b, w_up1_b, w_down1_b = _ffn_head(h, row(norm_ffn1_g), w_gate1, w_up1, w_down1)
    h1, (w_gate2_b, w_up2_b, w_down2_b) = _ffn(
        h, row(norm_ffn1_g), w_gate1_b, w_up1_b, w_down1_b, cast=(w_gate2, w_up2, w_down2), head=h1_head)

    qf, kf, vf, qs, ks, vs, lf = _inproj(
        h1, row(norm_mix_g), w_in_r, pos.reshape(batch_seq, 1), invf,
        tile(fox_q_g, FOX_HEADS), tile(fox_k_g, FOX_HEADS), tile(swa_q_g, SWA_Q_HEADS),
        tile(swa_k_g, SWA_KV_HEADS), bf_pad)

    ak, aq = _cumsum(lf, batch, seq)
    o_fox = _fox(qf, kf, vf, aq, ak, batch, seq)
    o_swa = _swa(sinks.astype(F32), qs, ks, vs, batch, seq)

    h2 = _outproj(o_fox, o_swa, h1, row(out_fox_g), row(_swa_reorder(out_swa_g, 0)), w_out.astype(BF16))
    return _ffn(h2, row(norm_ffn2_g), w_gate2_b, w_up2_b, w_down2_b)[0]


def kernel(x, positions, norm_ffn1_g, ffn1_w_gate, ffn1_w_up, ffn1_w_down, norm_mix_g, w_in, b_forget, fox_q_norm_g, fox_k_norm_g, swa_q_norm_g, swa_k_norm_g, swa_sinks, out_norm_fox_g, out_norm_swa_g, w_out, norm_ffn2_g, ffn2_w_gate, ffn2_w_up, ffn2_w_down):
    batch, seq, d = x.shape
    half = jnp.arange(0, HEAD_DIM, 2, dtype=F32)
    inv_freq = ROPE_THETA ** (-half / HEAD_DIM)
    invf = jnp.tile(inv_freq, LANES // (HEAD_DIM // 2)).reshape(1, LANES)
    stacks = (norm_ffn1_g, ffn1_w_gate, ffn1_w_up, ffn1_w_down, norm_mix_g, b_forget, fox_q_norm_g,
              fox_k_norm_g, swa_q_norm_g, swa_k_norm_g, swa_sinks, out_norm_fox_g, out_norm_swa_g, w_out,
              norm_ffn2_g, ffn2_w_gate, ffn2_w_up, ffn2_w_down)
    h = x.reshape(batch * seq, d)
    for layer in range(norm_ffn1_g.shape[0]):
        h = _layer(h, positions, invf, tuple(s[layer] for s in stacks), w_in, layer)
    return h.reshape(batch, seq, d)
```

```python
import functools

import jax
import jax.numpy as jnp
import numpy as np
from jax import lax
from jax.experimental import pallas as pl
from jax.experimental.pallas import tpu as pltpu

F32 = jnp.float32
BF16 = jnp.bfloat16

D_MODEL = 2048
HEAD_DIM = 64
FOX_HEADS = 16
SWA_Q_HEADS = 16
SWA_KV_HEADS = 4
SWA_GROUP = SWA_Q_HEADS // SWA_KV_HEADS
WINDOW = 128
D_FF = 5632
ROPE_THETA = 10000.0
EPS = 1e-6

D_FOX = FOX_HEADS * HEAD_DIM
D_SWA_Q = SWA_Q_HEADS * HEAD_DIM
D_SWA_KV = SWA_KV_HEADS * HEAD_DIM
N_GATE = FOX_HEADS

LANES = 128
SUBLANES = 8
MXU_TILE = 256
VMEM_LIMIT = 60 * 1024 * 1024

C_FQ = 0
C_FK = C_FQ + D_FOX
C_FV = C_FK + D_FOX
C_SQ = C_FV + D_FOX
C_SK = C_SQ + D_SWA_Q
C_SV = C_SK + D_SWA_KV
C_GATE = C_SV + D_SWA_KV
D_IN_PAD = C_GATE + LANES

SWA_HEAD_ORDER = tuple(8 * p + 4 * side + j for p in range(2) for j in range(SWA_GROUP) for side in range(2))

FFN_TM = 1024
FFN_TF = 512
HEAD_TF = 256
HEAD_CHUNKS = 8
ROW_TM = 512
INPROJ_TN = 512
OUTPROJ_ROWS = 256
INPROJ_DEPTH = 1
FOX_TQ = 256
FOX_TILES = 2
FOX_LAG_S, FOX_LAG_V = 1, 2
CUM_BLK = 256

LOG2E = 1.4426950408889634

AUG_PARTS = 3
ONE_LANE = AUG_PARTS * N_GATE


def _rms_rows(x, g):
    return x * lax.rsqrt(jnp.mean(x * x, axis=-1, keepdims=True) + EPS) * g


def _swiglu_tile(u, wg, wu, wd):
    gate = jnp.dot(u, wg, preferred_element_type=F32)
    up = jnp.dot(u, wu, preferred_element_type=F32)
    a = (gate * jax.nn.sigmoid(gate) * up * 0.5).astype(BF16)
    return jnp.dot(a, wd, preferred_element_type=F32)


def _ffn_first(x_ref, g_ref, u_ref, wg, wu, wd, o_ref):
    x = x_ref[...]
    u = _rms_rows(x, g_ref[...]).astype(BF16)
    u_ref[...] = u
    o_ref[...] = x + _swiglu_tile(u, wg, wu, wd)


def _ffn_step(u_ref, wg, wu, wd, o_ref):
    o_ref[...] += _swiglu_tile(u_ref[...], wg, wu, wd)


def _ffn_kernel(x_ref, g_ref, wg_ref, wu_ref, wd_ref, *refs, n_cast, has_head):
    head_ref = refs[0] if has_head else None
    refs = refs[1:] if has_head else refs
    cast_in, o_ref, cast_out, u_ref = refs[:n_cast], refs[n_cast], refs[n_cast + 1:-1], refs[-1]
    i, j = pl.program_id(0), pl.program_id(1)

    def side_job():
        for src, dst in zip(cast_in, cast_out):
            dst[...] = src[...].astype(BF16)

    if has_head:
        @pl.when(i == 0)
        def _():
            @pl.when(j < HEAD_CHUNKS)
            def _():
                rows = FFN_TM // HEAD_CHUNKS
                o_ref[pl.ds(pl.multiple_of(j * rows, rows), rows), :] = head_ref[...]

            side_job()

        computing = i > 0
    else:
        computing = True

    @pl.when(computing & (j == 0))
    def _():
        _ffn_first(x_ref, g_ref, u_ref, wg_ref[...], wu_ref[...], wd_ref[...], o_ref)
        side_job()

    @pl.when(computing & (j > 0))
    def _():
        _ffn_step(u_ref, wg_ref[...], wu_ref[...], wd_ref[...], o_ref)
        side_job()


def _ffn(x, g, wg, wu, wd, cast=(), head=None):
    t, d = x.shape
    f = wg.shape[1]
    ni, nj = t // FFN_TM, f // FFN_TF
    has_head = head is not None
    live = (lambda i: jnp.minimum(i, 1)) if has_head else (lambda i: 1)
    row = (lambda i: jnp.maximum(i, 1)) if has_head else (lambda i: i)
    cast_specs = []
    for w in cast:
        r, c = w.shape
        if r % ni == 0 and c % nj == 0 and (c // nj) % LANES == 0:
            cast_specs.append(pl.BlockSpec((r // ni, c // nj), lambda i, j: (i, j)))
        elif r % nj == 0 and c % ni == 0 and (c // ni) % LANES == 0:
            cast_specs.append(pl.BlockSpec((r // nj, c // ni), lambda i, j: (j, i)))
        else:
            assert r % ni == 0, w.shape
            cast_specs.append(pl.BlockSpec((r // ni, c), lambda i, j: (i, 0)))
    head_specs, head_args = [], []
    if has_head:
        last = HEAD_CHUNKS - 1
        head_specs = [pl.BlockSpec((FFN_TM // HEAD_CHUNKS, d),
                                   lambda i, j: (jnp.where(i == 0, jnp.minimum(j, last), last), 0))]
        head_args = [head]
    outs = pl.pallas_call(
        functools.partial(_ffn_kernel, n_cast=len(cast), has_head=has_head),
        grid=(ni, nj),
        in_specs=[
            pl.BlockSpec((FFN_TM, d), lambda i, j: (row(i), 0)),
            pl.BlockSpec((1, d), lambda i, j: (0, 0)),
            pl.BlockSpec((d, FFN_TF), lambda i, j: (0, j * live(i))),
            pl.BlockSpec((d, FFN_TF), lambda i, j: (0, j * live(i))),
            pl.BlockSpec((FFN_TF, d), lambda i, j: (j * live(i), 0)),
        ] + head_specs + cast_specs,
        out_specs=[pl.BlockSpec((FFN_TM, d), lambda i, j: (i, 0))] + cast_specs,
        out_shape=[jax.ShapeDtypeStruct((t, d), F32)] + [jax.ShapeDtypeStruct(w.shape, BF16) for w in cast],
        scratch_shapes=[pltpu.VMEM((FFN_TM, d), BF16)],
        compiler_params=pltpu.CompilerParams(
            dimension_semantics=("parallel", "arbitrary"), vmem_limit_bytes=VMEM_LIMIT),
        name="ffn_cast" if cast else "ffn",
    )(x, g, wg, wu, wd, *head_args, *cast)
    return outs[0], tuple(outs[1:])


def _ffn_head_kernel(x_ref, g_ref, wg_ref, wu_ref, wd_ref, o_ref, wgb_ref, wub_ref, wdb_ref, u_ref):
    wgb_ref[...] = wg_ref[...].astype(BF16)
    wub_ref[...] = wu_ref[...].astype(BF16)
    wdb_ref[...] = wd_ref[...].astype(BF16)

    @pl.when(pl.program_id(0) == 0)
    def _():
        _ffn_first(x_ref, g_ref, u_ref, wgb_ref[...], wub_ref[...], wdb_ref[...], o_ref)

    @pl.when(pl.program_id(0) > 0)
    def _():
        _ffn_step(u_ref, wgb_ref[...], wub_ref[...], wdb_ref[...], o_ref)


def _ffn_head(x, g, wg, wu, wd):
    t, d = x.shape
    f = wg.shape[1]
    once = lambda shape: pl.BlockSpec(shape, lambda j: (0, 0), pipeline_mode=pl.Buffered(1))
    col = pl.BlockSpec((d, HEAD_TF), lambda j: (0, j))
    rowb = pl.BlockSpec((HEAD_TF, d), lambda j: (j, 0))
    return pl.pallas_call(
        _ffn_head_kernel,
        grid=(f // HEAD_TF,),
        in_specs=[once((FFN_TM, d)), once((1, d)), col, col, rowb],
        out_specs=[pl.BlockSpec((FFN_TM, d), lambda j: (0, 0)), col, col, rowb],
        out_shape=[jax.ShapeDtypeStruct((FFN_TM, d), F32), jax.ShapeDtypeStruct(wg.shape, BF16),
                   jax.ShapeDtypeStruct(wu.shape, BF16), jax.ShapeDtypeStruct(wd.shape, BF16)],
        scratch_shapes=[pltpu.VMEM((FFN_TM, d), BF16)],
        compiler_params=pltpu.CompilerParams(dimension_semantics=("arbitrary",), vmem_limit_bytes=VMEM_LIMIT),
        name="ffn_head",
    )(x, g, wg, wu, wd)


def _head_rms(y, gain, bd):
    n = y.shape[1]
    y2 = (y * y).astype(BF16)
    parts = [jnp.dot(y2[:, c * MXU_TILE:(c + 1) * MXU_TILE], bd, preferred_element_type=F32)
             for c in range(n // MXU_TILE)]
    ss = parts[0] if len(parts) == 1 else jnp.concatenate(parts, axis=1)
    return y * lax.rsqrt(ss * (1.0 / HEAD_DIM) + EPS) * gain


def _rope(y, cos, sin_signed, first_half):
    outs = []
    for c in range(y.shape[1] // LANES):
        yc = y[:, c * LANES:(c + 1) * LANES]
        partner = jnp.where(first_half, pltpu.roll(yc, LANES - HEAD_DIM // 2, 1),
                            pltpu.roll(yc, HEAD_DIM // 2, 1))
        outs.append(yc * cos + partner * sin_signed)
    return outs[0] if len(outs) == 1 else jnp.concatenate(outs, axis=1)


def _inproj_kernel(h_ref, g_ref, w_ref, pos_ref, invf_ref, gfq_ref, gfk_ref, gsq_ref, gsk_ref, bf_ref,
                   qf_ref, kf_ref, vf_ref, qs_ref, ks_ref, vs_ref, lf_ref):
    u = _rms_rows(h_ref[...], g_ref[...]).astype(BF16)

    lane = lax.broadcasted_iota(jnp.int32, (1, LANES), 1)
    first_half = (lane % HEAD_DIM) < (HEAD_DIM // 2)
    rope_tables = []

    def rope_table():
        if not rope_tables:
            ang = pos_ref[...].astype(F32) * invf_ref[...]
            sin = jnp.sin(ang)
            rope_tables.append((jnp.cos(ang), jnp.where(first_half, -sin, sin)))
        return rope_tables[0]

    r = lax.broadcasted_iota(jnp.int32, (MXU_TILE, MXU_TILE), 0) // HEAD_DIM
    c = lax.broadcasted_iota(jnp.int32, (MXU_TILE, MXU_TILE), 1) // HEAD_DIM
    bd = jnp.where(r == c, 1.0, 0.0).astype(BF16)

    scale = HEAD_DIM ** -0.5

    def fox_q(y, lo, hi):
        qf_ref[:, lo:hi] = (_head_rms(y, gfq_ref[:, lo:hi], bd) * (scale * LOG2E)).astype(BF16)

    def fox_k(y, lo, hi):
        kf_ref[:, lo:hi] = _head_rms(y, gfk_ref[:, lo:hi], bd).astype(BF16)

    def fox_v(y, lo, hi):
        vf_ref[:, lo:hi] = y.astype(BF16)
        rope_table()

    def swa_q(y, lo, hi):
        y = _rope(_head_rms(y, gsq_ref[:, lo:hi], bd), *rope_table(), first_half)
        qs_ref[:, lo:hi] = (y * (scale * LOG2E)).astype(BF16)

    def swa_k(y, lo, hi):
        ks_ref[:, lo:hi] = _rope(_head_rms(y, gsk_ref[:, lo:hi], bd), *rope_table(), first_half).astype(BF16)

    def swa_v(y, lo, hi):
        vs_ref[:, lo:hi] = y.astype(BF16)

    def gate(y, lo, hi):
        z = y + bf_ref[...]
        lf_ref[...] = jnp.minimum(z, 0.0) - jnp.log1p(jnp.exp(-jnp.abs(z)))

    stages = []
    for start, end, epilogue in ((C_FQ, C_FK, fox_q), (C_FK, C_FV, fox_k), (C_FV, C_SQ, fox_v),
                                 (C_SQ, C_SK, swa_q), (C_SK, C_SV, swa_k), (C_SV, C_GATE, swa_v),
                                 (C_GATE, D_IN_PAD, gate)):
        for lo in range(start, end, INPROJ_TN):
            hi = min(lo + INPROJ_TN, end)
            stages.append((lo, hi, lo - start, hi - start, epilogue))
    pending = []
    for lo, hi, olo, ohi, epilogue in stages:
        y = jnp.dot(u, w_ref[:, lo:hi], preferred_element_type=F32)
        if len(pending) >= INPROJ_DEPTH:
            fn, *args = pending.pop(0)
            fn(*args)
        pending.append((epilogue, y, olo, ohi))
    for fn, *args in pending:
        fn(*args)


def _inproj(h, g, w, pos, invf, gfq, gfk, gsq, gsk, bf):
    t, d = h.shape
    row = lambda n: pl.BlockSpec((ROW_TM, n), lambda i: (i, 0))
    const = lambda shape: pl.BlockSpec(shape, lambda i: (0, 0))
    return pl.pallas_call(
        _inproj_kernel,
        grid=(t // ROW_TM,),
        in_specs=[
            row(d),
            const((1, d)),
            pl.BlockSpec((d, D_IN_PAD), lambda i: (0, 0), pipeline_mode=pl.Buffered(1)),
            row(1),
            const((1, LANES)),
            const((1, D_FOX)), const((1, D_FOX)), const((1, D_SWA_Q)), const((1, D_SWA_KV)),
            const((1, LANES)),
        ],
        out_specs=[row(D_FOX), row(D_FOX), row(D_FOX), row(D_SWA_Q), row(D_SWA_KV), row(D_SWA_KV), row(LANES)],
        out_shape=[
            jax.ShapeDtypeStruct((t, D_FOX), BF16), jax.ShapeDtypeStruct((t, D_FOX), BF16),
            jax.ShapeDtypeStruct((t, D_FOX), BF16), jax.ShapeDtypeStruct((t, D_SWA_Q), BF16),
            jax.ShapeDtypeStruct((t, D_SWA_KV), BF16), jax.ShapeDtypeStruct((t, D_SWA_KV), BF16),
            jax.ShapeDtypeStruct((t, LANES), F32),
        ],
        compiler_params=pltpu.CompilerParams(dimension_semantics=("parallel",), vmem_limit_bytes=VMEM_LIMIT),
        name="inproj",
    )(h, g, w, pos, invf, gfq, gfk, gsq, gsk, bf)


def _split3(x):
    p1 = x.astype(BF16)
    r1 = x - p1.astype(F32)
    p2 = r1.astype(BF16)
    p3 = (r1 - p2.astype(F32)).astype(BF16)
    return p1, p2, p3


def _cumsum_kernel(lf_ref, pmap_ref, ak_ref, aq_ref):
    s = lf_ref.shape[0]
    r = lax.broadcasted_iota(jnp.int32, (CUM_BLK, CUM_BLK), 0)
    c = lax.broadcasted_iota(jnp.int32, (CUM_BLK, CUM_BLK), 1)
    tri = jnp.where(r >= c, 1.0, 0.0).astype(BF16)
    lane = lax.broadcasted_iota(jnp.int32, (1, LANES), 1)
    is_gate = lane < N_GATE
    one = jnp.where(lane == ONE_LANE, 1.0, 0.0)
    local = []
    for b in range(s // CUM_BLK):
        p1, p2, p3 = _split3(lf_ref[b * CUM_BLK:(b + 1) * CUM_BLK, :])
        local.append(jnp.dot(tri, p1, preferred_element_type=F32)
                     + jnp.dot(tri, p2, preferred_element_type=F32)
                     + jnp.dot(tri, p3, preferred_element_type=F32))
    carry = jnp.zeros((1, LANES), F32)
    for b in range(s // CUM_BLK):
        rows = slice(b * CUM_BLK, (b + 1) * CUM_BLK)
        cb = local[b] + carry
        carry = cb[CUM_BLK - 1:CUM_BLK, :]
        terms = [jnp.where(is_gate, t.astype(F32), 0.0) for t in _split3(cb * LOG2E)]
        packed = one + terms[0]
        for i in range(1, AUG_PARTS):
            packed = packed + pltpu.roll(terms[i], i * N_GATE, 1)
        placed = jnp.dot(packed.astype(BF16), pmap_ref[...], preferred_element_type=F32)
        ak_ref[rows, :] = placed[:, :D_FOX].astype(BF16)
        aq_ref[rows, :] = placed[:, D_FOX:].astype(BF16)


def _aug_map():
    pmap = np.zeros((LANES, 2 * D_FOX), np.float32)
    for h in range(FOX_HEADS):
        base = (h // 2) * LANES + (HEAD_DIM if h % 2 == 0 else 0)
        for i in range(AUG_PARTS):
            pmap[i * N_GATE + h, base + i] = -1.0
            pmap[ONE_LANE, D_FOX + base + i] = 1.0
        pmap[ONE_LANE, base + AUG_PARTS] = 1.0
        pmap[h, D_FOX + base + AUG_PARTS] = 1.0
    return jnp.asarray(pmap, BF16)


def _cumsum(lf, batch, seq):
    pmap = _aug_map()
    out = pl.BlockSpec((seq, D_FOX), lambda b: (b, 0))
    return pl.pallas_call(
        _cumsum_kernel,
        grid=(batch,),
        in_specs=[pl.BlockSpec((seq, LANES), lambda b: (b, 0)), pl.BlockSpec(pmap.shape, lambda b: (0, 0))],
        out_specs=[out, out],
        out_shape=[jax.ShapeDtypeStruct((batch * seq, D_FOX), BF16)] * 2,
        compiler_params=pltpu.CompilerParams(dimension_semantics=("parallel",)),
        name="gate_cumsum",
    )(lf, pmap)


def _fox_kernel(q_ref, k_ref, v_ref, aq_ref, ak_ref, o_ref, qa_ref, ka_ref):
    seq = q_ref.shape[0]
    lane = lax.broadcasted_iota(jnp.int32, (1, LANES), 1)
    lo = lane < HEAD_DIM
    for tile in range(FOX_TILES):
        cols = slice(tile * LANES, (tile + 1) * LANES)
        q, k, aq, ak = q_ref[:, cols], k_ref[:, cols], aq_ref[:, cols], ak_ref[:, cols]
        qa_ref[2 * tile] = jnp.where(lo, q, aq)
        qa_ref[2 * tile + 1] = jnp.where(lo, aq, q)
        ka_ref[2 * tile] = jnp.where(lo, k, ak)
        ka_ref[2 * tile + 1] = jnp.where(lo, ak, k)

    qry = lax.broadcasted_iota(jnp.int32, (FOX_TQ, FOX_TQ), 0)
    key = lax.broadcasted_iota(jnp.int32, (FOX_TQ, FOX_TQ), 1)
    causal = key <= qry
    nt = (((1,), (1,)), ((), ()))

    def scores(i, head):
        rows = slice(i * FOX_TQ, (i + 1) * FOX_TQ)
        return lax.dot_general(qa_ref[head, rows, :], ka_ref[head, 0:(i + 1) * FOX_TQ, :], nt,
                               preferred_element_type=F32)

    def softmax(i, s):
        w = (i + 1) * FOX_TQ
        diag = jnp.where(causal, s[:, w - FOX_TQ:], -jnp.inf)
        s = diag if i == 0 else jnp.concatenate([s[:, :w - FOX_TQ], diag], axis=1)
        m = jnp.max(s, axis=1, keepdims=True)
        p = jnp.exp2(s - m)
        return p.astype(BF16), jnp.sum(p, axis=1, keepdims=True)

    def values(i, head, p, l):
        tile = head // 2
        o = jnp.dot(p, v_ref[0:(i + 1) * FOX_TQ, tile * LANES:(tile + 1) * LANES],
                    preferred_element_type=F32)
        return o / l

    chains = [(i, head) for i in reversed(range(seq // FOX_TQ)) for head in range(2 * FOX_TILES)]
    s_buf, p_buf, done = {}, {}, {}
    for t in range(len(chains) + FOX_LAG_V):
        if t < len(chains):
            s_buf[t] = scores(*chains[t])
        if FOX_LAG_S <= t < len(chains) + FOX_LAG_S:
            p_buf[t - FOX_LAG_S] = softmax(chains[t - FOX_LAG_S][0], s_buf.pop(t - FOX_LAG_S))
        if t >= FOX_LAG_V:
            i, head = chains[t - FOX_LAG_V]
            done[(i, head)] = values(i, head, *p_buf.pop(t - FOX_LAG_V))
            if head % 2 == 1:
                rows = slice(i * FOX_TQ, (i + 1) * FOX_TQ)
                cols = slice((head // 2) * LANES, (head // 2 + 1) * LANES)
                o_ref[rows, cols] = jnp.where(lo, done.pop((i, head - 1)), done.pop((i, head))).astype(BF16)


def _fox(q, k, v, aq, ak, batch, seq):
    width = FOX_TILES * LANES
    blk = pl.BlockSpec((seq, width), lambda b, p: (b, p))
    return pl.pallas_call(
        _fox_kernel,
        grid=(batch, D_FOX // width),
        in_specs=[blk] * 5,
        out_specs=blk,
        out_shape=jax.ShapeDtypeStruct((batch * seq, D_FOX), BF16),
        scratch_shapes=[pltpu.VMEM((2 * FOX_TILES, seq, LANES), BF16),
                        pltpu.VMEM((2 * FOX_TILES, seq, LANES), BF16)],
        compiler_params=pltpu.CompilerParams(
            dimension_semantics=("parallel", "arbitrary"), vmem_limit_bytes=VMEM_LIMIT),
        name="fox_attn",
    )(q, k, v, aq, ak)


SWA_ROWS = SWA_GROUP * WINDOW
SWA_UNROLL = 5
SWA_CHAINS = tuple((p, side) for p in range(D_SWA_KV // LANES) for side in range(2))


def _swa_kernel(sink_ref, q_ref, k_ref, v_ref, o_ref, bias_ref, sink_col_ref):
    seq = q_ref.shape[0]
    nblk = seq // WINDOW
    lane = lax.broadcasted_iota(jnp.int32, (1, LANES), 1)
    lo = lane < HEAD_DIM
    nt = (((1,), (1,)), ((), ()))

    r = lax.broadcasted_iota(jnp.int32, (SWA_ROWS, 2 * WINDOW), 0) % WINDOW
    c = lax.broadcasted_iota(jnp.int32, (SWA_ROWS, 2 * WINDOW), 1)
    bias_ref[...] = jnp.where((c > r) & (c <= r + WINDOW), 0.0, -jnp.inf)
    head = lax.broadcasted_iota(jnp.int32, (SWA_ROWS, LANES), 0) // WINDOW
    for n, (p, side) in enumerate(SWA_CHAINS):
        col = jnp.zeros((SWA_ROWS, LANES), F32)
        for j in range(SWA_GROUP):
            col = jnp.where(head == j, sink_ref[8 * p + 4 * side + j] * LOG2E, col)
        sink_col_ref[n] = col

    def block(q_rows, k_rows, first):
        cols = slice(WINDOW, 2 * WINDOW) if first else slice(0, 2 * WINDOW)

        def scores(n):
            p, side = SWA_CHAINS[n]
            tiles = [q_ref[q_rows, (p * SWA_GROUP + j) * LANES:(p * SWA_GROUP + j + 1) * LANES]
                     for j in range(SWA_GROUP)]
            qst = jnp.concatenate(tiles, axis=0)
            qm = jnp.where(lo == (side == 0), qst, jnp.zeros_like(qst))
            kw = k_ref[k_rows, p * LANES:(p + 1) * LANES]
            return lax.dot_general(qm, kw, nt, preferred_element_type=F32)

        def softmax(n, s):
            s = s + bias_ref[:, cols]
            sk = sink_col_ref[n]
            m = jnp.maximum(jnp.max(s, axis=1, keepdims=True), sk)
            e = jnp.exp2(s - jnp.concatenate([m] * (s.shape[1] // LANES), axis=1))
            l = jnp.sum(e, axis=1, keepdims=True) + jnp.exp2(sk - m)
            return e.astype(BF16), l

        def values(n, e, l):
            p, _ = SWA_CHAINS[n]
            vw = v_ref[k_rows, p * LANES:(p + 1) * LANES]
            return jnp.dot(e, vw, preferred_element_type=F32) / l

        s_buf, e_buf, done = {}, {}, {}
        for t in range(len(SWA_CHAINS) + 2):
            if t < len(SWA_CHAINS):
                s_buf[t] = scores(t)
            if 1 <= t <= len(SWA_CHAINS):
                e_buf[t - 1] = softmax(t - 1, s_buf.pop(t - 1))
            if t >= 2:
                p, side = SWA_CHAINS[t - 2]
                done[side] = values(t - 2, *e_buf.pop(t - 2))
                if side == 1:
                    o_both = jnp.where(lo, done.pop(0), done.pop(1)).astype(BF16)
                    for j in range(SWA_GROUP):
                        tile = p * SWA_GROUP + j
                        o_ref[q_rows, tile * LANES:(tile + 1) * LANES] = o_both[j * WINDOW:(j + 1) * WINDOW, :]

    block(slice(0, WINDOW), slice(0, WINDOW), True)

    def body(n, carry):
        q0 = pl.multiple_of(n * WINDOW, WINDOW)
        k0 = pl.multiple_of((n - 1) * WINDOW, WINDOW)
        block(pl.ds(q0, WINDOW), pl.ds(k0, 2 * WINDOW), False)
        return carry

    lax.fori_loop(1, nblk, body, 0, unroll=SWA_UNROLL)


def _swa(sinks, q, k, v, batch, seq):
    return pl.pallas_call(
        _swa_kernel,
        grid=(batch,),
        in_specs=[pl.BlockSpec(memory_space=pltpu.SMEM),
                  pl.BlockSpec((seq, D_SWA_Q), lambda b: (b, 0)),
                  pl.BlockSpec((seq, D_SWA_KV), lambda b: (b, 0)),
                  pl.BlockSpec((seq, D_SWA_KV), lambda b: (b, 0))],
        out_specs=pl.BlockSpec((seq, D_SWA_Q), lambda b: (b, 0)),
        out_shape=jax.ShapeDtypeStruct((batch * seq, D_SWA_Q), BF16),
        scratch_shapes=[pltpu.VMEM((SWA_ROWS, 2 * WINDOW), F32),
                        pltpu.VMEM((len(SWA_CHAINS), SWA_ROWS, LANES), F32)],
        compiler_params=pltpu.CompilerParams(dimension_semantics=("parallel",), vmem_limit_bytes=VMEM_LIMIT),
        name="swa_attn",
    )(sinks, q, k, v)


def _outproj_kernel(of_ref, os_ref, h_ref, gf_ref, gs_ref, wf_ref, ws_ref, o_ref, wsp_ref):
    @pl.when(pl.program_id(0) == 0)
    def _():
        for k, h in enumerate(SWA_HEAD_ORDER):
            wsp_ref[k * HEAD_DIM:(k + 1) * HEAD_DIM, :] = ws_ref[h * HEAD_DIM:(h + 1) * HEAD_DIM, :]

    rows = [slice(c * OUTPROJ_ROWS, (c + 1) * OUTPROJ_ROWS) for c in range(ROW_TM // OUTPROJ_ROWS)]

    def norms(r):
        return (_rms_rows(of_ref[r, :].astype(F32), gf_ref[...]).astype(BF16),
                _rms_rows(os_ref[r, :].astype(F32), gs_ref[...]).astype(BF16))

    ready = norms(rows[0])
    for c, r in enumerate(rows):
        nf, ns = ready
        if c + 1 < len(rows):
            ready = norms(rows[c + 1])
        acc = jnp.dot(nf, wf_ref[...], preferred_element_type=F32)
        acc = acc + jnp.dot(ns, wsp_ref[...], preferred_element_type=F32)
        o_ref[r, :] = h_ref[r, :] + acc


def _outproj(of, os_, h, gf, gs, w):
    t, d = h.shape
    row = lambda n: pl.BlockSpec((ROW_TM, n), lambda i: (i, 0))
    const = lambda shape: pl.BlockSpec(shape, lambda i: (0, 0))
    half = lambda n: pl.BlockSpec((D_FOX, d), lambda i: (n, 0), pipeline_mode=pl.Buffered(1))
    return pl.pallas_call(
        _outproj_kernel,
        grid=(t // ROW_TM,),
        in_specs=[row(D_FOX), row(D_SWA_Q), row(d), const((1, D_FOX)), const((1, D_SWA_Q)), half(0), half(1)],
        out_specs=row(d),
        out_shape=jax.ShapeDtypeStruct((t, d), F32),
        scratch_shapes=[pltpu.VMEM((D_SWA_Q, d), BF16)],
        compiler_params=pltpu.CompilerParams(dimension_semantics=("arbitrary",), vmem_limit_bytes=VMEM_LIMIT),
        name="outproj",
    )(of, os_, h, gf, gs, w, w)


def _swa_reorder(a, axis):
    shape = a.shape
    a = a.reshape(shape[:axis] + (2, 2, SWA_GROUP, HEAD_DIM) + shape[axis + 1:])
    a = jnp.swapaxes(a, axis + 1, axis + 2)
    return a.reshape(shape)


def _w_in_head_rows():
    gate0 = 3 * D_FOX
    sq0 = gate0 + N_GATE
    starts = [c for c in range(0, gate0, HEAD_DIM)]
    starts += [sq0 + h * HEAD_DIM for h in SWA_HEAD_ORDER]
    starts += [sq0 + D_SWA_Q + c for c in range(0, 2 * D_SWA_KV, HEAD_DIM)]
    starts += [gate0, gate0]
    assert len(starts) * HEAD_DIM == D_IN_PAD
    return np.asarray(starts, np.int32)


def _w_in_prep_kernel(rows_ref, wt_ref, o_ref):
    t = pl.program_id(0)
    pieces = [wt_ref[pl.ds(pl.multiple_of(rows_ref[2 * t + half], 2 * SUBLANES), HEAD_DIM), :]
              for half in range(2)]
    x = jnp.concatenate(pieces, axis=0)
    row = lax.broadcasted_iota(jnp.int32, (LANES, 1), 0)
    x = jnp.where((t == pl.num_programs(0) - 1) & (row >= N_GATE), 0.0, x)
    o_ref[...] = jnp.transpose(x).astype(BF16)


def _w_in_prep(w_in_stack, layer):
    _, d, n = w_in_stack.shape
    wt = jnp.swapaxes(w_in_stack, 1, 2)
    return pl.pallas_call(
        _w_in_prep_kernel,
        grid=(D_IN_PAD // LANES,),
        in_specs=[pl.BlockSpec(memory_space=pltpu.SMEM),
                  pl.BlockSpec((None, n, d), lambda t: (layer, 0, 0), pipeline_mode=pl.Buffered(1))],
        out_specs=pl.BlockSpec((d, LANES), lambda t: (0, t)),
        out_shape=jax.ShapeDtypeStruct((d, D_IN_PAD), BF16),
        compiler_params=pltpu.CompilerParams(dimension_semantics=("arbitrary",), vmem_limit_bytes=VMEM_LIMIT),
        name="w_in_prep",
    )(jnp.asarray(_w_in_head_rows()), wt)


def _layer(h, pos, invf, p, w_in_stack, layer):
    (norm_ffn1_g, w_gate1, w_up1, w_down1, norm_mix_g, b_forget, fox_q_g, fox_k_g, swa_q_g, swa_k_g,
     sinks, out_fox_g, out_swa_g, w_out, norm_ffn2_g, w_gate2, w_up2, w_down2) = p
    batch_seq, d = h.shape
    batch, seq = pos.shape
    row = lambda v: v.reshape(1, -1).astype(F32)
    tile = lambda v, n: jnp.tile(v.astype(F32), n).reshape(1, -1)

    w_in_r = _w_in_prep(w_in_stack, layer)
    bf_pad = jnp.concatenate([b_forget.astype(F32), jnp.zeros((LANES - N_GATE,), F32)]).reshape(1, LANES)

    h1_head, w_gate1_b, w_up1_b, w_down1_b = _ffn_head(h, row(norm_ffn1_g), w_gate1, w_up1, w_down1)
    h1, (w_gate2_b, w_up2_b, w_down2_b) = _ffn(
        h, row(norm_ffn1_g), w_gate1_b, w_up1_b, w_down1_b, cast=(w_gate2, w_up2, w_down2), head=h1_head)

    qf, kf, vf, qs, ks, vs, lf = _inproj(
        h1, row(norm_mix_g), w_in_r, pos.reshape(batch_seq, 1), invf,
        tile(fox_q_g, FOX_HEADS), tile(fox_k_g, FOX_HEADS), tile(swa_q_g, SWA_Q_HEADS),
        tile(swa_k_g, SWA_KV_HEADS), bf_pad)

    ak, aq = _cumsum(lf, batch, seq)
    o_fox = _fox(qf, kf, vf, aq, ak, batch, seq)
    o_swa = _swa(sinks.astype(F32), qs, ks, vs, batch, seq)

    h2 = _outproj(o_fox, o_swa, h1, row(out_fox_g), row(_swa_reorder(out_swa_g, 0)), w_out.astype(BF16))
    return _ffn(h2, row(norm_ffn2_g), w_gate2_b, w_up2_b, w_down2_b)[0]


def kernel(x, positions, norm_ffn1_g, ffn1_w_gate, ffn1_w_up, ffn1_w_down, norm_mix_g, w_in, b_forget, fox_q_norm_g, fox_k_norm_g, swa_q_norm_g, swa_k_norm_g, swa_sinks, out_norm_fox_g, out_norm_swa_g, w_out, norm_ffn2_g, ffn2_w_gate, ffn2_w_up, ffn2_w_down):
    batch, seq, d = x.shape
    half = jnp.arange(0, HEAD_DIM, 2, dtype=F32)
    inv_freq = ROPE_THETA ** (-half / HEAD_DIM)
    invf = jnp.tile(inv_freq, LANES // (HEAD_DIM // 2)).reshape(1, LANES)
    stacks = (norm_ffn1_g, ffn1_w_gate, ffn1_w_up, ffn1_w_down, norm_mix_g, b_forget, fox_q_norm_g,
              fox_k_norm_g, swa_q_norm_g, swa_k_norm_g, swa_sinks, out_norm_fox_g, out_norm_swa_g, w_out,
              norm_ffn2_g, ffn2_w_gate, ffn2_w_up, ffn2_w_down)
    h = x.reshape(batch * seq, d)
    for layer in range(norm_ffn1_g.shape[0]):
        h = _layer(h, positions, invf, tuple(s[layer] for s in stacks), w_in, layer)
    return h.reshape(batch, seq, d)
```

```python
import functools

import jax
import jax.numpy as jnp
import numpy as np
from jax import lax
from jax.experimental import pallas as pl
from jax.experimental.pallas import tpu as pltpu

F32 = jnp.float32
BF16 = jnp.bfloat16

D_MODEL = 2048
HEAD_DIM = 64
FOX_HEADS = 16
SWA_Q_HEADS = 16
SWA_KV_HEADS = 4
SWA_GROUP = SWA_Q_HEADS // SWA_KV_HEADS
WINDOW = 128
D_FF = 5632
ROPE_THETA = 10000.0
EPS = 1e-6

D_FOX = FOX_HEADS * HEAD_DIM
D_SWA_Q = SWA_Q_HEADS * HEAD_DIM
D_SWA_KV = SWA_KV_HEADS * HEAD_DIM
N_GATE = FOX_HEADS

LANES = 128
SUBLANES = 8
MXU_TILE = 256
VMEM_LIMIT = 60 * 1024 * 1024

C_FQ = 0
C_FK = C_FQ + D_FOX
C_FV = C_FK + D_FOX
C_SQ = C_FV + D_FOX
C_SK = C_SQ + D_SWA_Q
C_SV = C_SK + D_SWA_KV
C_GATE = C_SV + D_SWA_KV
D_IN_PAD = C_GATE + LANES

SWA_HEAD_ORDER = tuple(8 * p + 4 * side + j for p in range(2) for j in range(SWA_GROUP) for side in range(2))

FFN_TM = 1024
FFN_TF = 512
HEAD_TF = 256
HEAD_CHUNKS = 8
ROW_TM = 512
INPROJ_TN = 512
OUTPROJ_TM = 512
OUTPROJ_ROWS = 256
INPROJ_DEPTH = 1
FOX_TQ = 256
FOX_TILES = 2
FOX_LAG_S, FOX_LAG_V = 1, 2
CUM_BLK = 256

LOG2E = 1.4426950408889634

AUG_PARTS = 3
ONE_LANE = AUG_PARTS * N_GATE


def _rms_rows(x, g):
    return x * lax.rsqrt(jnp.mean(x * x, axis=-1, keepdims=True) + EPS) * g


def _swiglu_tile(u, wg, wu, wd):
    gate = jnp.dot(u, wg, preferred_element_type=F32)
    up = jnp.dot(u, wu, preferred_element_type=F32)
    a = (gate * jax.nn.sigmoid(gate) * up * 0.5).astype(BF16)
    return jnp.dot(a, wd, preferred_element_type=F32)


def _ffn_first(x_ref, g_ref, u_ref, wg, wu, wd, o_ref):
    x = x_ref[...]
    u = _rms_rows(x, g_ref[...]).astype(BF16)
    u_ref[...] = u
    o_ref[...] = x + _swiglu_tile(u, wg, wu, wd)


def _ffn_step(u_ref, wg, wu, wd, o_ref):
    o_ref[...] += _swiglu_tile(u_ref[...], wg, wu, wd)


def _ffn_kernel(x_ref, g_ref, wg_ref, wu_ref, wd_ref, *refs, n_cast, has_head):
    head_ref = refs[0] if has_head else None
    refs = refs[1:] if has_head else refs
    cast_in, o_ref, cast_out, u_ref = refs[:n_cast], refs[n_cast], refs[n_cast + 1:-1], refs[-1]
    i, j = pl.program_id(0), pl.program_id(1)

    def side_job():
        for src, dst in zip(cast_in, cast_out):
            dst[...] = src[...].astype(BF16)

    if has_head:
        @pl.when(i == 0)
        def _():
            @pl.when(j < HEAD_CHUNKS)
            def _():
                rows = FFN_TM // HEAD_CHUNKS
                o_ref[pl.ds(pl.multiple_of(j * rows, rows), rows), :] = head_ref[...]

            side_job()

        computing = i > 0
    else:
        computing = True

    @pl.when(computing & (j == 0))
    def _():
        _ffn_first(x_ref, g_ref, u_ref, wg_ref[...], wu_ref[...], wd_ref[...], o_ref)
        side_job()

    @pl.when(computing & (j > 0))
    def _():
        _ffn_step(u_ref, wg_ref[...], wu_ref[...], wd_ref[...], o_ref)
        side_job()


def _ffn(x, g, wg, wu, wd, cast=(), head=None):
    t, d = x.shape
    f = wg.shape[1]
    ni, nj = t // FFN_TM, f // FFN_TF
    has_head = head is not None
    live = (lambda i: jnp.minimum(i, 1)) if has_head else (lambda i: 1)
    row = (lambda i: jnp.maximum(i, 1)) if has_head else (lambda i: i)
    cast_specs = []
    for w in cast:
        r, c = w.shape
        if r % ni == 0 and c % nj == 0 and (c // nj) % LANES == 0:
            cast_specs.append(pl.BlockSpec((r // ni, c // nj), lambda i, j: (i, j)))
        elif r % nj == 0 and c % ni == 0 and (c // ni) % LANES == 0:
            cast_specs.append(pl.BlockSpec((r // nj, c // ni), lambda i, j: (j, i)))
        else:
            assert r % ni == 0, w.shape
            cast_specs.append(pl.BlockSpec((r // ni, c), lambda i, j: (i, 0)))
    head_specs, head_args = [], []
    if has_head:
        last = HEAD_CHUNKS - 1
        head_specs = [pl.BlockSpec((FFN_TM // HEAD_CHUNKS, d),
                                   lambda i, j: (jnp.where(i == 0, jnp.minimum(j, last), last), 0))]
        head_args = [head]
    outs = pl.pallas_call(
        functools.partial(_ffn_kernel, n_cast=len(cast), has_head=has_head),
        grid=(ni, nj),
        in_specs=[
            pl.BlockSpec((FFN_TM, d), lambda i, j: (row(i), 0)),
            pl.BlockSpec((1, d), lambda i, j: (0, 0)),
            pl.BlockSpec((d, FFN_TF), lambda i, j: (0, j * live(i))),
            pl.BlockSpec((d, FFN_TF), lambda i, j: (0, j * live(i))),
            pl.BlockSpec((FFN_TF, d), lambda i, j: (j * live(i), 0)),
        ] + head_specs + cast_specs,
        out_specs=[pl.BlockSpec((FFN_TM, d), lambda i, j: (i, 0))] + cast_specs,
        out_shape=[jax.ShapeDtypeStruct((t, d), F32)] + [jax.ShapeDtypeStruct(w.shape, BF16) for w in cast],
        scratch_shapes=[pltpu.VMEM((FFN_TM, d), BF16)],
        compiler_params=pltpu.CompilerParams(
            dimension_semantics=("parallel", "arbitrary"), vmem_limit_bytes=VMEM_LIMIT),
        name="ffn_cast" if cast else "ffn",
    )(x, g, wg, wu, wd, *head_args, *cast)
    return outs[0], tuple(outs[1:])


def _ffn_head_kernel(x_ref, g_ref, wg_ref, wu_ref, wd_ref, o_ref, wgb_ref, wub_ref, wdb_ref, u_ref):
    wgb_ref[...] = wg_ref[...].astype(BF16)
    wub_ref[...] = wu_ref[...].astype(BF16)
    wdb_ref[...] = wd_ref[...].astype(BF16)

    @pl.when(pl.program_id(0) == 0)
    def _():
        _ffn_first(x_ref, g_ref, u_ref, wgb_ref[...], wub_ref[...], wdb_ref[...], o_ref)

    @pl.when(pl.program_id(0) > 0)
    def _():
        _ffn_step(u_ref, wgb_ref[...], wub_ref[...], wdb_ref[...], o_ref)


def _ffn_head(x, g, wg, wu, wd):
    t, d = x.shape
    f = wg.shape[1]
    once = lambda shape: pl.BlockSpec(shape, lambda j: (0, 0), pipeline_mode=pl.Buffered(1))
    col = pl.BlockSpec((d, HEAD_TF), lambda j: (0, j))
    rowb = pl.BlockSpec((HEAD_TF, d), lambda j: (j, 0))
    return pl.pallas_call(
        _ffn_head_kernel,
        grid=(f // HEAD_TF,),
        in_specs=[once((FFN_TM, d)), once((1, d)), col, col, rowb],
        out_specs=[pl.BlockSpec((FFN_TM, d), lambda j: (0, 0)), col, col, rowb],
        out_shape=[jax.ShapeDtypeStruct((FFN_TM, d), F32), jax.ShapeDtypeStruct(wg.shape, BF16),
                   jax.ShapeDtypeStruct(wu.shape, BF16), jax.ShapeDtypeStruct(wd.shape, BF16)],
        scratch_shapes=[pltpu.VMEM((FFN_TM, d), BF16)],
        compiler_params=pltpu.CompilerParams(dimension_semantics=("arbitrary",), vmem_limit_bytes=VMEM_LIMIT),
        name="ffn_head",
    )(x, g, wg, wu, wd)


def _head_rms(y, gain, bd):
    n = y.shape[1]
    y2 = (y * y).astype(BF16)
    parts = [jnp.dot(y2[:, c * MXU_TILE:(c + 1) * MXU_TILE], bd, preferred_element_type=F32)
             for c in range(n // MXU_TILE)]
    ss = parts[0] if len(parts) == 1 else jnp.concatenate(parts, axis=1)
    return y * lax.rsqrt(ss * (1.0 / HEAD_DIM) + EPS) * gain


def _rope(y, cos, sin_signed, first_half):
    outs = []
    for c in range(y.shape[1] // LANES):
        yc = y[:, c * LANES:(c + 1) * LANES]
        partner = jnp.where(first_half, pltpu.roll(yc, LANES - HEAD_DIM // 2, 1),
                            pltpu.roll(yc, HEAD_DIM // 2, 1))
        outs.append(yc * cos + partner * sin_signed)
    return outs[0] if len(outs) == 1 else jnp.concatenate(outs, axis=1)


def _inproj_kernel(h_ref, g_ref, w_ref, pos_ref, invf_ref, gfq_ref, gfk_ref, gsq_ref, gsk_ref, bf_ref, cast_ref,
                   qf_ref, kf_ref, vf_ref, qs_ref, ks_ref, vs_ref, lf_ref, cast_out_ref):
    u = _rms_rows(h_ref[...], g_ref[...]).astype(BF16)
    cast_out_ref[...] = cast_ref[...].astype(BF16)

    lane = lax.broadcasted_iota(jnp.int32, (1, LANES), 1)
    first_half = (lane % HEAD_DIM) < (HEAD_DIM // 2)
    rope_tables = []

    def rope_table():
        if not rope_tables:
            ang = pos_ref[...].astype(F32) * invf_ref[...]
            sin = jnp.sin(ang)
            rope_tables.append((jnp.cos(ang), jnp.where(first_half, -sin, sin)))
        return rope_tables[0]

    r = lax.broadcasted_iota(jnp.int32, (MXU_TILE, MXU_TILE), 0) // HEAD_DIM
    c = lax.broadcasted_iota(jnp.int32, (MXU_TILE, MXU_TILE), 1) // HEAD_DIM
    bd = jnp.where(r == c, 1.0, 0.0).astype(BF16)

    scale = HEAD_DIM ** -0.5

    def fox_q(y, lo, hi):
        qf_ref[:, lo:hi] = (_head_rms(y, gfq_ref[:, lo:hi], bd) * (scale * LOG2E)).astype(BF16)

    def fox_k(y, lo, hi):
        kf_ref[:, lo:hi] = _head_rms(y, gfk_ref[:, lo:hi], bd).astype(BF16)

    def fox_v(y, lo, hi):
        vf_ref[:, lo:hi] = y.astype(BF16)
        rope_table()

    def swa_q(y, lo, hi):
        y = _rope(_head_rms(y, gsq_ref[:, lo:hi], bd), *rope_table(), first_half)
        qs_ref[:, lo:hi] = (y * (scale * LOG2E)).astype(BF16)

    def swa_k(y, lo, hi):
        ks_ref[:, lo:hi] = _rope(_head_rms(y, gsk_ref[:, lo:hi], bd), *rope_table(), first_half).astype(BF16)

    def swa_v(y, lo, hi):
        vs_ref[:, lo:hi] = y.astype(BF16)

    def gate(y, lo, hi):
        z = y + bf_ref[...]
        lf_ref[...] = jnp.minimum(z, 0.0) - jnp.log1p(jnp.exp(-jnp.abs(z)))

    stages = []
    for start, end, epilogue in ((C_FQ, C_FK, fox_q), (C_FK, C_FV, fox_k), (C_FV, C_SQ, fox_v),
                                 (C_SQ, C_SK, swa_q), (C_SK, C_SV, swa_k), (C_SV, C_GATE, swa_v),
                                 (C_GATE, D_IN_PAD, gate)):
        for lo in range(start, end, INPROJ_TN):
            hi = min(lo + INPROJ_TN, end)
            stages.append((lo, hi, lo - start, hi - start, epilogue))
    pending = []
    for lo, hi, olo, ohi, epilogue in stages:
        y = jnp.dot(u, w_ref[:, lo:hi], preferred_element_type=F32)
        if len(pending) >= INPROJ_DEPTH:
            fn, *args = pending.pop(0)
            fn(*args)
        pending.append((epilogue, y, olo, ohi))
    for fn, *args in pending:
        fn(*args)


def _inproj(h, g, w, pos, invf, gfq, gfk, gsq, gsk, bf, cast):
    t, d = h.shape
    steps = t // ROW_TM
    row = lambda n: pl.BlockSpec((ROW_TM, n), lambda i: (i, 0))
    const = lambda shape: pl.BlockSpec(shape, lambda i: (0, 0))
    cast_spec = pl.BlockSpec((cast.shape[0] // steps, cast.shape[1]), lambda i: (i, 0))
    return pl.pallas_call(
        _inproj_kernel,
        grid=(steps,),
        in_specs=[
            row(d),
            const((1, d)),
            pl.BlockSpec((d, D_IN_PAD), lambda i: (0, 0), pipeline_mode=pl.Buffered(1)),
            row(1),
            const((1, LANES)),
            const((1, D_FOX)), const((1, D_FOX)), const((1, D_SWA_Q)), const((1, D_SWA_KV)),
            const((1, LANES)),
            cast_spec,
        ],
        out_specs=[row(D_FOX), row(D_FOX), row(D_FOX), row(D_SWA_Q), row(D_SWA_KV), row(D_SWA_KV), row(LANES),
                   cast_spec],
        out_shape=[
            jax.ShapeDtypeStruct((t, D_FOX), BF16), jax.ShapeDtypeStruct((t, D_FOX), BF16),
            jax.ShapeDtypeStruct((t, D_FOX), BF16), jax.ShapeDtypeStruct((t, D_SWA_Q), BF16),
            jax.ShapeDtypeStruct((t, D_SWA_KV), BF16), jax.ShapeDtypeStruct((t, D_SWA_KV), BF16),
            jax.ShapeDtypeStruct((t, LANES), F32), jax.ShapeDtypeStruct(cast.shape, BF16),
        ],
        compiler_params=pltpu.CompilerParams(dimension_semantics=("parallel",), vmem_limit_bytes=VMEM_LIMIT),
        name="inproj",
    )(h, g, w, pos, invf, gfq, gfk, gsq, gsk, bf, cast)


def _split3(x):
    p1 = x.astype(BF16)
    r1 = x - p1.astype(F32)
    p2 = r1.astype(BF16)
    p3 = (r1 - p2.astype(F32)).astype(BF16)
    return p1, p2, p3


def _cumsum_kernel(lf_ref, pmap_ref, ak_ref, aq_ref):
    s = lf_ref.shape[0]
    r = lax.broadcasted_iota(jnp.int32, (CUM_BLK, CUM_BLK), 0)
    c = lax.broadcasted_iota(jnp.int32, (CUM_BLK, CUM_BLK), 1)
    tri = jnp.where(r >= c, 1.0, 0.0).astype(BF16)
    lane = lax.broadcasted_iota(jnp.int32, (1, LANES), 1)
    is_gate = lane < N_GATE
    one = jnp.where(lane == ONE_LANE, 1.0, 0.0)
    local = []
    for b in range(s // CUM_BLK):
        p1, p2, p3 = _split3(lf_ref[b * CUM_BLK:(b + 1) * CUM_BLK, :])
        local.append(jnp.dot(tri, p1, preferred_element_type=F32)
                     + jnp.dot(tri, p2, preferred_element_type=F32)
                     + jnp.dot(tri, p3, preferred_element_type=F32))
    carry = jnp.zeros((1, LANES), F32)
    for b in range(s // CUM_BLK):
        rows = slice(b * CUM_BLK, (b + 1) * CUM_BLK)
        cb = local[b] + carry
        carry = cb[CUM_BLK - 1:CUM_BLK, :]
        terms = [jnp.where(is_gate, t.astype(F32), 0.0) for t in _split3(cb * LOG2E)]
        packed = one + terms[0]
        for i in range(1, AUG_PARTS):
            packed = packed + pltpu.roll(terms[i], i * N_GATE, 1)
        placed = jnp.dot(packed.astype(BF16), pmap_ref[...], preferred_element_type=F32)
        ak_ref[rows, :] = placed[:, :D_FOX].astype(BF16)
        aq_ref[rows, :] = placed[:, D_FOX:].astype(BF16)


def _aug_map():
    pmap = np.zeros((LANES, 2 * D_FOX), np.float32)
    for h in range(FOX_HEADS):
        base = (h // 2) * LANES + (HEAD_DIM if h % 2 == 0 else 0)
        for i in range(AUG_PARTS):
            pmap[i * N_GATE + h, base + i] = -1.0
            pmap[ONE_LANE, D_FOX + base + i] = 1.0
        pmap[ONE_LANE, base + AUG_PARTS] = 1.0
        pmap[h, D_FOX + base + AUG_PARTS] = 1.0
    return jnp.asarray(pmap, BF16)


def _cumsum(lf, batch, seq):
    pmap = _aug_map()
    out = pl.BlockSpec((seq, D_FOX), lambda b: (b, 0))
    return pl.pallas_call(
        _cumsum_kernel,
        grid=(batch,),
        in_specs=[pl.BlockSpec((seq, LANES), lambda b: (b, 0)), pl.BlockSpec(pmap.shape, lambda b: (0, 0))],
        out_specs=[out, out],
        out_shape=[jax.ShapeDtypeStruct((batch * seq, D_FOX), BF16)] * 2,
        compiler_params=pltpu.CompilerParams(dimension_semantics=("parallel",)),
        name="gate_cumsum",
    )(lf, pmap)


def _fox_kernel(q_ref, k_ref, v_ref, aq_ref, ak_ref, o_ref, qa_ref, ka_ref):
    seq = q_ref.shape[0]
    lane = lax.broadcasted_iota(jnp.int32, (1, LANES), 1)
    lo = lane < HEAD_DIM
    for tile in range(FOX_TILES):
        cols = slice(tile * LANES, (tile + 1) * LANES)
        q, k, aq, ak = q_ref[:, cols], k_ref[:, cols], aq_ref[:, cols], ak_ref[:, cols]
        qa_ref[2 * tile] = jnp.where(lo, q, aq)
        qa_ref[2 * tile + 1] = jnp.where(lo, aq, q)
        ka_ref[2 * tile] = jnp.where(lo, k, ak)
        ka_ref[2 * tile + 1] = jnp.where(lo, ak, k)

    qry = lax.broadcasted_iota(jnp.int32, (FOX_TQ, FOX_TQ), 0)
    key = lax.broadcasted_iota(jnp.int32, (FOX_TQ, FOX_TQ), 1)
    causal = key <= qry
    nt = (((1,), (1,)), ((), ()))

    def scores(i, head):
        rows = slice(i * FOX_TQ, (i + 1) * FOX_TQ)
        return lax.dot_general(qa_ref[head, rows, :], ka_ref[head, 0:(i + 1) * FOX_TQ, :], nt,
                               preferred_element_type=F32)

    def softmax(i, s):
        w = (i + 1) * FOX_TQ
        diag = jnp.where(causal, s[:, w - FOX_TQ:], -jnp.inf)
        s = diag if i == 0 else jnp.concatenate([s[:, :w - FOX_TQ], diag], axis=1)
        m = jnp.max(s, axis=1, keepdims=True)
        p = jnp.exp2(s - m)
        return p.astype(BF16), jnp.sum(p, axis=1, keepdims=True)

    def values(i, head, p, l):
        tile = head // 2
        o = jnp.dot(p, v_ref[0:(i + 1) * FOX_TQ, tile * LANES:(tile + 1) * LANES],
                    preferred_element_type=F32)
        return o / l

    chains = [(i, head) for i in reversed(range(seq // FOX_TQ)) for head in range(2 * FOX_TILES)]
    s_buf, p_buf, done = {}, {}, {}
    for t in range(len(chains) + FOX_LAG_V):
        if t < len(chains):
            s_buf[t] = scores(*chains[t])
        if FOX_LAG_S <= t < len(chains) + FOX_LAG_S:
            p_buf[t - FOX_LAG_S] = softmax(chains[t - FOX_LAG_S][0], s_buf.pop(t - FOX_LAG_S))
        if t >= FOX_LAG_V:
            i, head = chains[t - FOX_LAG_V]
            done[(i, head)] = values(i, head, *p_buf.pop(t - FOX_LAG_V))
            if head % 2 == 1:
                rows = slice(i * FOX_TQ, (i + 1) * FOX_TQ)
                cols = slice((head // 2) * LANES, (head // 2 + 1) * LANES)
                o_ref[rows, cols] = jnp.where(lo, done.pop((i, head - 1)), done.pop((i, head))).astype(BF16)


def _fox(q, k, v, aq, ak, batch, seq):
    width = FOX_TILES * LANES
    blk = pl.BlockSpec((seq, width), lambda b, p: (b, p))
    return pl.pallas_call(
        _fox_kernel,
        grid=(batch, D_FOX // width),
        in_specs=[blk] * 5,
        out_specs=blk,
        out_shape=jax.ShapeDtypeStruct((batch * seq, D_FOX), BF16),
        scratch_shapes=[pltpu.VMEM((2 * FOX_TILES, seq, LANES), BF16),
                        pltpu.VMEM((2 * FOX_TILES, seq, LANES), BF16)],
        compiler_params=pltpu.CompilerParams(
            dimension_semantics=("parallel", "arbitrary"), vmem_limit_bytes=VMEM_LIMIT),
        name="fox_attn",
    )(q, k, v, aq, ak)


SWA_ROWS = SWA_GROUP * WINDOW
SWA_UNROLL = 15
SWA_CHAINS = tuple((p, side) for p in range(D_SWA_KV // LANES) for side in range(2))


def _swa_kernel(sink_ref, q_ref, k_ref, v_ref, o_ref, bias_ref, sink_col_ref):
    seq = q_ref.shape[0]
    nblk = seq // WINDOW
    lane = lax.broadcasted_iota(jnp.int32, (1, LANES), 1)
    lo = lane < HEAD_DIM
    nt = (((1,), (1,)), ((), ()))

    r = lax.broadcasted_iota(jnp.int32, (SWA_ROWS, 2 * WINDOW), 0) % WINDOW
    c = lax.broadcasted_iota(jnp.int32, (SWA_ROWS, 2 * WINDOW), 1)
    bias_ref[...] = jnp.where((c > r) & (c <= r + WINDOW), 0.0, -jnp.inf)
    head = lax.broadcasted_iota(jnp.int32, (SWA_ROWS, LANES), 0) // WINDOW
    for n, (p, side) in enumerate(SWA_CHAINS):
        col = jnp.zeros((SWA_ROWS, LANES), F32)
        for j in range(SWA_GROUP):
            col = jnp.where(head == j, sink_ref[8 * p + 4 * side + j] * LOG2E, col)
        sink_col_ref[n] = col

    def block(q_rows, k_rows, first):
        cols = slice(WINDOW, 2 * WINDOW) if first else slice(0, 2 * WINDOW)

        def scores(n):
            p, side = SWA_CHAINS[n]
            tiles = [q_ref[q_rows, (p * SWA_GROUP + j) * LANES:(p * SWA_GROUP + j + 1) * LANES]
                     for j in range(SWA_GROUP)]
            qst = jnp.concatenate(tiles, axis=0)
            qm = jnp.where(lo == (side == 0), qst, jnp.zeros_like(qst))
            kw = k_ref[k_rows, p * LANES:(p + 1) * LANES]
            return lax.dot_general(qm, kw, nt, preferred_element_type=F32)

        def softmax(n, s):
            s = s + bias_ref[:, cols]
            sk = sink_col_ref[n]
            m = jnp.maximum(jnp.max(s, axis=1, keepdims=True), sk)
            e = jnp.exp2(s - jnp.concatenate([m] * (s.shape[1] // LANES), axis=1))
            l = jnp.sum(e, axis=1, keepdims=True) + jnp.exp2(sk - m)
            return e.astype(BF16), l

        def values(n, e, l):
            p, _ = SWA_CHAINS[n]
            vw = v_ref[k_rows, p * LANES:(p + 1) * LANES]
            return jnp.dot(e, vw, preferred_element_type=F32) / l

        s_buf, e_buf, done = {}, {}, {}
        for t in range(len(SWA_CHAINS) + 2):
            if t < len(SWA_CHAINS):
                s_buf[t] = scores(t)
            if 1 <= t <= len(SWA_CHAINS):
                e_buf[t - 1] = softmax(t - 1, s_buf.pop(t - 1))
            if t >= 2:
                p, side = SWA_CHAINS[t - 2]
                done[side] = values(t - 2, *e_buf.pop(t - 2))
                if side == 1:
                    o_both = jnp.where(lo, done.pop(0), done.pop(1)).astype(BF16)
                    for j in range(SWA_GROUP):
                        tile = p * SWA_GROUP + j
                        o_ref[q_rows, tile * LANES:(tile + 1) * LANES] = o_both[j * WINDOW:(j + 1) * WINDOW, :]

    block(slice(0, WINDOW), slice(0, WINDOW), True)

    def body(n, carry):
        q0 = pl.multiple_of(n * WINDOW, WINDOW)
        k0 = pl.multiple_of((n - 1) * WINDOW, WINDOW)
        block(pl.ds(q0, WINDOW), pl.ds(k0, 2 * WINDOW), False)
        return carry

    lax.fori_loop(1, nblk, body, 0, unroll=SWA_UNROLL)


def _swa(sinks, q, k, v, batch, seq):
    return pl.pallas_call(
        _swa_kernel,
        grid=(batch,),
        in_specs=[pl.BlockSpec(memory_space=pltpu.SMEM),
                  pl.BlockSpec((seq, D_SWA_Q), lambda b: (b, 0)),
                  pl.BlockSpec((seq, D_SWA_KV), lambda b: (b, 0)),
                  pl.BlockSpec((seq, D_SWA_KV), lambda b: (b, 0))],
        out_specs=pl.BlockSpec((seq, D_SWA_Q), lambda b: (b, 0)),
        out_shape=jax.ShapeDtypeStruct((batch * seq, D_SWA_Q), BF16),
        scratch_shapes=[pltpu.VMEM((SWA_ROWS, 2 * WINDOW), F32),
                        pltpu.VMEM((len(SWA_CHAINS), SWA_ROWS, LANES), F32)],
        compiler_params=pltpu.CompilerParams(dimension_semantics=("parallel",), vmem_limit_bytes=VMEM_LIMIT),
        name="swa_attn",
    )(sinks, q, k, v)


def _outproj_kernel(of_ref, os_ref, h_ref, gf_ref, gs_ref, wf_ref, ws_ref, o_ref, wsp_ref):
    @pl.when(pl.program_id(0) == 0)
    def _():
        for k, h in enumerate(SWA_HEAD_ORDER):
            wsp_ref[k * HEAD_DIM:(k + 1) * HEAD_DIM, :] = ws_ref[h * HEAD_DIM:(h + 1) * HEAD_DIM, :]

    rows = [slice(c * OUTPROJ_ROWS, (c + 1) * OUTPROJ_ROWS) for c in range(OUTPROJ_TM // OUTPROJ_ROWS)]

    def norms(r):
        return (_rms_rows(of_ref[r, :].astype(F32), gf_ref[...]).astype(BF16),
                _rms_rows(os_ref[r, :].astype(F32), gs_ref[...]).astype(BF16))

    ready = norms(rows[0])
    for c, r in enumerate(rows):
        nf, ns = ready
        if c + 1 < len(rows):
            ready = norms(rows[c + 1])
        acc = jnp.dot(nf, wf_ref[...], preferred_element_type=F32)
        acc = acc + jnp.dot(ns, wsp_ref[...], preferred_element_type=F32)
        o_ref[r, :] = h_ref[r, :] + acc


def _outproj(of, os_, h, gf, gs, w):
    t, d = h.shape
    row = lambda n: pl.BlockSpec((OUTPROJ_TM, n), lambda i: (i, 0))
    const = lambda shape: pl.BlockSpec(shape, lambda i: (0, 0))
    half = lambda n: pl.BlockSpec((D_FOX, d), lambda i: (n, 0), pipeline_mode=pl.Buffered(1))
    return pl.pallas_call(
        _outproj_kernel,
        grid=(t // OUTPROJ_TM,),
        in_specs=[row(D_FOX), row(D_SWA_Q), row(d), const((1, D_FOX)), const((1, D_SWA_Q)), half(0), half(1)],
        out_specs=row(d),
        out_shape=jax.ShapeDtypeStruct((t, d), F32),
        scratch_shapes=[pltpu.VMEM((D_SWA_Q, d), BF16)],
        compiler_params=pltpu.CompilerParams(dimension_semantics=("arbitrary",), vmem_limit_bytes=VMEM_LIMIT),
        name="outproj",
    )(of, os_, h, gf, gs, w, w)


def _swa_reorder(a, axis):
    shape = a.shape
    a = a.reshape(shape[:axis] + (2, 2, SWA_GROUP, HEAD_DIM) + shape[axis + 1:])
    a = jnp.swapaxes(a, axis + 1, axis + 2)
    return a.reshape(shape)


def _w_in_head_rows():
    gate0 = 3 * D_FOX
    sq0 = gate0 + N_GATE
    starts = [c for c in range(0, gate0, HEAD_DIM)]
    starts += [sq0 + h * HEAD_DIM for h in SWA_HEAD_ORDER]
    starts += [sq0 + D_SWA_Q + c for c in range(0, 2 * D_SWA_KV, HEAD_DIM)]
    starts += [gate0, gate0]
    assert len(starts) * HEAD_DIM == D_IN_PAD
    return np.asarray(starts, np.int32)


def _w_in_prep_kernel(rows_ref, wt_ref, o_ref):
    t = pl.program_id(0)
    pieces = [wt_ref[pl.ds(pl.multiple_of(rows_ref[2 * t + half], 2 * SUBLANES), HEAD_DIM), :]
              for half in range(2)]
    x = jnp.concatenate(pieces, axis=0)
    row = lax.broadcasted_iota(jnp.int32, (LANES, 1), 0)
    x = jnp.where((t == pl.num_programs(0) - 1) & (row >= N_GATE), 0.0, x)
    o_ref[...] = jnp.transpose(x).astype(BF16)


def _w_in_prep(w_in_stack, layer):
    _, d, n = w_in_stack.shape
    wt = jnp.swapaxes(w_in_stack, 1, 2)
    return pl.pallas_call(
        _w_in_prep_kernel,
        grid=(D_IN_PAD // LANES,),
        in_specs=[pl.BlockSpec(memory_space=pltpu.SMEM),
                  pl.BlockSpec((None, n, d), lambda t: (layer, 0, 0), pipeline_mode=pl.Buffered(1))],
        out_specs=pl.BlockSpec((d, LANES), lambda t: (0, t)),
        out_shape=jax.ShapeDtypeStruct((d, D_IN_PAD), BF16),
        compiler_params=pltpu.CompilerParams(dimension_semantics=("arbitrary",), vmem_limit_bytes=VMEM_LIMIT),
        name="w_in_prep",
    )(jnp.asarray(_w_in_head_rows()), wt)


def _layer(h, pos, invf, p, w_in_stack, layer):
    (norm_ffn1_g, w_gate1, w_up1, w_down1, norm_mix_g, b_forget, fox_q_g, fox_k_g, swa_q_g, swa_k_g,
     sinks, out_fox_g, out_swa_g, w_out, norm_ffn2_g, w_gate2, w_up2, w_down2) = p
    batch_seq, d = h.shape
    batch, seq = pos.shape
    row = lambda v: v.reshape(1, -1).astype(F32)
    tile = lambda v, n: jnp.tile(v.astype(F32), n).reshape(1, -1)

    w_in_r = _w_in_prep(w_in_stack, layer)
    bf_pad = jnp.concatenate([b_forget.astype(F32), jnp.zeros((LANES - N_GATE,), F32)]).reshape(1, LANES)

    h1_head, w_gate1_b, w_up1_b, w_down1_b = _ffn_head(h, row(norm_ffn1_g), w_gate1, w_up1, w_down1)
    h1, (w_gate2_b, w_up2_b, w_down2_b) = _ffn(
        h, row(norm_ffn1_g), w_gate1_b, w_up1_b, w_down1_b, cast=(w_gate2, w_up2, w_down2), head=h1_head)

    qf, kf, vf, qs, ks, vs, lf, w_out_b = _inproj(
        h1, row(norm_mix_g), w_in_r, pos.reshape(batch_seq, 1), invf,
        tile(fox_q_g, FOX_HEADS), tile(fox_k_g, FOX_HEADS), tile(swa_q_g, SWA_Q_HEADS),
        tile(swa_k_g, SWA_KV_HEADS), bf_pad, cast=w_out)

    ak, aq = _cumsum(lf, batch, seq)
    o_fox = _fox(qf, kf, vf, aq, ak, batch, seq)
    o_swa = _swa(sinks.astype(F32), qs, ks, vs, batch, seq)

    h2 = _outproj(o_fox, o_swa, h1, row(out_fox_g), row(_swa_reorder(out_swa_g, 0)), w_out_b)
    return _ffn(h2, row(norm_ffn2_g), w_gate2_b, w_up2_b, w_down2_b)[0]


def kernel(x, positions, norm_ffn1_g, ffn1_w_gate, ffn1_w_up, ffn1_w_down, norm_mix_g, w_in, b_forget, fox_q_norm_g, fox_k_norm_g, swa_q_norm_g, swa_k_norm_g, swa_sinks, out_norm_fox_g, out_norm_swa_g, w_out, norm_ffn2_g, ffn2_w_gate, ffn2_w_up, ffn2_w_down):
    batch, seq, d = x.shape
    half = jnp.arange(0, HEAD_DIM, 2, dtype=F32)
    inv_freq = ROPE_THETA ** (-half / HEAD_DIM)
    invf = jnp.tile(inv_freq, LANES // (HEAD_DIM // 2)).reshape(1, LANES)
    stacks = (norm_ffn1_g, ffn1_w_gate, ffn1_w_up, ffn1_w_down, norm_mix_g, b_forget, fox_q_norm_g,
              fox_k_norm_g, swa_q_norm_g, swa_k_norm_g, swa_sinks, out_norm_fox_g, out_norm_swa_g, w_out,
              norm_ffn2_g, ffn2_w_gate, ffn2_w_up, ffn2_w_down)
    h = x.reshape(batch * seq, d)
    for layer in range(norm_ffn1_g.shape[0]):
        h = _layer(h, positions, invf, tuple(s[layer] for s in stacks), w_in, layer)
    return h.reshape(batch, seq, d)
```

```python
import functools

import jax
import jax.numpy as jnp
import numpy as np
from jax import lax
from jax.experimental import pallas as pl
from jax.experimental.pallas import tpu as pltpu

F32 = jnp.float32
BF16 = jnp.bfloat16

HEAD_DIM = 64
FOX_HEADS = 16
SWA_Q_HEADS = 16
SWA_KV_HEADS = 4
SWA_GROUP = SWA_Q_HEADS // SWA_KV_HEADS
WINDOW = 128
ROPE_THETA = 10000.0
EPS = 1e-6

D_FOX = FOX_HEADS * HEAD_DIM
D_SWA_Q = SWA_Q_HEADS * HEAD_DIM
D_SWA_KV = SWA_KV_HEADS * HEAD_DIM
N_GATE = FOX_HEADS

LANES = 128
MXU_TILE = 256
VMEM_LIMIT = 60 * 1024 * 1024

C_FQ = 0
C_FK = C_FQ + D_FOX
C_FV = C_FK + D_FOX
C_SQ = C_FV + D_FOX
C_SK = C_SQ + D_SWA_Q
C_SV = C_SK + D_SWA_KV
C_GATE = C_SV + D_SWA_KV
D_IN_PAD = C_GATE + LANES

SWA_HEAD_ORDER = tuple(8 * p + 4 * side + j for p in range(2) for j in range(SWA_GROUP) for side in range(2))

FFN_TM = 1024
FFN_TF = 512
HEAD_TF = 256
HEAD_CHUNKS = 8
WPREP_COLS = 512
INPROJ_TM = 512
INPROJ_TN = 512
INPROJ_DEPTH = 1
OUTPROJ_TM = 512
OUTPROJ_ROWS = 256
FOX_TQ = 256
FOX_TILES = 2
FOX_LAG_S, FOX_LAG_V = 1, 2
CUM_BLK = 256

LOG2E = 1.4426950408889634

AUG_PARTS = 3
ONE_LANE = AUG_PARTS * N_GATE


def _rms_rows(x, g):
    return x * lax.rsqrt(jnp.mean(x * x, axis=-1, keepdims=True) + EPS) * g


def _swiglu_tile(u, wg, wu, wd):
    gate = jnp.dot(u, wg, preferred_element_type=F32)
    up = jnp.dot(u, wu, preferred_element_type=F32)
    a = (gate * jax.nn.sigmoid(gate) * up * 0.5).astype(BF16)
    return jnp.dot(a, wd, preferred_element_type=F32)


def _ffn_first(x_ref, g_ref, u_ref, wg, wu, wd, o_ref):
    x = x_ref[...]
    u = _rms_rows(x, g_ref[...]).astype(BF16)
    u_ref[...] = u
    o_ref[...] = x + _swiglu_tile(u, wg, wu, wd)


def _ffn_step(u_ref, wg, wu, wd, o_ref):
    o_ref[...] += _swiglu_tile(u_ref[...], wg, wu, wd)


def _ffn_kernel(x_ref, g_ref, wg_ref, wu_ref, wd_ref, *refs, n_cast, has_head):
    head_ref = refs[0] if has_head else None
    refs = refs[1:] if has_head else refs
    cast_in, o_ref, cast_out, u_ref = refs[:n_cast], refs[n_cast], refs[n_cast + 1:-1], refs[-1]
    i, j = pl.program_id(0), pl.program_id(1)

    def side_job():
        for src, dst in zip(cast_in, cast_out):
            dst[...] = src[...].astype(BF16)

    if has_head:
        @pl.when(i == 0)
        def _():
            @pl.when(j < HEAD_CHUNKS)
            def _():
                rows = FFN_TM // HEAD_CHUNKS
                o_ref[pl.ds(pl.multiple_of(j * rows, rows), rows), :] = head_ref[...]

            side_job()

        computing = i > 0
    else:
        computing = True

    @pl.when(computing & (j == 0))
    def _():
        _ffn_first(x_ref, g_ref, u_ref, wg_ref[...], wu_ref[...], wd_ref[...], o_ref)
        side_job()

    @pl.when(computing & (j > 0))
    def _():
        _ffn_step(u_ref, wg_ref[...], wu_ref[...], wd_ref[...], o_ref)
        side_job()


def _ffn(x, g, wg, wu, wd, cast=(), head=None):
    t, d = x.shape
    f = wg.shape[1]
    ni, nj = t // FFN_TM, f // FFN_TF
    has_head = head is not None
    live = (lambda i: jnp.minimum(i, 1)) if has_head else (lambda i: 1)
    row = (lambda i: jnp.maximum(i, 1)) if has_head else (lambda i: i)
    cast_specs = []
    for w in cast:
        r, c = w.shape
        if r % ni == 0 and c % nj == 0 and (c // nj) % LANES == 0:
            cast_specs.append(pl.BlockSpec((r // ni, c // nj), lambda i, j: (i, j)))
        elif r % nj == 0 and c % ni == 0 and (c // ni) % LANES == 0:
            cast_specs.append(pl.BlockSpec((r // nj, c // ni), lambda i, j: (j, i)))
        else:
            assert r % ni == 0, w.shape
            cast_specs.append(pl.BlockSpec((r // ni, c), lambda i, j: (i, 0)))
    head_specs, head_args = [], []
    if has_head:
        last = HEAD_CHUNKS - 1
        head_specs = [pl.BlockSpec((FFN_TM // HEAD_CHUNKS, d),
                                   lambda i, j: (jnp.where(i == 0, jnp.minimum(j, last), last), 0))]
        head_args = [head]
    outs = pl.pallas_call(
        functools.partial(_ffn_kernel, n_cast=len(cast), has_head=has_head),
        grid=(ni, nj),
        in_specs=[
            pl.BlockSpec((FFN_TM, d), lambda i, j: (row(i), 0)),
            pl.BlockSpec((1, d), lambda i, j: (0, 0)),
            pl.BlockSpec((d, FFN_TF), lambda i, j: (0, j * live(i))),
            pl.BlockSpec((d, FFN_TF), lambda i, j: (0, j * live(i))),
            pl.BlockSpec((FFN_TF, d), lambda i, j: (j * live(i), 0)),
        ] + head_specs + cast_specs,
        out_specs=[pl.BlockSpec((FFN_TM, d), lambda i, j: (i, 0))] + cast_specs,
        out_shape=[jax.ShapeDtypeStruct((t, d), F32)] + [jax.ShapeDtypeStruct(w.shape, BF16) for w in cast],
        scratch_shapes=[pltpu.VMEM((FFN_TM, d), BF16)],
        compiler_params=pltpu.CompilerParams(
            dimension_semantics=("parallel", "arbitrary"), vmem_limit_bytes=VMEM_LIMIT),
        name="ffn_cast" if cast else "ffn",
    )(x, g, wg, wu, wd, *head_args, *cast)
    return outs[0], tuple(outs[1:])


def _ffn_head_kernel(x_ref, g_ref, wg_ref, wu_ref, wd_ref, o_ref, wgb_ref, wub_ref, wdb_ref, u_ref):
    wgb_ref[...] = wg_ref[...].astype(BF16)
    wub_ref[...] = wu_ref[...].astype(BF16)
    wdb_ref[...] = wd_ref[...].astype(BF16)

    @pl.when(pl.program_id(0) == 0)
    def _():
        _ffn_first(x_ref, g_ref, u_ref, wgb_ref[...], wub_ref[...], wdb_ref[...], o_ref)

    @pl.when(pl.program_id(0) > 0)
    def _():
        _ffn_step(u_ref, wgb_ref[...], wub_ref[...], wdb_ref[...], o_ref)


def _ffn_head(x, g, wg, wu, wd):
    t, d = x.shape
    f = wg.shape[1]
    once = lambda shape: pl.BlockSpec(shape, lambda j: (0, 0), pipeline_mode=pl.Buffered(1))
    col = pl.BlockSpec((d, HEAD_TF), lambda j: (0, j))
    rowb = pl.BlockSpec((HEAD_TF, d), lambda j: (j, 0))
    return pl.pallas_call(
        _ffn_head_kernel,
        grid=(f // HEAD_TF,),
        in_specs=[once((FFN_TM, d)), once((1, d)), col, col, rowb],
        out_specs=[pl.BlockSpec((FFN_TM, d), lambda j: (0, 0)), col, col, rowb],
        out_shape=[jax.ShapeDtypeStruct((FFN_TM, d), F32), jax.ShapeDtypeStruct(wg.shape, BF16),
                   jax.ShapeDtypeStruct(wu.shape, BF16), jax.ShapeDtypeStruct(wd.shape, BF16)],
        scratch_shapes=[pltpu.VMEM((FFN_TM, d), BF16)],
        compiler_params=pltpu.CompilerParams(dimension_semantics=("arbitrary",), vmem_limit_bytes=VMEM_LIMIT),
        name="ffn_head",
    )(x, g, wg, wu, wd)


def _head_rms(y, gain, bd):
    n = y.shape[1]
    y2 = (y * y).astype(BF16)
    parts = [jnp.dot(y2[:, c * MXU_TILE:(c + 1) * MXU_TILE], bd, preferred_element_type=F32)
             for c in range(n // MXU_TILE)]
    ss = parts[0] if len(parts) == 1 else jnp.concatenate(parts, axis=1)
    return y * lax.rsqrt(ss * (1.0 / HEAD_DIM) + EPS) * gain


def _rope(y, cos, sin_signed, first_half):
    outs = []
    for c in range(y.shape[1] // LANES):
        yc = y[:, c * LANES:(c + 1) * LANES]
        partner = jnp.where(first_half, pltpu.roll(yc, LANES - HEAD_DIM // 2, 1),
                            pltpu.roll(yc, HEAD_DIM // 2, 1))
        outs.append(yc * cos + partner * sin_signed)
    return outs[0] if len(outs) == 1 else jnp.concatenate(outs, axis=1)


def _inproj_kernel(h_ref, g_ref, w_ref, pos_ref, invf_ref, gfq_ref, gfk_ref, gsq_ref, gsk_ref, bf_ref, cast_ref,
                   qf_ref, kf_ref, vf_ref, qs_ref, ks_ref, vs_ref, lf_ref, cast_out_ref):
    u = _rms_rows(h_ref[...], g_ref[...]).astype(BF16)
    cast_out_ref[...] = cast_ref[...].astype(BF16)

    lane = lax.broadcasted_iota(jnp.int32, (1, LANES), 1)
    first_half = (lane % HEAD_DIM) < (HEAD_DIM // 2)
    rope_tables = []

    def rope_table():
        if not rope_tables:
            ang = pos_ref[...].astype(F32) * invf_ref[...]
            sin = jnp.sin(ang)
            rope_tables.append((jnp.cos(ang), jnp.where(first_half, -sin, sin)))
        return rope_tables[0]

    r = lax.broadcasted_iota(jnp.int32, (MXU_TILE, MXU_TILE), 0) // HEAD_DIM
    c = lax.broadcasted_iota(jnp.int32, (MXU_TILE, MXU_TILE), 1) // HEAD_DIM
    bd = jnp.where(r == c, 1.0, 0.0).astype(BF16)

    scale = HEAD_DIM ** -0.5

    def fox_q(y, lo, hi):
        qf_ref[:, lo:hi] = (_head_rms(y, gfq_ref[:, lo:hi], bd) * (scale * LOG2E)).astype(BF16)

    def fox_k(y, lo, hi):
        kf_ref[:, lo:hi] = _head_rms(y, gfk_ref[:, lo:hi], bd).astype(BF16)

    def fox_v(y, lo, hi):
        vf_ref[:, lo:hi] = y.astype(BF16)
        rope_table()

    def swa_q(y, lo, hi):
        y = _rope(_head_rms(y, gsq_ref[:, lo:hi], bd), *rope_table(), first_half)
        qs_ref[:, lo:hi] = (y * (scale * LOG2E)).astype(BF16)

    def swa_k(y, lo, hi):
        ks_ref[:, lo:hi] = _rope(_head_rms(y, gsk_ref[:, lo:hi], bd), *rope_table(), first_half).astype(BF16)

    def swa_v(y, lo, hi):
        vs_ref[:, lo:hi] = y.astype(BF16)

    def gate(y, lo, hi):
        z = y + bf_ref[...]
        lf_ref[...] = jnp.minimum(z, 0.0) - jnp.log1p(jnp.exp(-jnp.abs(z)))

    stages = []
    for start, end, epilogue in ((C_FQ, C_FK, fox_q), (C_FK, C_FV, fox_k), (C_FV, C_SQ, fox_v),
                                 (C_SQ, C_SK, swa_q), (C_SK, C_SV, swa_k), (C_SV, C_GATE, swa_v),
                                 (C_GATE, D_IN_PAD, gate)):
        for lo in range(start, end, INPROJ_TN):
            hi = min(lo + INPROJ_TN, end)
            stages.append((lo, hi, lo - start, hi - start, epilogue))
    pending = []
    for lo, hi, olo, ohi, epilogue in stages:
        y = jnp.dot(u, w_ref[:, lo:hi], preferred_element_type=F32)
        if len(pending) >= INPROJ_DEPTH:
            fn, *args = pending.pop(0)
            fn(*args)
        pending.append((epilogue, y, olo, ohi))
    for fn, *args in pending:
        fn(*args)


def _inproj(h, g, w, pos, invf, gfq, gfk, gsq, gsk, bf, cast):
    t, d = h.shape
    steps = t // INPROJ_TM
    row = lambda n: pl.BlockSpec((INPROJ_TM, n), lambda i: (i, 0))
    const = lambda shape: pl.BlockSpec(shape, lambda i: (0, 0))
    cast_spec = pl.BlockSpec((cast.shape[0] // steps, cast.shape[1]), lambda i: (i, 0))
    return pl.pallas_call(
        _inproj_kernel,
        grid=(steps,),
        in_specs=[
            row(d),
            const((1, d)),
            pl.BlockSpec((d, D_IN_PAD), lambda i: (0, 0), pipeline_mode=pl.Buffered(1)),
            row(1),
            const((1, LANES)),
            const((1, D_FOX)), const((1, D_FOX)), const((1, D_SWA_Q)), const((1, D_SWA_KV)),
            const((1, LANES)),
            cast_spec,
        ],
        out_specs=[row(D_FOX), row(D_FOX), row(D_FOX), row(D_SWA_Q), row(D_SWA_KV), row(D_SWA_KV), row(LANES),
                   cast_spec],
        out_shape=[
            jax.ShapeDtypeStruct((t, D_FOX), BF16), jax.ShapeDtypeStruct((t, D_FOX), BF16),
            jax.ShapeDtypeStruct((t, D_FOX), BF16), jax.ShapeDtypeStruct((t, D_SWA_Q), BF16),
            jax.ShapeDtypeStruct((t, D_SWA_KV), BF16), jax.ShapeDtypeStruct((t, D_SWA_KV), BF16),
            jax.ShapeDtypeStruct((t, LANES), F32), jax.ShapeDtypeStruct(cast.shape, BF16),
        ],
        compiler_params=pltpu.CompilerParams(dimension_semantics=("parallel",), vmem_limit_bytes=VMEM_LIMIT),
        name="inproj",
    )(h, g, w, pos, invf, gfq, gfk, gsq, gsk, bf, cast)


def _split3(x):
    p1 = x.astype(BF16)
    r1 = x - p1.astype(F32)
    p2 = r1.astype(BF16)
    p3 = (r1 - p2.astype(F32)).astype(BF16)
    return p1, p2, p3


def _cumsum_kernel(lf_ref, pmap_ref, ak_ref, aq_ref):
    s = lf_ref.shape[0]
    r = lax.broadcasted_iota(jnp.int32, (CUM_BLK, CUM_BLK), 0)
    c = lax.broadcasted_iota(jnp.int32, (CUM_BLK, CUM_BLK), 1)
    tri = jnp.where(r >= c, 1.0, 0.0).astype(BF16)
    lane = lax.broadcasted_iota(jnp.int32, (1, LANES), 1)
    is_gate = lane < N_GATE
    one = jnp.where(lane == ONE_LANE, 1.0, 0.0)
    local = []
    for b in range(s // CUM_BLK):
        p1, p2, p3 = _split3(lf_ref[b * CUM_BLK:(b + 1) * CUM_BLK, :])
        local.append(jnp.dot(tri, p1, preferred_element_type=F32)
                     + jnp.dot(tri, p2, preferred_element_type=F32)
                     + jnp.dot(tri, p3, preferred_element_type=F32))
    carry = jnp.zeros((1, LANES), F32)
    for b in range(s // CUM_BLK):
        rows = slice(b * CUM_BLK, (b + 1) * CUM_BLK)
        cb = local[b] + carry
        carry = cb[CUM_BLK - 1:CUM_BLK, :]
        terms = [jnp.where(is_gate, t.astype(F32), 0.0) for t in _split3(cb * LOG2E)]
        packed = one + terms[0]
        for i in range(1, AUG_PARTS):
            packed = packed + pltpu.roll(terms[i], i * N_GATE, 1)
        placed = jnp.dot(packed.astype(BF16), pmap_ref[...], preferred_element_type=F32)
        ak_ref[rows, :] = placed[:, :D_FOX].astype(BF16)
        aq_ref[rows, :] = placed[:, D_FOX:].astype(BF16)


def _aug_map():
    pmap = np.zeros((LANES, 2 * D_FOX), np.float32)
    for h in range(FOX_HEADS):
        base = (h // 2) * LANES + (HEAD_DIM if h % 2 == 0 else 0)
        for i in range(AUG_PARTS):
            pmap[i * N_GATE + h, base + i] = -1.0
            pmap[ONE_LANE, D_FOX + base + i] = 1.0
        pmap[ONE_LANE, base + AUG_PARTS] = 1.0
        pmap[h, D_FOX + base + AUG_PARTS] = 1.0
    return jnp.asarray(pmap, BF16)


def _cumsum(lf, batch, seq):
    pmap = _aug_map()
    out = pl.BlockSpec((seq, D_FOX), lambda b: (b, 0))
    return pl.pallas_call(
        _cumsum_kernel,
        grid=(batch,),
        in_specs=[pl.BlockSpec((seq, LANES), lambda b: (b, 0)), pl.BlockSpec(pmap.shape, lambda b: (0, 0))],
        out_specs=[out, out],
        out_shape=[jax.ShapeDtypeStruct((batch * seq, D_FOX), BF16)] * 2,
        compiler_params=pltpu.CompilerParams(dimension_semantics=("parallel",)),
        name="gate_cumsum",
    )(lf, pmap)


def _fox_kernel(q_ref, k_ref, v_ref, aq_ref, ak_ref, o_ref, qa_ref, ka_ref):
    seq = q_ref.shape[0]
    lane = lax.broadcasted_iota(jnp.int32, (1, LANES), 1)
    lo = lane < HEAD_DIM
    for tile in range(FOX_TILES):
        cols = slice(tile * LANES, (tile + 1) * LANES)
        q, k, aq, ak = q_ref[:, cols], k_ref[:, cols], aq_ref[:, cols], ak_ref[:, cols]
        qa_ref[2 * tile] = jnp.where(lo, q, aq)
        qa_ref[2 * tile + 1] = jnp.where(lo, aq, q)
        ka_ref[2 * tile] = jnp.where(lo, k, ak)
        ka_ref[2 * tile + 1] = jnp.where(lo, ak, k)

    qry = lax.broadcasted_iota(jnp.int32, (FOX_TQ, FOX_TQ), 0)
    key = lax.broadcasted_iota(jnp.int32, (FOX_TQ, FOX_TQ), 1)
    causal = key <= qry
    nt = (((1,), (1,)), ((), ()))

    def scores(i, head):
        rows = slice(i * FOX_TQ, (i + 1) * FOX_TQ)
        return lax.dot_general(qa_ref[head, rows, :], ka_ref[head, 0:(i + 1) * FOX_TQ, :], nt,
                               preferred_element_type=F32)

    def softmax(i, s):
        w = (i + 1) * FOX_TQ
        diag = jnp.where(causal, s[:, w - FOX_TQ:], -jnp.inf)
        s = diag if i == 0 else jnp.concatenate([s[:, :w - FOX_TQ], diag], axis=1)
        m = jnp.max(s, axis=1, keepdims=True)
        p = jnp.exp2(s - m)
        return p.astype(BF16), jnp.sum(p, axis=1, keepdims=True)

    def values(i, head, p, l):
        tile = head // 2
        o = jnp.dot(p, v_ref[0:(i + 1) * FOX_TQ, tile * LANES:(tile + 1) * LANES],
                    preferred_element_type=F32)
        return o / l

    chains = [(i, head) for i in reversed(range(seq // FOX_TQ)) for head in range(2 * FOX_TILES)]
    s_buf, p_buf, done = {}, {}, {}
    for t in range(len(chains) + FOX_LAG_V):
        if t < len(chains):
            s_buf[t] = scores(*chains[t])
        if FOX_LAG_S <= t < len(chains) + FOX_LAG_S:
            p_buf[t - FOX_LAG_S] = softmax(chains[t - FOX_LAG_S][0], s_buf.pop(t - FOX_LAG_S))
        if t >= FOX_LAG_V:
            i, head = chains[t - FOX_LAG_V]
            done[(i, head)] = values(i, head, *p_buf.pop(t - FOX_LAG_V))
            if head % 2 == 1:
                rows = slice(i * FOX_TQ, (i + 1) * FOX_TQ)
                cols = slice((head // 2) * LANES, (head // 2 + 1) * LANES)
                o_ref[rows, cols] = jnp.where(lo, done.pop((i, head - 1)), done.pop((i, head))).astype(BF16)


def _fox(q, k, v, aq, ak, batch, seq):
    width = FOX_TILES * LANES
    blk = pl.BlockSpec((seq, width), lambda b, p: (b, p))
    return pl.pallas_call(
        _fox_kernel,
        grid=(batch, D_FOX // width),
        in_specs=[blk] * 5,
        out_specs=blk,
        out_shape=jax.ShapeDtypeStruct((batch * seq, D_FOX), BF16),
        scratch_shapes=[pltpu.VMEM((2 * FOX_TILES, seq, LANES), BF16),
                        pltpu.VMEM((2 * FOX_TILES, seq, LANES), BF16)],
        compiler_params=pltpu.CompilerParams(
            dimension_semantics=("parallel", "arbitrary"), vmem_limit_bytes=VMEM_LIMIT),
        name="fox_attn",
    )(q, k, v, aq, ak)


SWA_ROWS = SWA_GROUP * WINDOW
SWA_UNROLL = 15
SWA_CHAINS = tuple((p, side) for p in range(D_SWA_KV // LANES) for side in range(2))


def _swa_kernel(sink_ref, q_ref, k_ref, v_ref, o_ref, bias_ref, sink_col_ref):
    seq = q_ref.shape[0]
    nblk = seq // WINDOW
    lane = lax.broadcasted_iota(jnp.int32, (1, LANES), 1)
    lo = lane < HEAD_DIM
    nt = (((1,), (1,)), ((), ()))

    r = lax.broadcasted_iota(jnp.int32, (SWA_ROWS, 2 * WINDOW), 0) % WINDOW
    c = lax.broadcasted_iota(jnp.int32, (SWA_ROWS, 2 * WINDOW), 1)
    bias_ref[...] = jnp.where((c > r) & (c <= r + WINDOW), 0.0, -jnp.inf)
    head = lax.broadcasted_iota(jnp.int32, (SWA_ROWS, LANES), 0) // WINDOW
    for n, (p, side) in enumerate(SWA_CHAINS):
        col = jnp.zeros((SWA_ROWS, LANES), F32)
        for j in range(SWA_GROUP):
            col = jnp.where(head == j, sink_ref[8 * p + 4 * side + j] * LOG2E, col)
        sink_col_ref[n] = col

    def block(q_rows, k_rows, first):
        cols = slice(WINDOW, 2 * WINDOW) if first else slice(0, 2 * WINDOW)

        def scores(n):
            p, side = SWA_CHAINS[n]
            tiles = [q_ref[q_rows, (p * SWA_GROUP + j) * LANES:(p * SWA_GROUP + j + 1) * LANES]
                     for j in range(SWA_GROUP)]
            qst = jnp.concatenate(tiles, axis=0)
            qm = jnp.where(lo == (side == 0), qst, jnp.zeros_like(qst))
            kw = k_ref[k_rows, p * LANES:(p + 1) * LANES]
            return lax.dot_general(qm, kw, nt, preferred_element_type=F32)

        def softmax(n, s):
            s = s + bias_ref[:, cols]
            sk = sink_col_ref[n]
            m = jnp.maximum(jnp.max(s, axis=1, keepdims=True), sk)
            e = jnp.exp2(s - jnp.concatenate([m] * (s.shape[1] // LANES), axis=1))
            l = jnp.sum(e, axis=1, keepdims=True) + jnp.exp2(sk - m)
            return e.astype(BF16), l

        def values(n, e, l):
            p, _ = SWA_CHAINS[n]
            vw = v_ref[k_rows, p * LANES:(p + 1) * LANES]
            return jnp.dot(e, vw, preferred_element_type=F32) / l

        s_buf, e_buf, done = {}, {}, {}
        for t in range(len(SWA_CHAINS) + 2):
            if t < len(SWA_CHAINS):
                s_buf[t] = scores(t)
            if 1 <= t <= len(SWA_CHAINS):
                e_buf[t - 1] = softmax(t - 1, s_buf.pop(t - 1))
            if t >= 2:
                p, side = SWA_CHAINS[t - 2]
                done[side] = values(t - 2, *e_buf.pop(t - 2))
                if side == 1:
                    o_both = jnp.where(lo, done.pop(0), done.pop(1)).astype(BF16)
                    for j in range(SWA_GROUP):
                        tile = p * SWA_GROUP + j
                        o_ref[q_rows, tile * LANES:(tile + 1) * LANES] = o_both[j * WINDOW:(j + 1) * WINDOW, :]

    block(slice(0, WINDOW), slice(0, WINDOW), True)

    def body(n, carry):
        q0 = pl.multiple_of(n * WINDOW, WINDOW)
        k0 = pl.multiple_of((n - 1) * WINDOW, WINDOW)
        block(pl.ds(q0, WINDOW), pl.ds(k0, 2 * WINDOW), False)
        return carry

    lax.fori_loop(1, nblk, body, 0, unroll=SWA_UNROLL)


def _swa(sinks, q, k, v, batch, seq):
    return pl.pallas_call(
        _swa_kernel,
        grid=(batch,),
        in_specs=[pl.BlockSpec(memory_space=pltpu.SMEM),
                  pl.BlockSpec((seq, D_SWA_Q), lambda b: (b, 0)),
                  pl.BlockSpec((seq, D_SWA_KV), lambda b: (b, 0)),
                  pl.BlockSpec((seq, D_SWA_KV), lambda b: (b, 0))],
        out_specs=pl.BlockSpec((seq, D_SWA_Q), lambda b: (b, 0)),
        out_shape=jax.ShapeDtypeStruct((batch * seq, D_SWA_Q), BF16),
        scratch_shapes=[pltpu.VMEM((SWA_ROWS, 2 * WINDOW), F32),
                        pltpu.VMEM((len(SWA_CHAINS), SWA_ROWS, LANES), F32)],
        compiler_params=pltpu.CompilerParams(dimension_semantics=("parallel",), vmem_limit_bytes=VMEM_LIMIT),
        name="swa_attn",
    )(sinks, q, k, v)


def _outproj_kernel(of_ref, os_ref, h_ref, gf_ref, gs_ref, wf_ref, ws_ref, o_ref, wsp_ref):
    @pl.when(pl.program_id(0) == 0)
    def _():
        for k, h in enumerate(SWA_HEAD_ORDER):
            wsp_ref[k * HEAD_DIM:(k + 1) * HEAD_DIM, :] = ws_ref[h * HEAD_DIM:(h + 1) * HEAD_DIM, :]

    rows = [slice(c * OUTPROJ_ROWS, (c + 1) * OUTPROJ_ROWS) for c in range(OUTPROJ_TM // OUTPROJ_ROWS)]

    def norms(r):
        return (_rms_rows(of_ref[r, :].astype(F32), gf_ref[...]).astype(BF16),
                _rms_rows(os_ref[r, :].astype(F32), gs_ref[...]).astype(BF16))

    ready = norms(rows[0])
    for c, r in enumerate(rows):
        nf, ns = ready
        if c + 1 < len(rows):
            ready = norms(rows[c + 1])
        acc = jnp.dot(nf, wf_ref[...], preferred_element_type=F32)
        acc = acc + jnp.dot(ns, wsp_ref[...], preferred_element_type=F32)
        o_ref[r, :] = h_ref[r, :] + acc


def _outproj(of, os_, h, gf, gs, w):
    t, d = h.shape
    row = lambda n: pl.BlockSpec((OUTPROJ_TM, n), lambda i: (i, 0))
    const = lambda shape: pl.BlockSpec(shape, lambda i: (0, 0))
    half = lambda n: pl.BlockSpec((D_FOX, d), lambda i: (n, 0), pipeline_mode=pl.Buffered(1))
    return pl.pallas_call(
        _outproj_kernel,
        grid=(t // OUTPROJ_TM,),
        in_specs=[row(D_FOX), row(D_SWA_Q), row(d), const((1, D_FOX)), const((1, D_SWA_Q)), half(0), half(1)],
        out_specs=row(d),
        out_shape=jax.ShapeDtypeStruct((t, d), F32),
        scratch_shapes=[pltpu.VMEM((D_SWA_Q, d), BF16)],
        compiler_params=pltpu.CompilerParams(dimension_semantics=("arbitrary",), vmem_limit_bytes=VMEM_LIMIT),
        name="outproj",
    )(of, os_, h, gf, gs, w, w)


def _swa_reorder(a, axis):
    shape = a.shape
    a = a.reshape(shape[:axis] + (2, 2, SWA_GROUP, HEAD_DIM) + shape[axis + 1:])
    a = jnp.swapaxes(a, axis + 1, axis + 2)
    return a.reshape(shape)


def _w_in_head_rows():
    gate0 = 3 * D_FOX
    sq0 = gate0 + N_GATE
    starts = [c for c in range(0, gate0, HEAD_DIM)]
    starts += [sq0 + h * HEAD_DIM for h in SWA_HEAD_ORDER]
    starts += [sq0 + D_SWA_Q + c for c in range(0, 2 * D_SWA_KV, HEAD_DIM)]
    starts += [gate0, gate0]
    assert len(starts) * HEAD_DIM == D_IN_PAD
    return np.asarray(starts, np.int32)


def _w_in_prep_kernel(wt_ref, o_ref):
    starts = [int(s) for s in _w_in_head_rows()]
    row = lax.broadcasted_iota(jnp.int32, (LANES, 1), 0)
    for t in range(D_IN_PAD // LANES):
        pieces = [wt_ref[s:s + HEAD_DIM, :] for s in starts[2 * t:2 * t + 2]]
        x = jnp.concatenate(pieces, axis=0)
        if t == D_IN_PAD // LANES - 1:
            x = jnp.where(row >= N_GATE, 0.0, x)
        o_ref[:, t * LANES:(t + 1) * LANES] = jnp.transpose(x).astype(BF16)


def _w_in_prep(w_in_stack, layer):
    _, d, n = w_in_stack.shape
    wt = jnp.swapaxes(w_in_stack, 1, 2)
    return pl.pallas_call(
        _w_in_prep_kernel,
        grid=(d // WPREP_COLS,),
        in_specs=[pl.BlockSpec((None, n, WPREP_COLS), lambda i: (layer, 0, i))],
        out_specs=pl.BlockSpec((WPREP_COLS, D_IN_PAD), lambda i: (i, 0)),
        out_shape=jax.ShapeDtypeStruct((d, D_IN_PAD), BF16),
        compiler_params=pltpu.CompilerParams(dimension_semantics=("parallel",), vmem_limit_bytes=VMEM_LIMIT),
        name="w_in_prep",
    )(wt)


def _layer(h, pos, invf, p, w_in_stack, layer):
    (norm_ffn1_g, w_gate1, w_up1, w_down1, norm_mix_g, b_forget, fox_q_g, fox_k_g, swa_q_g, swa_k_g,
     sinks, out_fox_g, out_swa_g, w_out, norm_ffn2_g, w_gate2, w_up2, w_down2) = p
    batch_seq, d = h.shape
    batch, seq = pos.shape
    row = lambda v: v.reshape(1, -1).astype(F32)
    tile = lambda v, n: jnp.tile(v.astype(F32), n).reshape(1, -1)

    w_in_r = _w_in_prep(w_in_stack, layer)
    bf_pad = jnp.concatenate([b_forget.astype(F32), jnp.zeros((LANES - N_GATE,), F32)]).reshape(1, LANES)

    h1_head, w_gate1_b, w_up1_b, w_down1_b = _ffn_head(h, row(norm_ffn1_g), w_gate1, w_up1, w_down1)
    h1, (w_gate2_b, w_up2_b, w_down2_b) = _ffn(
        h, row(norm_ffn1_g), w_gate1_b, w_up1_b, w_down1_b, cast=(w_gate2, w_up2, w_down2), head=h1_head)

    qf, kf, vf, qs, ks, vs, lf, w_out_b = _inproj(
        h1, row(norm_mix_g), w_in_r, pos.reshape(batch_seq, 1), invf,
        tile(fox_q_g, FOX_HEADS), tile(fox_k_g, FOX_HEADS), tile(swa_q_g, SWA_Q_HEADS),
        tile(swa_k_g, SWA_KV_HEADS), bf_pad, cast=w_out)

    ak, aq = _cumsum(lf, batch, seq)
    o_fox = _fox(qf, kf, vf, aq, ak, batch, seq)
    o_swa = _swa(sinks.astype(F32), qs, ks, vs, batch, seq)

    h2 = _outproj(o_fox, o_swa, h1, row(out_fox_g), row(_swa_reorder(out_swa_g, 0)), w_out_b)
    return _ffn(h2, row(norm_ffn2_g), w_gate2_b, w_up2_b, w_down2_b)[0]


def kernel(x, positions, norm_ffn1_g, ffn1_w_gate, ffn1_w_up, ffn1_w_down, norm_mix_g, w_in, b_forget, fox_q_norm_g, fox_k_norm_g, swa_q_norm_g, swa_k_norm_g, swa_sinks, out_norm_fox_g, out_norm_swa_g, w_out, norm_ffn2_g, ffn2_w_gate, ffn2_w_up, ffn2_w_down):
    batch, seq, d = x.shape
    half = jnp.arange(0, HEAD_DIM, 2, dtype=F32)
    inv_freq = ROPE_THETA ** (-half / HEAD_DIM)
    invf = jnp.tile(inv_freq, LANES // (HEAD_DIM // 2)).reshape(1, LANES)
    stacks = (norm_ffn1_g, ffn1_w_gate, ffn1_w_up, ffn1_w_down, norm_mix_g, b_forget, fox_q_norm_g,
              fox_k_norm_g, swa_q_norm_g, swa_k_norm_g, swa_sinks, out_norm_fox_g, out_norm_swa_g, w_out,
              norm_ffn2_g, ffn2_w_gate, ffn2_w_up, ffn2_w_down)
    h = x.reshape(batch * seq, d)
    for layer in range(norm_ffn1_g.shape[0]):
        h = _layer(h, positions, invf, tuple(s[layer] for s in stacks), w_in, layer)
    return h.reshape(batch, seq, d)
```

```python
import functools

import jax
import jax.numpy as jnp
import numpy as np
from jax import lax
from jax.experimental import pallas as pl
from jax.experimental.pallas import tpu as pltpu

F32 = jnp.float32
BF16 = jnp.bfloat16

HEAD_DIM = 64
FOX_HEADS = 16
SWA_Q_HEADS = 16
SWA_KV_HEADS = 4
SWA_GROUP = SWA_Q_HEADS // SWA_KV_HEADS
WINDOW = 128
ROPE_THETA = 10000.0
EPS = 1e-6

D_FOX = FOX_HEADS * HEAD_DIM
D_SWA_Q = SWA_Q_HEADS * HEAD_DIM
D_SWA_KV = SWA_KV_HEADS * HEAD_DIM
N_GATE = FOX_HEADS

LANES = 128
MXU_TILE = 256
VMEM_LIMIT = 60 * 1024 * 1024

C_FQ = 0
C_FK = C_FQ + D_FOX
C_FV = C_FK + D_FOX
C_SQ = C_FV + D_FOX
C_SK = C_SQ + D_SWA_Q
C_SV = C_SK + D_SWA_KV
C_GATE = C_SV + D_SWA_KV
D_IN_PAD = C_GATE + LANES

SWA_HEAD_ORDER = tuple(8 * p + 4 * side + j for p in range(2) for j in range(SWA_GROUP) for side in range(2))

FFN_TM = 1024
FFN_TF = 512
HEAD_TF = 256
HEAD_CHUNKS = 8
WPREP_COLS = 512
INPROJ_TM = 512
INPROJ_TN = 512
INPROJ_DEPTH = 1
OUTPROJ_TM = 512
OUTPROJ_ROWS = 256
FOX_TQ = 256
FOX_TILES = 2
FOX_LAG_S, FOX_LAG_V = 1, 2
CUM_BLK = 256

LOG2E = 1.4426950408889634

AUG_PARTS = 3
ONE_LANE = AUG_PARTS * N_GATE


def _rms_rows(x, g):
    return x * lax.rsqrt(jnp.mean(x * x, axis=-1, keepdims=True) + EPS) * g


def _swiglu_tile(u, wg, wu, wd):
    gate = jnp.dot(u, wg, preferred_element_type=F32)
    up = jnp.dot(u, wu, preferred_element_type=F32)
    a = (gate * jax.nn.sigmoid(gate) * up * 0.5).astype(BF16)
    return jnp.dot(a, wd, preferred_element_type=F32)


def _ffn_first(x_ref, g_ref, u_ref, wg, wu, wd, o_ref):
    x = x_ref[...]
    u = _rms_rows(x, g_ref[...]).astype(BF16)
    u_ref[...] = u
    o_ref[...] = x + _swiglu_tile(u, wg, wu, wd)


def _ffn_step(u_ref, wg, wu, wd, o_ref):
    o_ref[...] += _swiglu_tile(u_ref[...], wg, wu, wd)


def _col_tiles(ref):
    return jnp.concatenate([ref[k] for k in range(ref.shape[0])], axis=1)


def _ffn_kernel(x_ref, g_ref, wg_ref, wu_ref, wd_ref, *refs, n_cast, has_head):
    head_ref = refs[0] if has_head else None
    refs = refs[1:] if has_head else refs
    cast_in, o_ref, cast_out, u_ref = refs[:n_cast], refs[n_cast], refs[n_cast + 1:-1], refs[-1]
    i, j = pl.program_id(0), pl.program_id(1)

    def side_job():
        for src, dst in zip(cast_in, cast_out):
            if len(dst.shape) == 3:
                for k in range(dst.shape[0]):
                    dst[k] = src[:, k * HEAD_TF:(k + 1) * HEAD_TF].astype(BF16)
            else:
                dst[...] = src[...].astype(BF16)

    if has_head:
        @pl.when(i == 0)
        def _():
            @pl.when(j < HEAD_CHUNKS)
            def _():
                rows = FFN_TM // HEAD_CHUNKS
                o_ref[pl.ds(pl.multiple_of(j * rows, rows), rows), :] = head_ref[...]

            side_job()

        computing = i > 0
    else:
        computing = True

    @pl.when(computing & (j == 0))
    def _():
        _ffn_first(x_ref, g_ref, u_ref, _col_tiles(wg_ref), _col_tiles(wu_ref), wd_ref[...], o_ref)
        side_job()

    @pl.when(computing & (j > 0))
    def _():
        _ffn_step(u_ref, _col_tiles(wg_ref), _col_tiles(wu_ref), wd_ref[...], o_ref)
        side_job()


def _ffn(x, g, wg, wu, wd, cast=None, head=None):
    t, d = x.shape
    f = wd.shape[0]
    ni, nj = t // FFN_TM, f // FFN_TF
    per = FFN_TF // HEAD_TF
    has_head = head is not None
    live = (lambda i: jnp.minimum(i, 1)) if has_head else (lambda i: 1)
    row = (lambda i: jnp.maximum(i, 1)) if has_head else (lambda i: i)
    cast_args, cast_in_specs, cast_out_specs, cast_shapes = [], [], [], []
    if cast is not None:
        cg, cu, cd = cast
        rows, cols = d // ni, cd.shape[1] // ni
        assert cg.shape == cu.shape == (d, f) and cd.shape[0] == f and cols % LANES == 0
        cast_args = [cg, cu, cd]
        cast_in_specs = [pl.BlockSpec((rows, FFN_TF), lambda i, j: (i, j))] * 2 + [
            pl.BlockSpec((FFN_TF, cols), lambda i, j: (j, i))]
        cast_out_specs = [pl.BlockSpec((per, rows, HEAD_TF), lambda i, j: (j, i, 0))] * 2 + [
            pl.BlockSpec((FFN_TF, cols), lambda i, j: (j, i))]
        cast_shapes = [jax.ShapeDtypeStruct((f // HEAD_TF, d, HEAD_TF), BF16)] * 2 + [
            jax.ShapeDtypeStruct(cd.shape, BF16)]
    head_specs, head_args = [], []
    if has_head:
        last = HEAD_CHUNKS - 1
        head_specs = [pl.BlockSpec((FFN_TM // HEAD_CHUNKS, d),
                                   lambda i, j: (jnp.where(i == 0, jnp.minimum(j, last), last), 0))]
        head_args = [head]
    col = pl.BlockSpec((per, d, HEAD_TF), lambda i, j: (j * live(i), 0, 0))
    outs = pl.pallas_call(
        functools.partial(_ffn_kernel, n_cast=len(cast_args), has_head=has_head),
        grid=(ni, nj),
        in_specs=[
            pl.BlockSpec((FFN_TM, d), lambda i, j: (row(i), 0)),
            pl.BlockSpec((1, d), lambda i, j: (0, 0)),
            col, col,
            pl.BlockSpec((FFN_TF, d), lambda i, j: (j * live(i), 0)),
        ] + head_specs + cast_in_specs,
        out_specs=[pl.BlockSpec((FFN_TM, d), lambda i, j: (i, 0))] + cast_out_specs,
        out_shape=[jax.ShapeDtypeStruct((t, d), F32)] + cast_shapes,
        scratch_shapes=[pltpu.VMEM((FFN_TM, d), BF16)],
        compiler_params=pltpu.CompilerParams(
            dimension_semantics=("parallel", "arbitrary"), vmem_limit_bytes=VMEM_LIMIT),
        name="ffn_cast" if cast_args else "ffn",
    )(x, g, wg, wu, wd, *head_args, *cast_args)
    return outs[0], tuple(outs[1:])


def _ffn_head_kernel(x_ref, g_ref, wg_ref, wu_ref, wd_ref, o_ref, wgb_ref, wub_ref, wdb_ref, u_ref):
    wgb_ref[...] = wg_ref[...].astype(BF16)
    wub_ref[...] = wu_ref[...].astype(BF16)
    wdb_ref[...] = wd_ref[...].astype(BF16)

    @pl.when(pl.program_id(0) == 0)
    def _():
        _ffn_first(x_ref, g_ref, u_ref, wgb_ref[...], wub_ref[...], wdb_ref[...], o_ref)

    @pl.when(pl.program_id(0) > 0)
    def _():
        _ffn_step(u_ref, wgb_ref[...], wub_ref[...], wdb_ref[...], o_ref)


def _ffn_head(x, g, wg, wu, wd):
    t, d = x.shape
    f = wg.shape[1]
    once = lambda shape: pl.BlockSpec(shape, lambda j: (0, 0), pipeline_mode=pl.Buffered(1))
    col = pl.BlockSpec((d, HEAD_TF), lambda j: (0, j))
    col_major = pl.BlockSpec((None, d, HEAD_TF), lambda j: (j, 0, 0))
    rowb = pl.BlockSpec((HEAD_TF, d), lambda j: (j, 0))
    tiled = jax.ShapeDtypeStruct((f // HEAD_TF, d, HEAD_TF), BF16)
    return pl.pallas_call(
        _ffn_head_kernel,
        grid=(f // HEAD_TF,),
        in_specs=[once((FFN_TM, d)), once((1, d)), col, col, rowb],
        out_specs=[pl.BlockSpec((FFN_TM, d), lambda j: (0, 0)), col_major, col_major, rowb],
        out_shape=[jax.ShapeDtypeStruct((FFN_TM, d), F32), tiled, tiled, jax.ShapeDtypeStruct(wd.shape, BF16)],
        scratch_shapes=[pltpu.VMEM((FFN_TM, d), BF16)],
        compiler_params=pltpu.CompilerParams(dimension_semantics=("arbitrary",), vmem_limit_bytes=VMEM_LIMIT),
        name="ffn_head",
    )(x, g, wg, wu, wd)


def _head_rms(y, gain, bd):
    n = y.shape[1]
    y2 = (y * y).astype(BF16)
    parts = [jnp.dot(y2[:, c * MXU_TILE:(c + 1) * MXU_TILE], bd, preferred_element_type=F32)
             for c in range(n // MXU_TILE)]
    ss = parts[0] if len(parts) == 1 else jnp.concatenate(parts, axis=1)
    return y * lax.rsqrt(ss * (1.0 / HEAD_DIM) + EPS) * gain


def _rope(y, cos, sin_signed, first_half):
    outs = []
    for c in range(y.shape[1] // LANES):
        yc = y[:, c * LANES:(c + 1) * LANES]
        partner = jnp.where(first_half, pltpu.roll(yc, LANES - HEAD_DIM // 2, 1),
                            pltpu.roll(yc, HEAD_DIM // 2, 1))
        outs.append(yc * cos + partner * sin_signed)
    return outs[0] if len(outs) == 1 else jnp.concatenate(outs, axis=1)


def _inproj_kernel(h_ref, g_ref, w_ref, pos_ref, invf_ref, gfq_ref, gfk_ref, gsq_ref, gsk_ref, bf_ref, cast_ref,
                   qf_ref, kf_ref, vf_ref, qs_ref, ks_ref, vs_ref, lf_ref, cast_out_ref):
    u = _rms_rows(h_ref[...], g_ref[...]).astype(BF16)
    cast_out_ref[...] = cast_ref[...].astype(BF16)

    lane = lax.broadcasted_iota(jnp.int32, (1, LANES), 1)
    first_half = (lane % HEAD_DIM) < (HEAD_DIM // 2)
    rope_tables = []

    def rope_table():
        if not rope_tables:
            ang = pos_ref[...].astype(F32) * invf_ref[...]
            sin = jnp.sin(ang)
            rope_tables.append((jnp.cos(ang), jnp.where(first_half, -sin, sin)))
        return rope_tables[0]

    r = lax.broadcasted_iota(jnp.int32, (MXU_TILE, MXU_TILE), 0) // HEAD_DIM
    c = lax.broadcasted_iota(jnp.int32, (MXU_TILE, MXU_TILE), 1) // HEAD_DIM
    bd = jnp.where(r == c, 1.0, 0.0).astype(BF16)

    scale = HEAD_DIM ** -0.5

    def fox_q(y, lo, hi):
        qf_ref[:, lo:hi] = (_head_rms(y, gfq_ref[:, lo:hi], bd) * (scale * LOG2E)).astype(BF16)

    def fox_k(y, lo, hi):
        kf_ref[:, lo:hi] = _head_rms(y, gfk_ref[:, lo:hi], bd).astype(BF16)

    def fox_v(y, lo, hi):
        vf_ref[:, lo:hi] = y.astype(BF16)
        rope_table()

    def swa_q(y, lo, hi):
        y = _rope(_head_rms(y, gsq_ref[:, lo:hi], bd), *rope_table(), first_half)
        qs_ref[:, lo:hi] = (y * (scale * LOG2E)).astype(BF16)

    def swa_k(y, lo, hi):
        ks_ref[:, lo:hi] = _rope(_head_rms(y, gsk_ref[:, lo:hi], bd), *rope_table(), first_half).astype(BF16)

    def swa_v(y, lo, hi):
        vs_ref[:, lo:hi] = y.astype(BF16)

    def gate(y, lo, hi):
        z = y + bf_ref[...]
        lf_ref[...] = jnp.minimum(z, 0.0) - jnp.log1p(jnp.exp(-jnp.abs(z)))

    stages = []
    for start, end, epilogue in ((C_FQ, C_FK, fox_q), (C_FK, C_FV, fox_k), (C_FV, C_SQ, fox_v),
                                 (C_SQ, C_SK, swa_q), (C_SK, C_SV, swa_k), (C_SV, C_GATE, swa_v),
                                 (C_GATE, D_IN_PAD, gate)):
        for lo in range(start, end, INPROJ_TN):
            hi = min(lo + INPROJ_TN, end)
            stages.append((lo, hi, lo - start, hi - start, epilogue))
    pending = []
    for lo, hi, olo, ohi, epilogue in stages:
        y = jnp.dot(u, w_ref[:, lo:hi], preferred_element_type=F32)
        if len(pending) >= INPROJ_DEPTH:
            fn, *args = pending.pop(0)
            fn(*args)
        pending.append((epilogue, y, olo, ohi))
    for fn, *args in pending:
        fn(*args)


def _inproj(h, g, w, pos, invf, gfq, gfk, gsq, gsk, bf, cast):
    t, d = h.shape
    steps = t // INPROJ_TM
    row = lambda n: pl.BlockSpec((INPROJ_TM, n), lambda i: (i, 0))
    const = lambda shape: pl.BlockSpec(shape, lambda i: (0, 0))
    cast_spec = pl.BlockSpec((cast.shape[0] // steps, cast.shape[1]), lambda i: (i, 0))
    return pl.pallas_call(
        _inproj_kernel,
        grid=(steps,),
        in_specs=[
            row(d),
            const((1, d)),
            pl.BlockSpec((d, D_IN_PAD), lambda i: (0, 0), pipeline_mode=pl.Buffered(1)),
            row(1),
            const((1, LANES)),
            const((1, D_FOX)), const((1, D_FOX)), const((1, D_SWA_Q)), const((1, D_SWA_KV)),
            const((1, LANES)),
            cast_spec,
        ],
        out_specs=[row(D_FOX), row(D_FOX), row(D_FOX), row(D_SWA_Q), row(D_SWA_KV), row(D_SWA_KV), row(LANES),
                   cast_spec],
        out_shape=[
            jax.ShapeDtypeStruct((t, D_FOX), BF16), jax.ShapeDtypeStruct((t, D_FOX), BF16),
            jax.ShapeDtypeStruct((t, D_FOX), BF16), jax.ShapeDtypeStruct((t, D_SWA_Q), BF16),
            jax.ShapeDtypeStruct((t, D_SWA_KV), BF16), jax.ShapeDtypeStruct((t, D_SWA_KV), BF16),
            jax.ShapeDtypeStruct((t, LANES), F32), jax.ShapeDtypeStruct(cast.shape, BF16),
        ],
        compiler_params=pltpu.CompilerParams(dimension_semantics=("parallel",), vmem_limit_bytes=VMEM_LIMIT),
        name="inproj",
    )(h, g, w, pos, invf, gfq, gfk, gsq, gsk, bf, cast)


def _split3(x):
    p1 = x.astype(BF16)
    r1 = x - p1.astype(F32)
    p2 = r1.astype(BF16)
    p3 = (r1 - p2.astype(F32)).astype(BF16)
    return p1, p2, p3


def _cumsum_kernel(lf_ref, pmap_ref, ak_ref, aq_ref):
    s = lf_ref.shape[0]
    r = lax.broadcasted_iota(jnp.int32, (CUM_BLK, CUM_BLK), 0)
    c = lax.broadcasted_iota(jnp.int32, (CUM_BLK, CUM_BLK), 1)
    tri = jnp.where(r >= c, 1.0, 0.0).astype(BF16)
    lane = lax.broadcasted_iota(jnp.int32, (1, LANES), 1)
    is_gate = lane < N_GATE
    one = jnp.where(lane == ONE_LANE, 1.0, 0.0)
    local = []
    for b in range(s // CUM_BLK):
        p1, p2, p3 = _split3(lf_ref[b * CUM_BLK:(b + 1) * CUM_BLK, :])
        local.append(jnp.dot(tri, p1, preferred_element_type=F32)
                     + jnp.dot(tri, p2, preferred_element_type=F32)
                     + jnp.dot(tri, p3, preferred_element_type=F32))
    carry = jnp.zeros((1, LANES), F32)
    for b in range(s // CUM_BLK):
        rows = slice(b * CUM_BLK, (b + 1) * CUM_BLK)
        cb = local[b] + carry
        carry = cb[CUM_BLK - 1:CUM_BLK, :]
        terms = [jnp.where(is_gate, t.astype(F32), 0.0) for t in _split3(cb * LOG2E)]
        packed = one + terms[0]
        for i in range(1, AUG_PARTS):
            packed = packed + pltpu.roll(terms[i], i * N_GATE, 1)
        placed = jnp.dot(packed.astype(BF16), pmap_ref[...], preferred_element_type=F32)
        ak_ref[rows, :] = placed[:, :D_FOX].astype(BF16)
        aq_ref[rows, :] = placed[:, D_FOX:].astype(BF16)


def _aug_map():
    pmap = np.zeros((LANES, 2 * D_FOX), np.float32)
    for h in range(FOX_HEADS):
        base = (h // 2) * LANES + (HEAD_DIM if h % 2 == 0 else 0)
        for i in range(AUG_PARTS):
            pmap[i * N_GATE + h, base + i] = -1.0
            pmap[ONE_LANE, D_FOX + base + i] = 1.0
        pmap[ONE_LANE, base + AUG_PARTS] = 1.0
        pmap[h, D_FOX + base + AUG_PARTS] = 1.0
    return jnp.asarray(pmap, BF16)


def _cumsum(lf, batch, seq):
    pmap = _aug_map()
    out = pl.BlockSpec((seq, D_FOX), lambda b: (b, 0))
    return pl.pallas_call(
        _cumsum_kernel,
        grid=(batch,),
        in_specs=[pl.BlockSpec((seq, LANES), lambda b: (b, 0)), pl.BlockSpec(pmap.shape, lambda b: (0, 0))],
        out_specs=[out, out],
        out_shape=[jax.ShapeDtypeStruct((batch * seq, D_FOX), BF16)] * 2,
        compiler_params=pltpu.CompilerParams(dimension_semantics=("parallel",)),
        name="gate_cumsum",
    )(lf, pmap)


def _fox_kernel(q_ref, k_ref, v_ref, aq_ref, ak_ref, o_ref, qa_ref, ka_ref):
    seq = q_ref.shape[0]
    lane = lax.broadcasted_iota(jnp.int32, (1, LANES), 1)
    lo = lane < HEAD_DIM
    for tile in range(FOX_TILES):
        cols = slice(tile * LANES, (tile + 1) * LANES)
        q, k, aq, ak = q_ref[:, cols], k_ref[:, cols], aq_ref[:, cols], ak_ref[:, cols]
        qa_ref[2 * tile] = jnp.where(lo, q, aq)
        qa_ref[2 * tile + 1] = jnp.where(lo, aq, q)
        ka_ref[2 * tile] = jnp.where(lo, k, ak)
        ka_ref[2 * tile + 1] = jnp.where(lo, ak, k)

    qry = lax.broadcasted_iota(jnp.int32, (FOX_TQ, FOX_TQ), 0)
    key = lax.broadcasted_iota(jnp.int32, (FOX_TQ, FOX_TQ), 1)
    causal = key <= qry
    nt = (((1,), (1,)), ((), ()))

    def scores(i, head):
        rows = slice(i * FOX_TQ, (i + 1) * FOX_TQ)
        return lax.dot_general(qa_ref[head, rows, :], ka_ref[head, 0:(i + 1) * FOX_TQ, :], nt,
                               preferred_element_type=F32)

    def softmax(i, s):
        w = (i + 1) * FOX_TQ
        diag = jnp.where(causal, s[:, w - FOX_TQ:], -jnp.inf)
        s = diag if i == 0 else jnp.concatenate([s[:, :w - FOX_TQ], diag], axis=1)
        m = jnp.max(s, axis=1, keepdims=True)
        p = jnp.exp2(s - m)
        return p.astype(BF16), jnp.sum(p, axis=1, keepdims=True)

    def values(i, head, p, l):
        tile = head // 2
        o = jnp.dot(p, v_ref[0:(i + 1) * FOX_TQ, tile * LANES:(tile + 1) * LANES],
                    preferred_element_type=F32)
        return o / l

    chains = [(i, head) for i in reversed(range(seq // FOX_TQ)) for head in range(2 * FOX_TILES)]
    s_buf, p_buf, done = {}, {}, {}
    for t in range(len(chains) + FOX_LAG_V):
        if t < len(chains):
            s_buf[t] = scores(*chains[t])
        if FOX_LAG_S <= t < len(chains) + FOX_LAG_S:
            p_buf[t - FOX_LAG_S] = softmax(chains[t - FOX_LAG_S][0], s_buf.pop(t - FOX_LAG_S))
        if t >= FOX_LAG_V:
            i, head = chains[t - FOX_LAG_V]
            done[(i, head)] = values(i, head, *p_buf.pop(t - FOX_LAG_V))
            if head % 2 == 1:
                rows = slice(i * FOX_TQ, (i + 1) * FOX_TQ)
                cols = slice((head // 2) * LANES, (head // 2 + 1) * LANES)
                o_ref[rows, cols] = jnp.where(lo, done.pop((i, head - 1)), done.pop((i, head))).astype(BF16)


def _fox(q, k, v, aq, ak, batch, seq):
    width = FOX_TILES * LANES
    blk = pl.BlockSpec((seq, width), lambda b, p: (b, p))
    return pl.pallas_call(
        _fox_kernel,
        grid=(batch, D_FOX // width),
        in_specs=[blk] * 5,
        out_specs=blk,
        out_shape=jax.ShapeDtypeStruct((batch * seq, D_FOX), BF16),
        scratch_shapes=[pltpu.VMEM((2 * FOX_TILES, seq, LANES), BF16),
                        pltpu.VMEM((2 * FOX_TILES, seq, LANES), BF16)],
        compiler_params=pltpu.CompilerParams(
            dimension_semantics=("parallel", "arbitrary"), vmem_limit_bytes=VMEM_LIMIT),
        name="fox_attn",
    )(q, k, v, aq, ak)


SWA_ROWS = SWA_GROUP * WINDOW
SWA_UNROLL = 15
SWA_CHAINS = tuple((p, side) for p in range(D_SWA_KV // LANES) for side in range(2))


def _swa_kernel(sink_ref, q_ref, k_ref, v_ref, o_ref, bias_ref, sink_col_ref):
    seq = q_ref.shape[0]
    nblk = seq // WINDOW
    lane = lax.broadcasted_iota(jnp.int32, (1, LANES), 1)
    lo = lane < HEAD_DIM
    nt = (((1,), (1,)), ((), ()))

    r = lax.broadcasted_iota(jnp.int32, (SWA_ROWS, 2 * WINDOW), 0) % WINDOW
    c = lax.broadcasted_iota(jnp.int32, (SWA_ROWS, 2 * WINDOW), 1)
    bias_ref[...] = jnp.where((c > r) & (c <= r + WINDOW), 0.0, -jnp.inf)
    head = lax.broadcasted_iota(jnp.int32, (SWA_ROWS, LANES), 0) // WINDOW
    for n, (p, side) in enumerate(SWA_CHAINS):
        col = jnp.zeros((SWA_ROWS, LANES), F32)
        for j in range(SWA_GROUP):
            col = jnp.where(head == j, sink_ref[8 * p + 4 * side + j] * LOG2E, col)
        sink_col_ref[n] = col

    def block(q_rows, k_rows, first):
        cols = slice(WINDOW, 2 * WINDOW) if first else slice(0, 2 * WINDOW)

        def scores(n):
            p, side = SWA_CHAINS[n]
            tiles = [q_ref[q_rows, (p * SWA_GROUP + j) * LANES:(p * SWA_GROUP + j + 1) * LANES]
                     for j in range(SWA_GROUP)]
            qst = jnp.concatenate(tiles, axis=0)
            qm = jnp.where(lo == (side == 0), qst, jnp.zeros_like(qst))
            kw = k_ref[k_rows, p * LANES:(p + 1) * LANES]
            return lax.dot_general(qm, kw, nt, preferred_element_type=F32)

        def softmax(n, s):
            s = s + bias_ref[:, cols]
            sk = sink_col_ref[n]
            m = jnp.maximum(jnp.max(s, axis=1, keepdims=True), sk)
            e = jnp.exp2(s - jnp.concatenate([m] * (s.shape[1] // LANES), axis=1))
            l = jnp.sum(e, axis=1, keepdims=True) + jnp.exp2(sk - m)
            return e.astype(BF16), l

        def values(n, e, l):
            p, _ = SWA_CHAINS[n]
            vw = v_ref[k_rows, p * LANES:(p + 1) * LANES]
            return jnp.dot(e, vw, preferred_element_type=F32) / l

        s_buf, e_buf, done = {}, {}, {}
        for t in range(len(SWA_CHAINS) + 2):
            if t < len(SWA_CHAINS):
                s_buf[t] = scores(t)
            if 1 <= t <= len(SWA_CHAINS):
                e_buf[t - 1] = softmax(t - 1, s_buf.pop(t - 1))
            if t >= 2:
                p, side = SWA_CHAINS[t - 2]
                done[side] = values(t - 2, *e_buf.pop(t - 2))
                if side == 1:
                    o_both = jnp.where(lo, done.pop(0), done.pop(1)).astype(BF16)
                    for j in range(SWA_GROUP):
                        tile = p * SWA_GROUP + j
                        o_ref[q_rows, tile * LANES:(tile + 1) * LANES] = o_both[j * WINDOW:(j + 1) * WINDOW, :]

    block(slice(0, WINDOW), slice(0, WINDOW), True)

    def body(n, carry):
        q0 = pl.multiple_of(n * WINDOW, WINDOW)
        k0 = pl.multiple_of((n - 1) * WINDOW, WINDOW)
        block(pl.ds(q0, WINDOW), pl.ds(k0, 2 * WINDOW), False)
        return carry

    lax.fori_loop(1, nblk, body, 0, unroll=SWA_UNROLL)


def _swa(sinks, q, k, v, batch, seq):
    return pl.pallas_call(
        _swa_kernel,
        grid=(batch,),
        in_specs=[pl.BlockSpec(memory_space=pltpu.SMEM),
                  pl.BlockSpec((seq, D_SWA_Q), lambda b: (b, 0)),
                  pl.BlockSpec((seq, D_SWA_KV), lambda b: (b, 0)),
                  pl.BlockSpec((seq, D_SWA_KV), lambda b: (b, 0))],
        out_specs=pl.BlockSpec((seq, D_SWA_Q), lambda b: (b, 0)),
        out_shape=jax.ShapeDtypeStruct((batch * seq, D_SWA_Q), BF16),
        scratch_shapes=[pltpu.VMEM((SWA_ROWS, 2 * WINDOW), F32),
                        pltpu.VMEM((len(SWA_CHAINS), SWA_ROWS, LANES), F32)],
        compiler_params=pltpu.CompilerParams(dimension_semantics=("parallel",), vmem_limit_bytes=VMEM_LIMIT),
        name="swa_attn",
    )(sinks, q, k, v)


def _outproj_kernel(of_ref, os_ref, h_ref, gf_ref, gs_ref, wf_ref, ws_ref, o_ref, wsp_ref):
    @pl.when(pl.program_id(0) == 0)
    def _():
        for k, h in enumerate(SWA_HEAD_ORDER):
            wsp_ref[k * HEAD_DIM:(k + 1) * HEAD_DIM, :] = ws_ref[h * HEAD_DIM:(h + 1) * HEAD_DIM, :]

    rows = [slice(c * OUTPROJ_ROWS, (c + 1) * OUTPROJ_ROWS) for c in range(OUTPROJ_TM // OUTPROJ_ROWS)]

    def norms(r):
        return (_rms_rows(of_ref[r, :].astype(F32), gf_ref[...]).astype(BF16),
                _rms_rows(os_ref[r, :].astype(F32), gs_ref[...]).astype(BF16))

    ready = norms(rows[0])
    for c, r in enumerate(rows):
        nf, ns = ready
        if c + 1 < len(rows):
            ready = norms(rows[c + 1])
        acc = jnp.dot(nf, wf_ref[...], preferred_element_type=F32)
        acc = acc + jnp.dot(ns, wsp_ref[...], preferred_element_type=F32)
        o_ref[r, :] = h_ref[r, :] + acc


def _outproj(of, os_, h, gf, gs, w):
    t, d = h.shape
    row = lambda n: pl.BlockSpec((OUTPROJ_TM, n), lambda i: (i, 0))
    const = lambda shape: pl.BlockSpec(shape, lambda i: (0, 0))
    half = lambda n: pl.BlockSpec((D_FOX, d), lambda i: (n, 0), pipeline_mode=pl.Buffered(1))
    return pl.pallas_call(
        _outproj_kernel,
        grid=(t // OUTPROJ_TM,),
        in_specs=[row(D_FOX), row(D_SWA_Q), row(d), const((1, D_FOX)), const((1, D_SWA_Q)), half(0), half(1)],
        out_specs=row(d),
        out_shape=jax.ShapeDtypeStruct((t, d), F32),
        scratch_shapes=[pltpu.VMEM((D_SWA_Q, d), BF16)],
        compiler_params=pltpu.CompilerParams(dimension_semantics=("arbitrary",), vmem_limit_bytes=VMEM_LIMIT),
        name="outproj",
    )(of, os_, h, gf, gs, w, w)


def _swa_reorder(a, axis):
    shape = a.shape
    a = a.reshape(shape[:axis] + (2, 2, SWA_GROUP, HEAD_DIM) + shape[axis + 1:])
    a = jnp.swapaxes(a, axis + 1, axis + 2)
    return a.reshape(shape)


def _w_in_head_rows():
    gate0 = 3 * D_FOX
    sq0 = gate0 + N_GATE
    starts = [c for c in range(0, gate0, HEAD_DIM)]
    starts += [sq0 + h * HEAD_DIM for h in SWA_HEAD_ORDER]
    starts += [sq0 + D_SWA_Q + c for c in range(0, 2 * D_SWA_KV, HEAD_DIM)]
    starts += [gate0, gate0]
    assert len(starts) * HEAD_DIM == D_IN_PAD
    return np.asarray(starts, np.int32)


def _w_in_prep_kernel(wt_ref, o_ref):
    starts = [int(s) for s in _w_in_head_rows()]
    row = lax.broadcasted_iota(jnp.int32, (LANES, 1), 0)
    for t in range(D_IN_PAD // LANES):
        pieces = [wt_ref[s:s + HEAD_DIM, :] for s in starts[2 * t:2 * t + 2]]
        x = jnp.concatenate(pieces, axis=0)
        if t == D_IN_PAD // LANES - 1:
            x = jnp.where(row >= N_GATE, 0.0, x)
        o_ref[:, t * LANES:(t + 1) * LANES] = jnp.transpose(x).astype(BF16)


def _w_in_prep(w_in_stack, layer):
    _, d, n = w_in_stack.shape
    wt = jnp.swapaxes(w_in_stack, 1, 2)
    return pl.pallas_call(
        _w_in_prep_kernel,
        grid=(d // WPREP_COLS,),
        in_specs=[pl.BlockSpec((None, n, WPREP_COLS), lambda i: (layer, 0, i))],
        out_specs=pl.BlockSpec((WPREP_COLS, D_IN_PAD), lambda i: (i, 0)),
        out_shape=jax.ShapeDtypeStruct((d, D_IN_PAD), BF16),
        compiler_params=pltpu.CompilerParams(dimension_semantics=("parallel",), vmem_limit_bytes=VMEM_LIMIT),
        name="w_in_prep",
    )(wt)


def _layer(h, pos, invf, p, w_in_stack, layer):
    (norm_ffn1_g, w_gate1, w_up1, w_down1, norm_mix_g, b_forget, fox_q_g, fox_k_g, swa_q_g, swa_k_g,
     sinks, out_fox_g, out_swa_g, w_out, norm_ffn2_g, w_gate2, w_up2, w_down2) = p
    batch_seq, d = h.shape
    batch, seq = pos.shape
    row = lambda v: v.reshape(1, -1).astype(F32)
    tile = lambda v, n: jnp.tile(v.astype(F32), n).reshape(1, -1)

    w_in_r = _w_in_prep(w_in_stack, layer)
    bf_pad = jnp.concatenate([b_forget.astype(F32), jnp.zeros((LANES - N_GATE,), F32)]).reshape(1, LANES)

    h1_head, w_gate1_b, w_up1_b, w_down1_b = _ffn_head(h, row(norm_ffn1_g), w_gate1, w_up1, w_down1)
    h1, (w_gate2_b, w_up2_b, w_down2_b) = _ffn(
        h, row(norm_ffn1_g), w_gate1_b, w_up1_b, w_down1_b, cast=(w_gate2, w_up2, w_down2), head=h1_head)

    qf, kf, vf, qs, ks, vs, lf, w_out_b = _inproj(
        h1, row(norm_mix_g), w_in_r, pos.reshape(batch_seq, 1), invf,
        tile(fox_q_g, FOX_HEADS), tile(fox_k_g, FOX_HEADS), tile(swa_q_g, SWA_Q_HEADS),
        tile(swa_k_g, SWA_KV_HEADS), bf_pad, cast=w_out)

    ak, aq = _cumsum(lf, batch, seq)
    o_fox = _fox(qf, kf, vf, aq, ak, batch, seq)
    o_swa = _swa(sinks.astype(F32), qs, ks, vs, batch, seq)

    h2 = _outproj(o_fox, o_swa, h1, row(out_fox_g), row(_swa_reorder(out_swa_g, 0)), w_out_b)
    return _ffn(h2, row(norm_ffn2_g), w_gate2_b, w_up2_b, w_down2_b)[0]


def kernel(x, positions, norm_ffn1_g, ffn1_w_gate, ffn1_w_up, ffn1_w_down, norm_mix_g, w_in, b_forget, fox_q_norm_g, fox_k_norm_g, swa_q_norm_g, swa_k_norm_g, swa_sinks, out_norm_fox_g, out_norm_swa_g, w_out, norm_ffn2_g, ffn2_w_gate, ffn2_w_up, ffn2_w_down):
    batch, seq, d = x.shape
    half = jnp.arange(0, HEAD_DIM, 2, dtype=F32)
    inv_freq = ROPE_THETA ** (-half / HEAD_DIM)
    invf = jnp.tile(inv_freq, LANES // (HEAD_DIM // 2)).reshape(1, LANES)
    stacks = (norm_ffn1_g, ffn1_w_gate, ffn1_w_up, ffn1_w_down, norm_mix_g, b_forget, fox_q_norm_g,
              fox_k_norm_g, swa_q_norm_g, swa_k_norm_g, swa_sinks, out_norm_fox_g, out_norm_swa_g, w_out,
              norm_ffn2_g, ffn2_w_gate, ffn2_w_up, ffn2_w_down)
    h = x.reshape(batch * seq, d)
    for layer in range(norm_ffn1_g.shape[0]):
        h = _layer(h, positions, invf, tuple(s[layer] for s in stacks), w_in, layer)
    return h.reshape(batch, seq, d)
```

```python
import functools

import jax
import jax.numpy as jnp
import numpy as np
from jax import lax
from jax.experimental import pallas as pl
from jax.experimental.pallas import tpu as pltpu

F32 = jnp.float32
BF16 = jnp.bfloat16

HEAD_DIM = 64
FOX_HEADS = 16
SWA_Q_HEADS = 16
SWA_KV_HEADS = 4
SWA_GROUP = SWA_Q_HEADS // SWA_KV_HEADS
WINDOW = 128
ROPE_THETA = 10000.0
EPS = 1e-6

D_FOX = FOX_HEADS * HEAD_DIM
D_SWA_Q = SWA_Q_HEADS * HEAD_DIM
D_SWA_KV = SWA_KV_HEADS * HEAD_DIM
N_GATE = FOX_HEADS

LANES = 128
MXU_TILE = 256
VMEM_LIMIT = 60 * 1024 * 1024

C_FQ = 0
C_FK = C_FQ + D_FOX
C_FV = C_FK + D_FOX
C_SQ = C_FV + D_FOX
C_SK = C_SQ + D_SWA_Q
C_SV = C_SK + D_SWA_KV
C_GATE = C_SV + D_SWA_KV
D_IN_PAD = C_GATE + LANES

SWA_HEAD_ORDER = tuple(8 * p + 4 * side + j for p in range(2) for j in range(SWA_GROUP) for side in range(2))

FFN_TM = 1024
FFN_TF = 512
HEAD_TF = 256
HEAD_CHUNKS = 8
WPREP_COLS = 512
INPROJ_TM = 512
INPROJ_TN = 512
INPROJ_DEPTH = 1
OUTPROJ_TM = 512
OUTPROJ_ROWS = 256
FOX_TQ = 256
FOX_TILES = 2
FOX_LAG_S, FOX_LAG_V = 1, 2
CUM_BLK = 256

LOG2E = 1.4426950408889634

AUG_PARTS = 3
ONE_LANE = AUG_PARTS * N_GATE


def _rms_rows(x, g):
    return x * lax.rsqrt(jnp.mean(x * x, axis=-1, keepdims=True) + EPS) * g


def _swiglu_tile(u, wg, wu, wd):
    gate = jnp.dot(u, wg, preferred_element_type=F32)
    up = jnp.dot(u, wu, preferred_element_type=F32)
    a = (gate * jax.nn.sigmoid(gate) * up * 0.5).astype(BF16)
    return jnp.dot(a, wd, preferred_element_type=F32)


def _ffn_first(x_ref, g_ref, u_ref, wg, wu, wd, o_ref):
    x = x_ref[...]
    u = _rms_rows(x, g_ref[...]).astype(BF16)
    u_ref[...] = u
    o_ref[...] = x + _swiglu_tile(u, wg, wu, wd)


def _ffn_step(u_ref, wg, wu, wd, o_ref):
    o_ref[...] += _swiglu_tile(u_ref[...], wg, wu, wd)


def _ffn_kernel(x_ref, g_ref, wg_ref, wu_ref, wd_ref, *refs, n_cast, has_head):
    head_ref = refs[0] if has_head else None
    refs = refs[1:] if has_head else refs
    cast_in, o_ref, cast_out, u_ref = refs[:n_cast], refs[n_cast], refs[n_cast + 1:-1], refs[-1]
    i, j = pl.program_id(0), pl.program_id(1)

    def side_job():
        for src, dst in zip(cast_in, cast_out):
            dst[...] = src[...].astype(BF16)

    if has_head:
        @pl.when(i == 0)
        def _():
            @pl.when(j < HEAD_CHUNKS)
            def _():
                rows = FFN_TM // HEAD_CHUNKS
                o_ref[pl.ds(pl.multiple_of(j * rows, rows), rows), :] = head_ref[...]

            side_job()

        computing = i > 0
    else:
        computing = True

    @pl.when(computing & (j == 0))
    def _():
        _ffn_first(x_ref, g_ref, u_ref, wg_ref[...], wu_ref[...], wd_ref[...], o_ref)
        side_job()

    @pl.when(computing & (j > 0))
    def _():
        _ffn_step(u_ref, wg_ref[...], wu_ref[...], wd_ref[...], o_ref)
        side_job()


def _ffn(x, g, wg, wu, wd, cast=(), head=None):
    t, d = x.shape
    f = wg.shape[1]
    ni, nj = t // FFN_TM, f // FFN_TF
    has_head = head is not None
    live = (lambda i: jnp.minimum(i, 1)) if has_head else (lambda i: 1)
    row = (lambda i: jnp.maximum(i, 1)) if has_head else (lambda i: i)
    cast_specs = []
    for w in cast:
        r, c = w.shape
        if r % ni == 0 and c % nj == 0 and (c // nj) % LANES == 0:
            cast_specs.append(pl.BlockSpec((r // ni, c // nj), lambda i, j: (i, j)))
        elif r % nj == 0 and c % ni == 0 and (c // ni) % LANES == 0:
            cast_specs.append(pl.BlockSpec((r // nj, c // ni), lambda i, j: (j, i)))
        else:
            assert r % ni == 0, w.shape
            cast_specs.append(pl.BlockSpec((r // ni, c), lambda i, j: (i, 0)))
    head_specs, head_args = [], []
    if has_head:
        last = HEAD_CHUNKS - 1
        head_specs = [pl.BlockSpec((FFN_TM // HEAD_CHUNKS, d),
                                   lambda i, j: (jnp.where(i == 0, jnp.minimum(j, last), last), 0))]
        head_args = [head]
    outs = pl.pallas_call(
        functools.partial(_ffn_kernel, n_cast=len(cast), has_head=has_head),
        grid=(ni, nj),
        in_specs=[
            pl.BlockSpec((FFN_TM, d), lambda i, j: (row(i), 0)),
            pl.BlockSpec((1, d), lambda i, j: (0, 0)),
            pl.BlockSpec((d, FFN_TF), lambda i, j: (0, j * live(i))),
            pl.BlockSpec((d, FFN_TF), lambda i, j: (0, j * live(i))),
            pl.BlockSpec((FFN_TF, d), lambda i, j: (j * live(i), 0)),
        ] + head_specs + cast_specs,
        out_specs=[pl.BlockSpec((FFN_TM, d), lambda i, j: (i, 0))] + cast_specs,
        out_shape=[jax.ShapeDtypeStruct((t, d), F32)] + [jax.ShapeDtypeStruct(w.shape, BF16) for w in cast],
        scratch_shapes=[pltpu.VMEM((FFN_TM, d), BF16)],
        compiler_params=pltpu.CompilerParams(
            dimension_semantics=("parallel", "arbitrary"), vmem_limit_bytes=VMEM_LIMIT),
        name="ffn_cast" if cast else "ffn",
    )(x, g, wg, wu, wd, *head_args, *cast)
    return outs[0], tuple(outs[1:])


def _ffn_head_kernel(x_ref, g_ref, wg_ref, wu_ref, wd_ref, o_ref, wgb_ref, wub_ref, wdb_ref, u_ref):
    wgb_ref[...] = wg_ref[...].astype(BF16)
    wub_ref[...] = wu_ref[...].astype(BF16)
    wdb_ref[...] = wd_ref[...].astype(BF16)

    @pl.when(pl.program_id(0) == 0)
    def _():
        _ffn_first(x_ref, g_ref, u_ref, wgb_ref[...], wub_ref[...], wdb_ref[...], o_ref)

    @pl.when(pl.program_id(0) > 0)
    def _():
        _ffn_step(u_ref, wgb_ref[...], wub_ref[...], wdb_ref[...], o_ref)


def _ffn_head(x, g, wg, wu, wd):
    t, d = x.shape
    f = wg.shape[1]
    once = lambda shape: pl.BlockSpec(shape, lambda j: (0, 0), pipeline_mode=pl.Buffered(1))
    col = pl.BlockSpec((d, HEAD_TF), lambda j: (0, j))
    rowb = pl.BlockSpec((HEAD_TF, d), lambda j: (j, 0))
    return pl.pallas_call(
        _ffn_head_kernel,
        grid=(f // HEAD_TF,),
        in_specs=[once((FFN_TM, d)), once((1, d)), col, col, rowb],
        out_specs=[pl.BlockSpec((FFN_TM, d), lambda j: (0, 0)), col, col, rowb],
        out_shape=[jax.ShapeDtypeStruct((FFN_TM, d), F32), jax.ShapeDtypeStruct(wg.shape, BF16),
                   jax.ShapeDtypeStruct(wu.shape, BF16), jax.ShapeDtypeStruct(wd.shape, BF16)],
        scratch_shapes=[pltpu.VMEM((FFN_TM, d), BF16)],
        compiler_params=pltpu.CompilerParams(dimension_semantics=("arbitrary",), vmem_limit_bytes=VMEM_LIMIT),
        name="ffn_head",
    )(x, g, wg, wu, wd)


def _head_rms(y, gain, bd):
    n = y.shape[1]
    y2 = (y * y).astype(BF16)
    parts = [jnp.dot(y2[:, c * MXU_TILE:(c + 1) * MXU_TILE], bd, preferred_element_type=F32)
             for c in range(n // MXU_TILE)]
    ss = parts[0] if len(parts) == 1 else jnp.concatenate(parts, axis=1)
    return y * lax.rsqrt(ss * (1.0 / HEAD_DIM) + EPS) * gain


def _rope(y, cos, sin_signed, first_half):
    outs = []
    for c in range(y.shape[1] // LANES):
        yc = y[:, c * LANES:(c + 1) * LANES]
        partner = jnp.where(first_half, pltpu.roll(yc, LANES - HEAD_DIM // 2, 1),
                            pltpu.roll(yc, HEAD_DIM // 2, 1))
        outs.append(yc * cos + partner * sin_signed)
    return outs[0] if len(outs) == 1 else jnp.concatenate(outs, axis=1)


def _inproj_kernel(h_ref, g_ref, w_ref, pos_ref, invf_ref, gfq_ref, gfk_ref, gsq_ref, gsk_ref, bf_ref, cast_ref,
                   qf_ref, kf_ref, vf_ref, qs_ref, ks_ref, vs_ref, lf_ref, cast_out_ref):
    u = _rms_rows(h_ref[...], g_ref[...]).astype(BF16)
    cast_out_ref[...] = cast_ref[...].astype(BF16)

    lane = lax.broadcasted_iota(jnp.int32, (1, LANES), 1)
    first_half = (lane % HEAD_DIM) < (HEAD_DIM // 2)
    rope_tables = []

    def rope_table():
        if not rope_tables:
            ang = pos_ref[...].astype(F32) * invf_ref[...]
            sin = jnp.sin(ang)
            rope_tables.append((jnp.cos(ang), jnp.where(first_half, -sin, sin)))
        return rope_tables[0]

    r = lax.broadcasted_iota(jnp.int32, (MXU_TILE, MXU_TILE), 0) // HEAD_DIM
    c = lax.broadcasted_iota(jnp.int32, (MXU_TILE, MXU_TILE), 1) // HEAD_DIM
    bd = jnp.where(r == c, 1.0, 0.0).astype(BF16)

    scale = HEAD_DIM ** -0.5

    def fox_q(y, lo, hi):
        qf_ref[:, lo:hi] = (_head_rms(y, gfq_ref[:, lo:hi], bd) * (scale * LOG2E)).astype(BF16)

    def fox_k(y, lo, hi):
        kf_ref[:, lo:hi] = _head_rms(y, gfk_ref[:, lo:hi], bd).astype(BF16)

    def fox_v(y, lo, hi):
        vf_ref[:, lo:hi] = y.astype(BF16)
        rope_table()

    def swa_q(y, lo, hi):
        y = _rope(_head_rms(y, gsq_ref[:, lo:hi], bd), *rope_table(), first_half)
        qs_ref[:, lo:hi] = (y * (scale * LOG2E)).astype(BF16)

    def swa_k(y, lo, hi):
        ks_ref[:, lo:hi] = _rope(_head_rms(y, gsk_ref[:, lo:hi], bd), *rope_table(), first_half).astype(BF16)

    def swa_v(y, lo, hi):
        vs_ref[:, lo:hi] = y.astype(BF16)

    def gate(y, lo, hi):
        z = y + bf_ref[...]
        lf_ref[...] = jnp.minimum(z, 0.0) - jnp.log1p(jnp.exp(-jnp.abs(z)))

    stages = []
    for start, end, epilogue in ((C_FQ, C_FK, fox_q), (C_FK, C_FV, fox_k), (C_FV, C_SQ, fox_v),
                                 (C_SQ, C_SK, swa_q), (C_SK, C_SV, swa_k), (C_SV, C_GATE, swa_v),
                                 (C_GATE, D_IN_PAD, gate)):
        for lo in range(start, end, INPROJ_TN):
            hi = min(lo + INPROJ_TN, end)
            stages.append((lo, hi, lo - start, hi - start, epilogue))
    pending = []
    for lo, hi, olo, ohi, epilogue in stages:
        y = jnp.dot(u, w_ref[:, lo:hi], preferred_element_type=F32)
        if len(pending) >= INPROJ_DEPTH:
            fn, *args = pending.pop(0)
            fn(*args)
        pending.append((epilogue, y, olo, ohi))
    for fn, *args in pending:
        fn(*args)


def _inproj(h, g, w, pos, invf, gfq, gfk, gsq, gsk, bf, cast):
    t, d = h.shape
    steps = t // INPROJ_TM
    row = lambda n: pl.BlockSpec((INPROJ_TM, n), lambda i: (i, 0))
    const = lambda shape: pl.BlockSpec(shape, lambda i: (0, 0))
    cast_spec = pl.BlockSpec((cast.shape[0] // steps, cast.shape[1]), lambda i: (i, 0))
    return pl.pallas_call(
        _inproj_kernel,
        grid=(steps,),
        in_specs=[
            row(d),
            const((1, d)),
            pl.BlockSpec((d, D_IN_PAD), lambda i: (0, 0), pipeline_mode=pl.Buffered(1)),
            row(1),
            const((1, LANES)),
            const((1, D_FOX)), const((1, D_FOX)), const((1, D_SWA_Q)), const((1, D_SWA_KV)),
            const((1, LANES)),
            cast_spec,
        ],
        out_specs=[row(D_FOX), row(D_FOX), row(D_FOX), row(D_SWA_Q), row(D_SWA_KV), row(D_SWA_KV), row(LANES),
                   cast_spec],
        out_shape=[
            jax.ShapeDtypeStruct((t, D_FOX), BF16), jax.ShapeDtypeStruct((t, D_FOX), BF16),
            jax.ShapeDtypeStruct((t, D_FOX), BF16), jax.ShapeDtypeStruct((t, D_SWA_Q), BF16),
            jax.ShapeDtypeStruct((t, D_SWA_KV), BF16), jax.ShapeDtypeStruct((t, D_SWA_KV), BF16),
            jax.ShapeDtypeStruct((t, LANES), F32), jax.ShapeDtypeStruct(cast.shape, BF16),
        ],
        compiler_params=pltpu.CompilerParams(dimension_semantics=("parallel",), vmem_limit_bytes=VMEM_LIMIT),
        name="inproj",
    )(h, g, w, pos, invf, gfq, gfk, gsq, gsk, bf, cast)


def _split3(x):
    p1 = x.astype(BF16)
    r1 = x - p1.astype(F32)
    p2 = r1.astype(BF16)
    p3 = (r1 - p2.astype(F32)).astype(BF16)
    return p1, p2, p3


def _cumsum_kernel(lf_ref, pmap_ref, ak_ref, aq_ref):
    s = lf_ref.shape[0]
    r = lax.broadcasted_iota(jnp.int32, (CUM_BLK, CUM_BLK), 0)
    c = lax.broadcasted_iota(jnp.int32, (CUM_BLK, CUM_BLK), 1)
    tri = jnp.where(r >= c, 1.0, 0.0).astype(BF16)
    lane = lax.broadcasted_iota(jnp.int32, (1, LANES), 1)
    is_gate = lane < N_GATE
    one = jnp.where(lane == ONE_LANE, 1.0, 0.0)
    local = []
    for b in range(s // CUM_BLK):
        p1, p2, p3 = _split3(lf_ref[b * CUM_BLK:(b + 1) * CUM_BLK, :])
        local.append(jnp.dot(tri, p1, preferred_element_type=F32)
                     + jnp.dot(tri, p2, preferred_element_type=F32)
                     + jnp.dot(tri, p3, preferred_element_type=F32))
    carry = jnp.zeros((1, LANES), F32)
    for b in range(s // CUM_BLK):
        rows = slice(b * CUM_BLK, (b + 1) * CUM_BLK)
        cb = local[b] + carry
        carry = cb[CUM_BLK - 1:CUM_BLK, :]
        terms = [jnp.where(is_gate, t.astype(F32), 0.0) for t in _split3(cb * LOG2E)]
        packed = one + terms[0]
        for i in range(1, AUG_PARTS):
            packed = packed + pltpu.roll(terms[i], i * N_GATE, 1)
        placed = jnp.dot(packed.astype(BF16), pmap_ref[...], preferred_element_type=F32)
        ak_ref[rows, :] = placed[:, :D_FOX].astype(BF16)
        aq_ref[rows, :] = placed[:, D_FOX:].astype(BF16)


def _aug_map():
    pmap = np.zeros((LANES, 2 * D_FOX), np.float32)
    for h in range(FOX_HEADS):
        base = (h // 2) * LANES + (HEAD_DIM if h % 2 == 0 else 0)
        for i in range(AUG_PARTS):
            pmap[i * N_GATE + h, base + i] = -1.0
            pmap[ONE_LANE, D_FOX + base + i] = 1.0
        pmap[ONE_LANE, base + AUG_PARTS] = 1.0
        pmap[h, D_FOX + base + AUG_PARTS] = 1.0
    return jnp.asarray(pmap, BF16)


def _cumsum(lf, batch, seq):
    pmap = _aug_map()
    out = pl.BlockSpec((seq, D_FOX), lambda b: (b, 0))
    return pl.pallas_call(
        _cumsum_kernel,
        grid=(batch,),
        in_specs=[pl.BlockSpec((seq, LANES), lambda b: (b, 0)), pl.BlockSpec(pmap.shape, lambda b: (0, 0))],
        out_specs=[out, out],
        out_shape=[jax.ShapeDtypeStruct((batch * seq, D_FOX), BF16)] * 2,
        compiler_params=pltpu.CompilerParams(dimension_semantics=("parallel",)),
        name="gate_cumsum",
    )(lf, pmap)


def _fox_kernel(q_ref, k_ref, v_ref, aq_ref, ak_ref, *refs, n_cast):
    cast_in, o_ref, cast_out = refs[:n_cast], refs[n_cast], refs[n_cast + 1:2 * n_cast + 1]
    qa_ref, ka_ref = refs[2 * n_cast + 1:]
    for src, dst in zip(cast_in, cast_out):
        dst[...] = src[...].astype(BF16)
    seq = q_ref.shape[0]
    lane = lax.broadcasted_iota(jnp.int32, (1, LANES), 1)
    lo = lane < HEAD_DIM
    for tile in range(FOX_TILES):
        cols = slice(tile * LANES, (tile + 1) * LANES)
        q, k, aq, ak = q_ref[:, cols], k_ref[:, cols], aq_ref[:, cols], ak_ref[:, cols]
        qa_ref[2 * tile] = jnp.where(lo, q, aq)
        qa_ref[2 * tile + 1] = jnp.where(lo, aq, q)
        ka_ref[2 * tile] = jnp.where(lo, k, ak)
        ka_ref[2 * tile + 1] = jnp.where(lo, ak, k)

    qry = lax.broadcasted_iota(jnp.int32, (FOX_TQ, FOX_TQ), 0)
    key = lax.broadcasted_iota(jnp.int32, (FOX_TQ, FOX_TQ), 1)
    causal = key <= qry
    nt = (((1,), (1,)), ((), ()))

    def scores(i, head):
        rows = slice(i * FOX_TQ, (i + 1) * FOX_TQ)
        return lax.dot_general(qa_ref[head, rows, :], ka_ref[head, 0:(i + 1) * FOX_TQ, :], nt,
                               preferred_element_type=F32)

    def softmax(i, s):
        w = (i + 1) * FOX_TQ
        diag = jnp.where(causal, s[:, w - FOX_TQ:], -jnp.inf)
        s = diag if i == 0 else jnp.concatenate([s[:, :w - FOX_TQ], diag], axis=1)
        m = jnp.max(s, axis=1, keepdims=True)
        p = jnp.exp2(s - m)
        return p.astype(BF16), jnp.sum(p, axis=1, keepdims=True)

    def values(i, head, p, l):
        tile = head // 2
        o = jnp.dot(p, v_ref[0:(i + 1) * FOX_TQ, tile * LANES:(tile + 1) * LANES],
                    preferred_element_type=F32)
        return o / l

    chains = [(i, head) for i in reversed(range(seq // FOX_TQ)) for head in range(2 * FOX_TILES)]
    s_buf, p_buf, done = {}, {}, {}
    for t in range(len(chains) + FOX_LAG_V):
        if t < len(chains):
            s_buf[t] = scores(*chains[t])
        if FOX_LAG_S <= t < len(chains) + FOX_LAG_S:
            p_buf[t - FOX_LAG_S] = softmax(chains[t - FOX_LAG_S][0], s_buf.pop(t - FOX_LAG_S))
        if t >= FOX_LAG_V:
            i, head = chains[t - FOX_LAG_V]
            done[(i, head)] = values(i, head, *p_buf.pop(t - FOX_LAG_V))
            if head % 2 == 1:
                rows = slice(i * FOX_TQ, (i + 1) * FOX_TQ)
                cols = slice((head // 2) * LANES, (head // 2 + 1) * LANES)
                o_ref[rows, cols] = jnp.where(lo, done.pop((i, head - 1)), done.pop((i, head))).astype(BF16)


def _fox(q, k, v, aq, ak, batch, seq, cast=()):
    width = FOX_TILES * LANES
    tiles = D_FOX // width
    blk = pl.BlockSpec((seq, width), lambda b, p: (b, p))
    cast_specs = [pl.BlockSpec((w.shape[0] // (batch * tiles), w.shape[1]), lambda b, p: (b * tiles + p, 0))
                  for w in cast]
    outs = pl.pallas_call(
        functools.partial(_fox_kernel, n_cast=len(cast)),
        grid=(batch, tiles),
        in_specs=[blk] * 5 + cast_specs,
        out_specs=[blk] + cast_specs,
        out_shape=[jax.ShapeDtypeStruct((batch * seq, D_FOX), BF16)]
                  + [jax.ShapeDtypeStruct(w.shape, BF16) for w in cast],
        scratch_shapes=[pltpu.VMEM((2 * FOX_TILES, seq, LANES), BF16),
                        pltpu.VMEM((2 * FOX_TILES, seq, LANES), BF16)],
        compiler_params=pltpu.CompilerParams(
            dimension_semantics=("parallel", "arbitrary"), vmem_limit_bytes=VMEM_LIMIT),
        name="fox_attn",
    )(q, k, v, aq, ak, *cast)
    return outs[0], tuple(outs[1:])


SWA_ROWS = SWA_GROUP * WINDOW
SWA_UNROLL = 15
SWA_CHAINS = tuple((p, side) for p in range(D_SWA_KV // LANES) for side in range(2))


def _swa_kernel(sink_ref, q_ref, k_ref, v_ref, o_ref, bias_ref, sink_col_ref):
    seq = q_ref.shape[0]
    nblk = seq // WINDOW
    lane = lax.broadcasted_iota(jnp.int32, (1, LANES), 1)
    lo = lane < HEAD_DIM
    nt = (((1,), (1,)), ((), ()))

    r = lax.broadcasted_iota(jnp.int32, (SWA_ROWS, 2 * WINDOW), 0) % WINDOW
    c = lax.broadcasted_iota(jnp.int32, (SWA_ROWS, 2 * WINDOW), 1)
    bias_ref[...] = jnp.where((c > r) & (c <= r + WINDOW), 0.0, -jnp.inf)
    head = lax.broadcasted_iota(jnp.int32, (SWA_ROWS, LANES), 0) // WINDOW
    for n, (p, side) in enumerate(SWA_CHAINS):
        col = jnp.zeros((SWA_ROWS, LANES), F32)
        for j in range(SWA_GROUP):
            col = jnp.where(head == j, sink_ref[8 * p + 4 * side + j] * LOG2E, col)
        sink_col_ref[n] = col

    def block(q_rows, k_rows, first):
        cols = slice(WINDOW, 2 * WINDOW) if first else slice(0, 2 * WINDOW)

        def scores(n):
            p, side = SWA_CHAINS[n]
            tiles = [q_ref[q_rows, (p * SWA_GROUP + j) * LANES:(p * SWA_GROUP + j + 1) * LANES]
                     for j in range(SWA_GROUP)]
            qst = jnp.concatenate(tiles, axis=0)
            qm = jnp.where(lo == (side == 0), qst, jnp.zeros_like(qst))
            kw = k_ref[k_rows, p * LANES:(p + 1) * LANES]
            return lax.dot_general(qm, kw, nt, preferred_element_type=F32)

        def softmax(n, s):
            s = s + bias_ref[:, cols]
            sk = sink_col_ref[n]
            m = jnp.maximum(jnp.max(s, axis=1, keepdims=True), sk)
            e = jnp.exp2(s - jnp.concatenate([m] * (s.shape[1] // LANES), axis=1))
            l = jnp.sum(e, axis=1, keepdims=True) + jnp.exp2(sk - m)
            return e.astype(BF16), l

        def values(n, e, l):
            p, _ = SWA_CHAINS[n]
            vw = v_ref[k_rows, p * LANES:(p + 1) * LANES]
            return jnp.dot(e, vw, preferred_element_type=F32) / l

        s_buf, e_buf, done = {}, {}, {}
        for t in range(len(SWA_CHAINS) + 2):
            if t < len(SWA_CHAINS):
                s_buf[t] = scores(t)
            if 1 <= t <= len(SWA_CHAINS):
                e_buf[t - 1] = softmax(t - 1, s_buf.pop(t - 1))
            if t >= 2:
                p, side = SWA_CHAINS[t - 2]
                done[side] = values(t - 2, *e_buf.pop(t - 2))
                if side == 1:
                    o_both = jnp.where(lo, done.pop(0), done.pop(1)).astype(BF16)
                    for j in range(SWA_GROUP):
                        tile = p * SWA_GROUP + j
                        o_ref[q_rows, tile * LANES:(tile + 1) * LANES] = o_both[j * WINDOW:(j + 1) * WINDOW, :]

    block(slice(0, WINDOW), slice(0, WINDOW), True)

    def body(n, carry):
        q0 = pl.multiple_of(n * WINDOW, WINDOW)
        k0 = pl.multiple_of((n - 1) * WINDOW, WINDOW)
        block(pl.ds(q0, WINDOW), pl.ds(k0, 2 * WINDOW), False)
        return carry

    lax.fori_loop(1, nblk, body, 0, unroll=SWA_UNROLL)


def _swa(sinks, q, k, v, batch, seq):
    return pl.pallas_call(
        _swa_kernel,
        grid=(batch,),
        in_specs=[pl.BlockSpec(memory_space=pltpu.SMEM),
                  pl.BlockSpec((seq, D_SWA_Q), lambda b: (b, 0)),
                  pl.BlockSpec((seq, D_SWA_KV), lambda b: (b, 0)),
                  pl.BlockSpec((seq, D_SWA_KV), lambda b: (b, 0))],
        out_specs=pl.BlockSpec((seq, D_SWA_Q), lambda b: (b, 0)),
        out_shape=jax.ShapeDtypeStruct((batch * seq, D_SWA_Q), BF16),
        scratch_shapes=[pltpu.VMEM((SWA_ROWS, 2 * WINDOW), F32),
                        pltpu.VMEM((len(SWA_CHAINS), SWA_ROWS, LANES), F32)],
        compiler_params=pltpu.CompilerParams(dimension_semantics=("parallel",), vmem_limit_bytes=VMEM_LIMIT),
        name="swa_attn",
    )(sinks, q, k, v)


def _outproj_kernel(of_ref, os_ref, h_ref, gf_ref, gs_ref, wf_ref, ws_ref, o_ref, wsp_ref):
    @pl.when(pl.program_id(0) == 0)
    def _():
        for k, h in enumerate(SWA_HEAD_ORDER):
            wsp_ref[k * HEAD_DIM:(k + 1) * HEAD_DIM, :] = ws_ref[h * HEAD_DIM:(h + 1) * HEAD_DIM, :]

    rows = [slice(c * OUTPROJ_ROWS, (c + 1) * OUTPROJ_ROWS) for c in range(OUTPROJ_TM // OUTPROJ_ROWS)]

    def norms(r):
        return (_rms_rows(of_ref[r, :].astype(F32), gf_ref[...]).astype(BF16),
                _rms_rows(os_ref[r, :].astype(F32), gs_ref[...]).astype(BF16))

    ready = norms(rows[0])
    for c, r in enumerate(rows):
        nf, ns = ready
        if c + 1 < len(rows):
            ready = norms(rows[c + 1])
        acc = jnp.dot(nf, wf_ref[...], preferred_element_type=F32)
        acc = acc + jnp.dot(ns, wsp_ref[...], preferred_element_type=F32)
        o_ref[r, :] = h_ref[r, :] + acc


def _outproj(of, os_, h, gf, gs, w):
    t, d = h.shape
    row = lambda n: pl.BlockSpec((OUTPROJ_TM, n), lambda i: (i, 0))
    const = lambda shape: pl.BlockSpec(shape, lambda i: (0, 0))
    half = lambda n: pl.BlockSpec((D_FOX, d), lambda i: (n, 0), pipeline_mode=pl.Buffered(1))
    return pl.pallas_call(
        _outproj_kernel,
        grid=(t // OUTPROJ_TM,),
        in_specs=[row(D_FOX), row(D_SWA_Q), row(d), const((1, D_FOX)), const((1, D_SWA_Q)), half(0), half(1)],
        out_specs=row(d),
        out_shape=jax.ShapeDtypeStruct((t, d), F32),
        scratch_shapes=[pltpu.VMEM((D_SWA_Q, d), BF16)],
        compiler_params=pltpu.CompilerParams(dimension_semantics=("arbitrary",), vmem_limit_bytes=VMEM_LIMIT),
        name="outproj",
    )(of, os_, h, gf, gs, w, w)


def _swa_reorder(a, axis):
    shape = a.shape
    a = a.reshape(shape[:axis] + (2, 2, SWA_GROUP, HEAD_DIM) + shape[axis + 1:])
    a = jnp.swapaxes(a, axis + 1, axis + 2)
    return a.reshape(shape)


def _w_in_head_rows():
    gate0 = 3 * D_FOX
    sq0 = gate0 + N_GATE
    starts = [c for c in range(0, gate0, HEAD_DIM)]
    starts += [sq0 + h * HEAD_DIM for h in SWA_HEAD_ORDER]
    starts += [sq0 + D_SWA_Q + c for c in range(0, 2 * D_SWA_KV, HEAD_DIM)]
    starts += [gate0, gate0]
    assert len(starts) * HEAD_DIM == D_IN_PAD
    return np.asarray(starts, np.int32)


def _w_in_prep_kernel(wt_ref, o_ref):
    starts = [int(s) for s in _w_in_head_rows()]
    row = lax.broadcasted_iota(jnp.int32, (LANES, 1), 0)
    for t in range(D_IN_PAD // LANES):
        pieces = [wt_ref[s:s + HEAD_DIM, :] for s in starts[2 * t:2 * t + 2]]
        x = jnp.concatenate(pieces, axis=0)
        if t == D_IN_PAD // LANES - 1:
            x = jnp.where(row >= N_GATE, 0.0, x)
        o_ref[:, t * LANES:(t + 1) * LANES] = jnp.transpose(x).astype(BF16)


def _w_in_prep(w_in_stack, layer):
    _, d, n = w_in_stack.shape
    wt = jnp.swapaxes(w_in_stack, 1, 2)
    return pl.pallas_call(
        _w_in_prep_kernel,
        grid=(d // WPREP_COLS,),
        in_specs=[pl.BlockSpec((None, n, WPREP_COLS), lambda i: (layer, 0, i))],
        out_specs=pl.BlockSpec((WPREP_COLS, D_IN_PAD), lambda i: (i, 0)),
        out_shape=jax.ShapeDtypeStruct((d, D_IN_PAD), BF16),
        compiler_params=pltpu.CompilerParams(dimension_semantics=("parallel",), vmem_limit_bytes=VMEM_LIMIT),
        name="w_in_prep",
    )(wt)


def _layer(h, pos, invf, p, w_in_stack, layer):
    (norm_ffn1_g, w_gate1, w_up1, w_down1, norm_mix_g, b_forget, fox_q_g, fox_k_g, swa_q_g, swa_k_g,
     sinks, out_fox_g, out_swa_g, w_out, norm_ffn2_g, w_gate2, w_up2, w_down2) = p
    batch_seq, d = h.shape
    batch, seq = pos.shape
    row = lambda v: v.reshape(1, -1).astype(F32)
    tile = lambda v, n: jnp.tile(v.astype(F32), n).reshape(1, -1)

    w_in_r = _w_in_prep(w_in_stack, layer)
    bf_pad = jnp.concatenate([b_forget.astype(F32), jnp.zeros((LANES - N_GATE,), F32)]).reshape(1, LANES)

    h1_head, w_gate1_b, w_up1_b, w_down1_b = _ffn_head(h, row(norm_ffn1_g), w_gate1, w_up1, w_down1)
    h1, (w_down2_b,) = _ffn(
        h, row(norm_ffn1_g), w_gate1_b, w_up1_b, w_down1_b, cast=(w_down2,), head=h1_head)

    qf, kf, vf, qs, ks, vs, lf, w_out_b = _inproj(
        h1, row(norm_mix_g), w_in_r, pos.reshape(batch_seq, 1), invf,
        tile(fox_q_g, FOX_HEADS), tile(fox_k_g, FOX_HEADS), tile(swa_q_g, SWA_Q_HEADS),
        tile(swa_k_g, SWA_KV_HEADS), bf_pad, cast=w_out)

    ak, aq = _cumsum(lf, batch, seq)
    o_fox, (w_gate2_b, w_up2_b) = _fox(qf, kf, vf, aq, ak, batch, seq, cast=(w_gate2, w_up2))
    o_swa = _swa(sinks.astype(F32), qs, ks, vs, batch, seq)

    h2 = _outproj(o_fox, o_swa, h1, row(out_fox_g), row(_swa_reorder(out_swa_g, 0)), w_out_b)
    return _ffn(h2, row(norm_ffn2_g), w_gate2_b, w_up2_b, w_down2_b)[0]


def kernel(x, positions, norm_ffn1_g, ffn1_w_gate, ffn1_w_up, ffn1_w_down, norm_mix_g, w_in, b_forget, fox_q_norm_g, fox_k_norm_g, swa_q_norm_g, swa_k_norm_g, swa_sinks, out_norm_fox_g, out_norm_swa_g, w_out, norm_ffn2_g, ffn2_w_gate, ffn2_w_up, ffn2_w_down):
    batch, seq, d = x.shape
    half = jnp.arange(0, HEAD_DIM, 2, dtype=F32)
    inv_freq = ROPE_THETA ** (-half / HEAD_DIM)
    invf = jnp.tile(inv_freq, LANES // (HEAD_DIM // 2)).reshape(1, LANES)
    stacks = (norm_ffn1_g, ffn1_w_gate, ffn1_w_up, ffn1_w_down, norm_mix_g, b_forget, fox_q_norm_g,
              fox_k_norm_g, swa_q_norm_g, swa_k_norm_g, swa_sinks, out_norm_fox_g, out_norm_swa_g, w_out,
              norm_ffn2_g, ffn2_w_gate, ffn2_w_up, ffn2_w_down)
    h = x.reshape(batch * seq, d)
    for layer in range(norm_ffn1_g.shape[0]):
        h = _layer(h, positions, invf, tuple(s[layer] for s in stacks), w_in, layer)
    return h.reshape(batch, seq, d)
```

```python
import functools

import jax
import jax.numpy as jnp
import numpy as np
from jax import lax
from jax.experimental import pallas as pl
from jax.experimental.pallas import tpu as pltpu

F32 = jnp.float32
BF16 = jnp.bfloat16

HEAD_DIM = 64
FOX_HEADS = 16
SWA_Q_HEADS = 16
SWA_KV_HEADS = 4
SWA_GROUP = SWA_Q_HEADS // SWA_KV_HEADS
WINDOW = 128
ROPE_THETA = 10000.0
EPS = 1e-6

D_FOX = FOX_HEADS * HEAD_DIM
D_SWA_Q = SWA_Q_HEADS * HEAD_DIM
D_SWA_KV = SWA_KV_HEADS * HEAD_DIM
N_GATE = FOX_HEADS

LANES = 128
MXU_TILE = 256
VMEM_LIMIT = 60 * 1024 * 1024

C_FQ = 0
C_FK = C_FQ + D_FOX
C_FV = C_FK + D_FOX
C_SQ = C_FV + D_FOX
C_SK = C_SQ + D_SWA_Q
C_SV = C_SK + D_SWA_KV
C_GATE = C_SV + D_SWA_KV
D_IN_PAD = C_GATE + LANES

SWA_HEAD_ORDER = tuple(8 * p + 4 * side + j for p in range(2) for j in range(SWA_GROUP) for side in range(2))

FFN_TM = 1024
FFN_TF = 512
HEAD_TF = 256
HEAD_CHUNKS = 8
WPREP_COLS = 512
INPROJ_TM = 512
INPROJ_TN = 512
INPROJ_EPI_ROWS = 256
INPROJ_DEPTH = 1
OUTPROJ_TM = 512
OUTPROJ_ROWS = 256
FOX_TQ = 256
FOX_TILES = 2
FOX_LAG_S, FOX_LAG_V = 1, 2
CUM_BLK = 256

LOG2E = 1.4426950408889634

AUG_PARTS = 3
ONE_LANE = AUG_PARTS * N_GATE


def _rms_rows(x, g):
    return x * lax.rsqrt(jnp.mean(x * x, axis=-1, keepdims=True) + EPS) * g


def _swiglu_tile(u, wg, wu, wd):
    gate = jnp.dot(u, wg, preferred_element_type=F32)
    up = jnp.dot(u, wu, preferred_element_type=F32)
    a = (gate * jax.nn.sigmoid(gate) * up * 0.5).astype(BF16)
    return jnp.dot(a, wd, preferred_element_type=F32)


def _ffn_first(x_ref, g_ref, u_ref, wg, wu, wd, o_ref):
    x = x_ref[...]
    u = _rms_rows(x, g_ref[...]).astype(BF16)
    u_ref[...] = u
    o_ref[...] = x + _swiglu_tile(u, wg, wu, wd)


def _ffn_step(u_ref, wg, wu, wd, o_ref):
    o_ref[...] += _swiglu_tile(u_ref[...], wg, wu, wd)


def _ffn_kernel(x_ref, g_ref, wg_ref, wu_ref, wd_ref, *refs, n_cast, has_head):
    head_ref = refs[0] if has_head else None
    refs = refs[1:] if has_head else refs
    cast_in, o_ref, cast_out, u_ref = refs[:n_cast], refs[n_cast], refs[n_cast + 1:-1], refs[-1]
    i, j = pl.program_id(0), pl.program_id(1)

    def side_job():
        for src, dst in zip(cast_in, cast_out):
            dst[...] = src[...].astype(BF16)

    if has_head:
        @pl.when(i == 0)
        def _():
            @pl.when(j < HEAD_CHUNKS)
            def _():
                rows = FFN_TM // HEAD_CHUNKS
                o_ref[pl.ds(pl.multiple_of(j * rows, rows), rows), :] = head_ref[...]

            side_job()

        computing = i > 0
    else:
        computing = True

    @pl.when(computing & (j == 0))
    def _():
        _ffn_first(x_ref, g_ref, u_ref, wg_ref[...], wu_ref[...], wd_ref[...], o_ref)
        side_job()

    @pl.when(computing & (j > 0))
    def _():
        _ffn_step(u_ref, wg_ref[...], wu_ref[...], wd_ref[...], o_ref)
        side_job()


def _ffn(x, g, wg, wu, wd, cast=(), head=None):
    t, d = x.shape
    f = wg.shape[1]
    ni, nj = t // FFN_TM, f // FFN_TF
    has_head = head is not None
    live = (lambda i: jnp.minimum(i, 1)) if has_head else (lambda i: 1)
    row = (lambda i: jnp.maximum(i, 1)) if has_head else (lambda i: i)
    cast_specs = []
    for w in cast:
        r, c = w.shape
        if r % ni == 0 and c % nj == 0 and (c // nj) % LANES == 0:
            cast_specs.append(pl.BlockSpec((r // ni, c // nj), lambda i, j: (i, j)))
        elif r % nj == 0 and c % ni == 0 and (c // ni) % LANES == 0:
            cast_specs.append(pl.BlockSpec((r // nj, c // ni), lambda i, j: (j, i)))
        else:
            assert r % ni == 0, w.shape
            cast_specs.append(pl.BlockSpec((r // ni, c), lambda i, j: (i, 0)))
    head_specs, head_args = [], []
    if has_head:
        last = HEAD_CHUNKS - 1
        head_specs = [pl.BlockSpec((FFN_TM // HEAD_CHUNKS, d),
                                   lambda i, j: (jnp.where(i == 0, jnp.minimum(j, last), last), 0))]
        head_args = [head]
    outs = pl.pallas_call(
        functools.partial(_ffn_kernel, n_cast=len(cast), has_head=has_head),
        grid=(ni, nj),
        in_specs=[
            pl.BlockSpec((FFN_TM, d), lambda i, j: (row(i), 0)),
            pl.BlockSpec((1, d), lambda i, j: (0, 0)),
            pl.BlockSpec((d, FFN_TF), lambda i, j: (0, j * live(i))),
            pl.BlockSpec((d, FFN_TF), lambda i, j: (0, j * live(i))),
            pl.BlockSpec((FFN_TF, d), lambda i, j: (j * live(i), 0)),
        ] + head_specs + cast_specs,
        out_specs=[pl.BlockSpec((FFN_TM, d), lambda i, j: (i, 0))] + cast_specs,
        out_shape=[jax.ShapeDtypeStruct((t, d), F32)] + [jax.ShapeDtypeStruct(w.shape, BF16) for w in cast],
        scratch_shapes=[pltpu.VMEM((FFN_TM, d), BF16)],
        compiler_params=pltpu.CompilerParams(
            dimension_semantics=("parallel", "arbitrary"), vmem_limit_bytes=VMEM_LIMIT),
        name="ffn_cast" if cast else "ffn",
    )(x, g, wg, wu, wd, *head_args, *cast)
    return outs[0], tuple(outs[1:])


def _ffn_head_kernel(x_ref, g_ref, wg_ref, wu_ref, wd_ref, o_ref, wgb_ref, wub_ref, wdb_ref, u_ref):
    wgb_ref[...] = wg_ref[...].astype(BF16)
    wub_ref[...] = wu_ref[...].astype(BF16)
    wdb_ref[...] = wd_ref[...].astype(BF16)

    @pl.when(pl.program_id(0) == 0)
    def _():
        _ffn_first(x_ref, g_ref, u_ref, wgb_ref[...], wub_ref[...], wdb_ref[...], o_ref)

    @pl.when(pl.program_id(0) > 0)
    def _():
        _ffn_step(u_ref, wgb_ref[...], wub_ref[...], wdb_ref[...], o_ref)


def _ffn_head(x, g, wg, wu, wd):
    t, d = x.shape
    f = wg.shape[1]
    once = lambda shape: pl.BlockSpec(shape, lambda j: (0, 0), pipeline_mode=pl.Buffered(1))
    col = pl.BlockSpec((d, HEAD_TF), lambda j: (0, j))
    rowb = pl.BlockSpec((HEAD_TF, d), lambda j: (j, 0))
    return pl.pallas_call(
        _ffn_head_kernel,
        grid=(f // HEAD_TF,),
        in_specs=[once((FFN_TM, d)), once((1, d)), col, col, rowb],
        out_specs=[pl.BlockSpec((FFN_TM, d), lambda j: (0, 0)), col, col, rowb],
        out_shape=[jax.ShapeDtypeStruct((FFN_TM, d), F32), jax.ShapeDtypeStruct(wg.shape, BF16),
                   jax.ShapeDtypeStruct(wu.shape, BF16), jax.ShapeDtypeStruct(wd.shape, BF16)],
        scratch_shapes=[pltpu.VMEM((FFN_TM, d), BF16)],
        compiler_params=pltpu.CompilerParams(dimension_semantics=("arbitrary",), vmem_limit_bytes=VMEM_LIMIT),
        name="ffn_head",
    )(x, g, wg, wu, wd)


def _head_rms(y, gain, bd):
    n = y.shape[1]
    y2 = (y * y).astype(BF16)
    parts = [jnp.dot(y2[:, c * MXU_TILE:(c + 1) * MXU_TILE], bd, preferred_element_type=F32)
             for c in range(n // MXU_TILE)]
    ss = parts[0] if len(parts) == 1 else jnp.concatenate(parts, axis=1)
    return y * lax.rsqrt(ss * (1.0 / HEAD_DIM) + EPS) * gain


def _rope(y, cos, sin_signed, first_half):
    outs = []
    for c in range(y.shape[1] // LANES):
        yc = y[:, c * LANES:(c + 1) * LANES]
        partner = jnp.where(first_half, pltpu.roll(yc, LANES - HEAD_DIM // 2, 1),
                            pltpu.roll(yc, HEAD_DIM // 2, 1))
        outs.append(yc * cos + partner * sin_signed)
    return outs[0] if len(outs) == 1 else jnp.concatenate(outs, axis=1)


def _inproj_kernel(h_ref, g_ref, w_ref, pos_ref, invf_ref, gfq_ref, gfk_ref, gsq_ref, gsk_ref, bf_ref, cast_ref,
                   qf_ref, kf_ref, vf_ref, qs_ref, ks_ref, vs_ref, lf_ref, cast_out_ref):
    u = _rms_rows(h_ref[...], g_ref[...]).astype(BF16)
    cast_out_ref[...] = cast_ref[...].astype(BF16)

    lane = lax.broadcasted_iota(jnp.int32, (1, LANES), 1)
    first_half = (lane % HEAD_DIM) < (HEAD_DIM // 2)
    rope_tables = []

    def rope_table():
        if not rope_tables:
            ang = pos_ref[...].astype(F32) * invf_ref[...]
            sin = jnp.sin(ang)
            rope_tables.append((jnp.cos(ang), jnp.where(first_half, -sin, sin)))
        return rope_tables[0]

    r = lax.broadcasted_iota(jnp.int32, (MXU_TILE, MXU_TILE), 0) // HEAD_DIM
    c = lax.broadcasted_iota(jnp.int32, (MXU_TILE, MXU_TILE), 1) // HEAD_DIM
    bd = jnp.where(r == c, 1.0, 0.0).astype(BF16)

    scale = HEAD_DIM ** -0.5

    def fox_q(y, lo, hi, r):
        qf_ref[r, lo:hi] = (_head_rms(y, gfq_ref[:, lo:hi], bd) * (scale * LOG2E)).astype(BF16)

    def fox_k(y, lo, hi, r):
        kf_ref[r, lo:hi] = _head_rms(y, gfk_ref[:, lo:hi], bd).astype(BF16)

    def fox_v(y, lo, hi, r):
        vf_ref[r, lo:hi] = y.astype(BF16)
        rope_table()

    def swa_q(y, lo, hi, r):
        y = _rope(_head_rms(y, gsq_ref[:, lo:hi], bd), *(t[r] for t in rope_table()), first_half)
        qs_ref[r, lo:hi] = (y * (scale * LOG2E)).astype(BF16)

    def swa_k(y, lo, hi, r):
        ks_ref[r, lo:hi] = _rope(_head_rms(y, gsk_ref[:, lo:hi], bd), *(t[r] for t in rope_table()),
                                 first_half).astype(BF16)

    def swa_v(y, lo, hi, r):
        vs_ref[r, lo:hi] = y.astype(BF16)

    def gate(y, lo, hi, r):
        z = y + bf_ref[...]
        lf_ref[r, :] = jnp.minimum(z, 0.0) - jnp.log1p(jnp.exp(-jnp.abs(z)))

    stages = []
    for start, end, epilogue in ((C_FQ, C_FK, fox_q), (C_FK, C_FV, fox_k), (C_FV, C_SQ, fox_v),
                                 (C_SQ, C_SK, swa_q), (C_SK, C_SV, swa_k), (C_SV, C_GATE, swa_v),
                                 (C_GATE, D_IN_PAD, gate)):
        for lo in range(start, end, INPROJ_TN):
            hi = min(lo + INPROJ_TN, end)
            stages.append((lo, hi, lo - start, hi - start, epilogue))
    halves = [slice(c * INPROJ_EPI_ROWS, (c + 1) * INPROJ_EPI_ROWS) for c in range(INPROJ_TM // INPROJ_EPI_ROWS)]

    def flush(fn, y, olo, ohi):
        for r in halves:
            fn(y[r, :], olo, ohi, r)

    pending = []
    for lo, hi, olo, ohi, epilogue in stages:
        y = jnp.dot(u, w_ref[:, lo:hi], preferred_element_type=F32)
        if len(pending) >= INPROJ_DEPTH:
            flush(*pending.pop(0))
        pending.append((epilogue, y, olo, ohi))
    for item in pending:
        flush(*item)


def _inproj(h, g, w, pos, invf, gfq, gfk, gsq, gsk, bf, cast):
    t, d = h.shape
    steps = t // INPROJ_TM
    row = lambda n: pl.BlockSpec((INPROJ_TM, n), lambda i: (i, 0))
    const = lambda shape: pl.BlockSpec(shape, lambda i: (0, 0))
    cast_spec = pl.BlockSpec((cast.shape[0] // steps, cast.shape[1]), lambda i: (i, 0))
    return pl.pallas_call(
        _inproj_kernel,
        grid=(steps,),
        in_specs=[
            row(d),
            const((1, d)),
            pl.BlockSpec((d, D_IN_PAD), lambda i: (0, 0), pipeline_mode=pl.Buffered(1)),
            row(1),
            const((1, LANES)),
            const((1, D_FOX)), const((1, D_FOX)), const((1, D_SWA_Q)), const((1, D_SWA_KV)),
            const((1, LANES)),
            cast_spec,
        ],
        out_specs=[row(D_FOX), row(D_FOX), row(D_FOX), row(D_SWA_Q), row(D_SWA_KV), row(D_SWA_KV), row(LANES),
                   cast_spec],
        out_shape=[
            jax.ShapeDtypeStruct((t, D_FOX), BF16), jax.ShapeDtypeStruct((t, D_FOX), BF16),
            jax.ShapeDtypeStruct((t, D_FOX), BF16), jax.ShapeDtypeStruct((t, D_SWA_Q), BF16),
            jax.ShapeDtypeStruct((t, D_SWA_KV), BF16), jax.ShapeDtypeStruct((t, D_SWA_KV), BF16),
            jax.ShapeDtypeStruct((t, LANES), F32), jax.ShapeDtypeStruct(cast.shape, BF16),
        ],
        compiler_params=pltpu.CompilerParams(dimension_semantics=("parallel",), vmem_limit_bytes=VMEM_LIMIT),
        name="inproj",
    )(h, g, w, pos, invf, gfq, gfk, gsq, gsk, bf, cast)


def _split3(x):
    p1 = x.astype(BF16)
    r1 = x - p1.astype(F32)
    p2 = r1.astype(BF16)
    p3 = (r1 - p2.astype(F32)).astype(BF16)
    return p1, p2, p3


def _cumsum_kernel(lf_ref, pmap_ref, ak_ref, aq_ref):
    s = lf_ref.shape[0]
    r = lax.broadcasted_iota(jnp.int32, (CUM_BLK, CUM_BLK), 0)
    c = lax.broadcasted_iota(jnp.int32, (CUM_BLK, CUM_BLK), 1)
    tri = jnp.where(r >= c, 1.0, 0.0).astype(BF16)
    lane = lax.broadcasted_iota(jnp.int32, (1, LANES), 1)
    is_gate = lane < N_GATE
    one = jnp.where(lane == ONE_LANE, 1.0, 0.0)
    local = []
    for b in range(s // CUM_BLK):
        p1, p2, p3 = _split3(lf_ref[b * CUM_BLK:(b + 1) * CUM_BLK, :])
        local.append(jnp.dot(tri, p1, preferred_element_type=F32)
                     + jnp.dot(tri, p2, preferred_element_type=F32)
                     + jnp.dot(tri, p3, preferred_element_type=F32))
    carry = jnp.zeros((1, LANES), F32)
    for b in range(s // CUM_BLK):
        rows = slice(b * CUM_BLK, (b + 1) * CUM_BLK)
        cb = local[b] + carry
        carry = cb[CUM_BLK - 1:CUM_BLK, :]
        terms = [jnp.where(is_gate, t.astype(F32), 0.0) for t in _split3(cb * LOG2E)]
        packed = one + terms[0]
        for i in range(1, AUG_PARTS):
            packed = packed + pltpu.roll(terms[i], i * N_GATE, 1)
        placed = jnp.dot(packed.astype(BF16), pmap_ref[...], preferred_element_type=F32)
        ak_ref[rows, :] = placed[:, :D_FOX].astype(BF16)
        aq_ref[rows, :] = placed[:, D_FOX:].astype(BF16)


def _aug_map():
    pmap = np.zeros((LANES, 2 * D_FOX), np.float32)
    for h in range(FOX_HEADS):
        base = (h // 2) * LANES + (HEAD_DIM if h % 2 == 0 else 0)
        for i in range(AUG_PARTS):
            pmap[i * N_GATE + h, base + i] = -1.0
            pmap[ONE_LANE, D_FOX + base + i] = 1.0
        pmap[ONE_LANE, base + AUG_PARTS] = 1.0
        pmap[h, D_FOX + base + AUG_PARTS] = 1.0
    return jnp.asarray(pmap, BF16)


def _cumsum(lf, batch, seq):
    pmap = _aug_map()
    out = pl.BlockSpec((seq, D_FOX), lambda b: (b, 0))
    return pl.pallas_call(
        _cumsum_kernel,
        grid=(batch,),
        in_specs=[pl.BlockSpec((seq, LANES), lambda b: (b, 0)), pl.BlockSpec(pmap.shape, lambda b: (0, 0))],
        out_specs=[out, out],
        out_shape=[jax.ShapeDtypeStruct((batch * seq, D_FOX), BF16)] * 2,
        compiler_params=pltpu.CompilerParams(dimension_semantics=("parallel",)),
        name="gate_cumsum",
    )(lf, pmap)


def _fox_kernel(q_ref, k_ref, v_ref, aq_ref, ak_ref, o_ref, qa_ref, ka_ref):
    seq = q_ref.shape[0]
    lane = lax.broadcasted_iota(jnp.int32, (1, LANES), 1)
    lo = lane < HEAD_DIM
    for tile in range(FOX_TILES):
        cols = slice(tile * LANES, (tile + 1) * LANES)
        q, k, aq, ak = q_ref[:, cols], k_ref[:, cols], aq_ref[:, cols], ak_ref[:, cols]
        qa_ref[2 * tile] = jnp.where(lo, q, aq)
        qa_ref[2 * tile + 1] = jnp.where(lo, aq, q)
        ka_ref[2 * tile] = jnp.where(lo, k, ak)
        ka_ref[2 * tile + 1] = jnp.where(lo, ak, k)

    qry = lax.broadcasted_iota(jnp.int32, (FOX_TQ, FOX_TQ), 0)
    key = lax.broadcasted_iota(jnp.int32, (FOX_TQ, FOX_TQ), 1)
    causal = key <= qry
    nt = (((1,), (1,)), ((), ()))

    def scores(i, head):
        rows = slice(i * FOX_TQ, (i + 1) * FOX_TQ)
        return lax.dot_general(qa_ref[head, rows, :], ka_ref[head, 0:(i + 1) * FOX_TQ, :], nt,
                               preferred_element_type=F32)

    def softmax(i, s):
        w = (i + 1) * FOX_TQ
        diag = jnp.where(causal, s[:, w - FOX_TQ:], -jnp.inf)
        s = diag if i == 0 else jnp.concatenate([s[:, :w - FOX_TQ], diag], axis=1)
        m = jnp.max(s, axis=1, keepdims=True)
        p = jnp.exp2(s - m)
        return p.astype(BF16), jnp.sum(p, axis=1, keepdims=True)

    def values(i, head, p, l):
        tile = head // 2
        o = jnp.dot(p, v_ref[0:(i + 1) * FOX_TQ, tile * LANES:(tile + 1) * LANES],
                    preferred_element_type=F32)
        return o / l

    chains = [(i, head) for i in reversed(range(seq // FOX_TQ)) for head in range(2 * FOX_TILES)]
    s_buf, p_buf, done = {}, {}, {}
    for t in range(len(chains) + FOX_LAG_V):
        if t < len(chains):
            s_buf[t] = scores(*chains[t])
        if FOX_LAG_S <= t < len(chains) + FOX_LAG_S:
            p_buf[t - FOX_LAG_S] = softmax(chains[t - FOX_LAG_S][0], s_buf.pop(t - FOX_LAG_S))
        if t >= FOX_LAG_V:
            i, head = chains[t - FOX_LAG_V]
            done[(i, head)] = values(i, head, *p_buf.pop(t - FOX_LAG_V))
            if head % 2 == 1:
                rows = slice(i * FOX_TQ, (i + 1) * FOX_TQ)
                cols = slice((head // 2) * LANES, (head // 2 + 1) * LANES)
                o_ref[rows, cols] = jnp.where(lo, done.pop((i, head - 1)), done.pop((i, head))).astype(BF16)


def _fox(q, k, v, aq, ak, batch, seq):
    width = FOX_TILES * LANES
    blk = pl.BlockSpec((seq, width), lambda b, p: (b, p))
    return pl.pallas_call(
        _fox_kernel,
        grid=(batch, D_FOX // width),
        in_specs=[blk] * 5,
        out_specs=blk,
        out_shape=jax.ShapeDtypeStruct((batch * seq, D_FOX), BF16),
        scratch_shapes=[pltpu.VMEM((2 * FOX_TILES, seq, LANES), BF16),
                        pltpu.VMEM((2 * FOX_TILES, seq, LANES), BF16)],
        compiler_params=pltpu.CompilerParams(
            dimension_semantics=("parallel", "arbitrary"), vmem_limit_bytes=VMEM_LIMIT),
        name="fox_attn",
    )(q, k, v, aq, ak)


SWA_ROWS = SWA_GROUP * WINDOW
SWA_UNROLL = 15
SWA_CHAINS = tuple((p, side) for p in range(D_SWA_KV // LANES) for side in range(2))


def _swa_kernel(sink_ref, q_ref, k_ref, v_ref, o_ref, bias_ref, sink_col_ref):
    seq = q_ref.shape[0]
    nblk = seq // WINDOW
    lane = lax.broadcasted_iota(jnp.int32, (1, LANES), 1)
    lo = lane < HEAD_DIM
    nt = (((1,), (1,)), ((), ()))

    r = lax.broadcasted_iota(jnp.int32, (SWA_ROWS, 2 * WINDOW), 0) % WINDOW
    c = lax.broadcasted_iota(jnp.int32, (SWA_ROWS, 2 * WINDOW), 1)
    bias_ref[...] = jnp.where((c > r) & (c <= r + WINDOW), 0.0, -jnp.inf)
    head = lax.broadcasted_iota(jnp.int32, (SWA_ROWS, LANES), 0) // WINDOW
    for n, (p, side) in enumerate(SWA_CHAINS):
        col = jnp.zeros((SWA_ROWS, LANES), F32)
        for j in range(SWA_GROUP):
            col = jnp.where(head == j, sink_ref[8 * p + 4 * side + j] * LOG2E, col)
        sink_col_ref[n] = col

    def block(q_rows, k_rows, first):
        cols = slice(WINDOW, 2 * WINDOW) if first else slice(0, 2 * WINDOW)

        def scores(n):
            p, side = SWA_CHAINS[n]
            tiles = [q_ref[q_rows, (p * SWA_GROUP + j) * LANES:(p * SWA_GROUP + j + 1) * LANES]
                     for j in range(SWA_GROUP)]
            qst = jnp.concatenate(tiles, axis=0)
            qm = jnp.where(lo == (side == 0), qst, jnp.zeros_like(qst))
            kw = k_ref[k_rows, p * LANES:(p + 1) * LANES]
            return lax.dot_general(qm, kw, nt, preferred_element_type=F32)

        def softmax(n, s):
            s = s + bias_ref[:, cols]
            sk = sink_col_ref[n]
            m = jnp.maximum(jnp.max(s, axis=1, keepdims=True), sk)
            e = jnp.exp2(s - jnp.concatenate([m] * (s.shape[1] // LANES), axis=1))
            l = jnp.sum(e, axis=1, keepdims=True) + jnp.exp2(sk - m)
            return e.astype(BF16), l

        def values(n, e, l):
            p, _ = SWA_CHAINS[n]
            vw = v_ref[k_rows, p * LANES:(p + 1) * LANES]
            return jnp.dot(e, vw, preferred_element_type=F32) / l

        s_buf, e_buf, done = {}, {}, {}
        for t in range(len(SWA_CHAINS) + 2):
            if t < len(SWA_CHAINS):
                s_buf[t] = scores(t)
            if 1 <= t <= len(SWA_CHAINS):
                e_buf[t - 1] = softmax(t - 1, s_buf.pop(t - 1))
            if t >= 2:
                p, side = SWA_CHAINS[t - 2]
                done[side] = values(t - 2, *e_buf.pop(t - 2))
                if side == 1:
                    o_both = jnp.where(lo, done.pop(0), done.pop(1)).astype(BF16)
                    for j in range(SWA_GROUP):
                        tile = p * SWA_GROUP + j
                        o_ref[q_rows, tile * LANES:(tile + 1) * LANES] = o_both[j * WINDOW:(j + 1) * WINDOW, :]

    block(slice(0, WINDOW), slice(0, WINDOW), True)

    def body(n, carry):
        q0 = pl.multiple_of(n * WINDOW, WINDOW)
        k0 = pl.multiple_of((n - 1) * WINDOW, WINDOW)
        block(pl.ds(q0, WINDOW), pl.ds(k0, 2 * WINDOW), False)
        return carry

    lax.fori_loop(1, nblk, body, 0, unroll=SWA_UNROLL)


def _swa(sinks, q, k, v, batch, seq):
    return pl.pallas_call(
        _swa_kernel,
        grid=(batch,),
        in_specs=[pl.BlockSpec(memory_space=pltpu.SMEM),
                  pl.BlockSpec((seq, D_SWA_Q), lambda b: (b, 0)),
                  pl.BlockSpec((seq, D_SWA_KV), lambda b: (b, 0)),
                  pl.BlockSpec((seq, D_SWA_KV), lambda b: (b, 0))],
        out_specs=pl.BlockSpec((seq, D_SWA_Q), lambda b: (b, 0)),
        out_shape=jax.ShapeDtypeStruct((batch * seq, D_SWA_Q), BF16),
        scratch_shapes=[pltpu.VMEM((SWA_ROWS, 2 * WINDOW), F32),
                        pltpu.VMEM((len(SWA_CHAINS), SWA_ROWS, LANES), F32)],
        compiler_params=pltpu.CompilerParams(dimension_semantics=("parallel",), vmem_limit_bytes=VMEM_LIMIT),
        name="swa_attn",
    )(sinks, q, k, v)


def _outproj_kernel(of_ref, os_ref, h_ref, gf_ref, gs_ref, wf_ref, ws_ref, o_ref, wsp_ref):
    @pl.when(pl.program_id(0) == 0)
    def _():
        for k, h in enumerate(SWA_HEAD_ORDER):
            wsp_ref[k * HEAD_DIM:(k + 1) * HEAD_DIM, :] = ws_ref[h * HEAD_DIM:(h + 1) * HEAD_DIM, :]

    rows = [slice(c * OUTPROJ_ROWS, (c + 1) * OUTPROJ_ROWS) for c in range(OUTPROJ_TM // OUTPROJ_ROWS)]

    def norms(r):
        return (_rms_rows(of_ref[r, :].astype(F32), gf_ref[...]).astype(BF16),
                _rms_rows(os_ref[r, :].astype(F32), gs_ref[...]).astype(BF16))

    ready = norms(rows[0])
    for c, r in enumerate(rows):
        nf, ns = ready
        if c + 1 < len(rows):
            ready = norms(rows[c + 1])
        acc = jnp.dot(nf, wf_ref[...], preferred_element_type=F32)
        acc = acc + jnp.dot(ns, wsp_ref[...], preferred_element_type=F32)
        o_ref[r, :] = h_ref[r, :] + acc


def _outproj(of, os_, h, gf, gs, w):
    t, d = h.shape
    row = lambda n: pl.BlockSpec((OUTPROJ_TM, n), lambda i: (i, 0))
    const = lambda shape: pl.BlockSpec(shape, lambda i: (0, 0))
    half = lambda n: pl.BlockSpec((D_FOX, d), lambda i: (n, 0), pipeline_mode=pl.Buffered(1))
    return pl.pallas_call(
        _outproj_kernel,
        grid=(t // OUTPROJ_TM,),
        in_specs=[row(D_FOX), row(D_SWA_Q), row(d), const((1, D_FOX)), const((1, D_SWA_Q)), half(0), half(1)],
        out_specs=row(d),
        out_shape=jax.ShapeDtypeStruct((t, d), F32),
        scratch_shapes=[pltpu.VMEM((D_SWA_Q, d), BF16)],
        compiler_params=pltpu.CompilerParams(dimension_semantics=("arbitrary",), vmem_limit_bytes=VMEM_LIMIT),
        name="outproj",
    )(of, os_, h, gf, gs, w, w)


def _swa_reorder(a, axis):
    shape = a.shape
    a = a.reshape(shape[:axis] + (2, 2, SWA_GROUP, HEAD_DIM) + shape[axis + 1:])
    a = jnp.swapaxes(a, axis + 1, axis + 2)
    return a.reshape(shape)


def _w_in_head_rows():
    gate0 = 3 * D_FOX
    sq0 = gate0 + N_GATE
    starts = [c for c in range(0, gate0, HEAD_DIM)]
    starts += [sq0 + h * HEAD_DIM for h in SWA_HEAD_ORDER]
    starts += [sq0 + D_SWA_Q + c for c in range(0, 2 * D_SWA_KV, HEAD_DIM)]
    starts += [gate0, gate0]
    assert len(starts) * HEAD_DIM == D_IN_PAD
    return np.asarray(starts, np.int32)


def _w_in_prep_kernel(wt_ref, o_ref):
    starts = [int(s) for s in _w_in_head_rows()]
    row = lax.broadcasted_iota(jnp.int32, (LANES, 1), 0)
    for t in range(D_IN_PAD // LANES):
        pieces = [wt_ref[s:s + HEAD_DIM, :] for s in starts[2 * t:2 * t + 2]]
        x = jnp.concatenate(pieces, axis=0)
        if t == D_IN_PAD // LANES - 1:
            x = jnp.where(row >= N_GATE, 0.0, x)
        o_ref[:, t * LANES:(t + 1) * LANES] = jnp.transpose(x).astype(BF16)


def _w_in_prep(w_in_stack, layer):
    _, d, n = w_in_stack.shape
    wt = jnp.swapaxes(w_in_stack, 1, 2)
    return pl.pallas_call(
        _w_in_prep_kernel,
        grid=(d // WPREP_COLS,),
        in_specs=[pl.BlockSpec((None, n, WPREP_COLS), lambda i: (layer, 0, i))],
        out_specs=pl.BlockSpec((WPREP_COLS, D_IN_PAD), lambda i: (i, 0)),
        out_shape=jax.ShapeDtypeStruct((d, D_IN_PAD), BF16),
        compiler_params=pltpu.CompilerParams(dimension_semantics=("parallel",), vmem_limit_bytes=VMEM_LIMIT),
        name="w_in_prep",
    )(wt)


def _layer(h, pos, invf, p, w_in_stack, layer):
    (norm_ffn1_g, w_gate1, w_up1, w_down1, norm_mix_g, b_forget, fox_q_g, fox_k_g, swa_q_g, swa_k_g,
     sinks, out_fox_g, out_swa_g, w_out, norm_ffn2_g, w_gate2, w_up2, w_down2) = p
    batch_seq, d = h.shape
    batch, seq = pos.shape
    row = lambda v: v.reshape(1, -1).astype(F32)
    tile = lambda v, n: jnp.tile(v.astype(F32), n).reshape(1, -1)

    w_in_r = _w_in_prep(w_in_stack, layer)
    bf_pad = jnp.concatenate([b_forget.astype(F32), jnp.zeros((LANES - N_GATE,), F32)]).reshape(1, LANES)

    h1_head, w_gate1_b, w_up1_b, w_down1_b = _ffn_head(h, row(norm_ffn1_g), w_gate1, w_up1, w_down1)
    h1, (w_gate2_b, w_up2_b, w_down2_b) = _ffn(
        h, row(norm_ffn1_g), w_gate1_b, w_up1_b, w_down1_b, cast=(w_gate2, w_up2, w_down2), head=h1_head)

    qf, kf, vf, qs, ks, vs, lf, w_out_b = _inproj(
        h1, row(norm_mix_g), w_in_r, pos.reshape(batch_seq, 1), invf,
        tile(fox_q_g, FOX_HEADS), tile(fox_k_g, FOX_HEADS), tile(swa_q_g, SWA_Q_HEADS),
        tile(swa_k_g, SWA_KV_HEADS), bf_pad, cast=w_out)

    ak, aq = _cumsum(lf, batch, seq)
    o_fox = _fox(qf, kf, vf, aq, ak, batch, seq)
    o_swa = _swa(sinks.astype(F32), qs, ks, vs, batch, seq)

    h2 = _outproj(o_fox, o_swa, h1, row(out_fox_g), row(_swa_reorder(out_swa_g, 0)), w_out_b)
    return _ffn(h2, row(norm_ffn2_g), w_gate2_b, w_up2_b, w_down2_b)[0]


def kernel(x, positions, norm_ffn1_g, ffn1_w_gate, ffn1_w_up, ffn1_w_down, norm_mix_g, w_in, b_forget, fox_q_norm_g, fox_k_norm_g, swa_q_norm_g, swa_k_norm_g, swa_sinks, out_norm_fox_g, out_norm_swa_g, w_out, norm_ffn2_g, ffn2_w_gate, ffn2_w_up, ffn2_w_down):
    batch, seq, d = x.shape
    half = jnp.arange(0, HEAD_DIM, 2, dtype=F32)
    inv_freq = ROPE_THETA ** (-half / HEAD_DIM)
    invf = jnp.tile(inv_freq, LANES // (HEAD_DIM // 2)).reshape(1, LANES)
    stacks = (norm_ffn1_g, ffn1_w_gate, ffn1_w_up, ffn1_w_down, norm_mix_g, b_forget, fox_q_norm_g,
              fox_k_norm_g, swa_q_norm_g, swa_k_norm_g, swa_sinks, out_norm_fox_g, out_norm_swa_g, w_out,
              norm_ffn2_g, ffn2_w_gate, ffn2_w_up, ffn2_w_down)
    h = x.reshape(batch * seq, d)
    for layer in range(norm_ffn1_g.shape[0]):
        h = _layer(h, positions, invf, tuple(s[layer] for s in stacks), w_in, layer)
    return h.reshape(batch, seq, d)
```

```python
import functools

import jax
import jax.numpy as jnp
import numpy as np
from jax import lax
from jax.experimental import pallas as pl
from jax.experimental.pallas import tpu as pltpu

F32 = jnp.float32
BF16 = jnp.bfloat16

HEAD_DIM = 64
FOX_HEADS = 16
SWA_Q_HEADS = 16
SWA_KV_HEADS = 4
SWA_GROUP = SWA_Q_HEADS // SWA_KV_HEADS
WINDOW = 128
ROPE_THETA = 10000.0
EPS = 1e-6

D_FOX = FOX_HEADS * HEAD_DIM
D_SWA_Q = SWA_Q_HEADS * HEAD_DIM
D_SWA_KV = SWA_KV_HEADS * HEAD_DIM
N_GATE = FOX_HEADS

LANES = 128
MXU_TILE = 256
VMEM_LIMIT = 60 * 1024 * 1024

C_FQ = 0
C_FK = C_FQ + D_FOX
C_FV = C_FK + D_FOX
C_SQ = C_FV + D_FOX
C_SK = C_SQ + D_SWA_Q
C_SV = C_SK + D_SWA_KV
C_GATE = C_SV + D_SWA_KV
D_IN_PAD = C_GATE + LANES

SWA_HEAD_ORDER = tuple(8 * p + 4 * side + j for p in range(2) for j in range(SWA_GROUP) for side in range(2))

FFN_TM = 1024
FFN_TF = 512
HEAD_TF = 256
HEAD_CHUNKS = 8
WPREP_COLS = 512
INPROJ_TM = 512
INPROJ_TN = 512
INPROJ_DEPTH = 1
OUTPROJ_TM = 512
OUTPROJ_ROWS = 256
FOX_TQ = 256
FOX_TILES = 2
FOX_LAG_S, FOX_LAG_V = 1, 2
CUM_BLK = 256

LOG2E = 1.4426950408889634

AUG_PARTS = 3
ONE_LANE = AUG_PARTS * N_GATE


def _rms_rows(x, g):
    return x * lax.rsqrt(jnp.mean(x * x, axis=-1, keepdims=True) + EPS) * g


def _swiglu_tile(u, wg, wu, wd):
    gate = jnp.dot(u, wg, preferred_element_type=F32)
    up = jnp.dot(u, wu, preferred_element_type=F32)
    a = (gate * jax.nn.sigmoid(gate) * up * 0.5).astype(BF16)
    return jnp.dot(a, wd, preferred_element_type=F32)


def _ffn_first(x_ref, g_ref, u_ref, wg, wu, wd, o_ref):
    x = x_ref[...]
    u = _rms_rows(x, g_ref[...]).astype(BF16)
    u_ref[...] = u
    o_ref[...] = x + _swiglu_tile(u, wg, wu, wd)


def _ffn_step(u_ref, wg, wu, wd, o_ref):
    o_ref[...] += _swiglu_tile(u_ref[...], wg, wu, wd)


def _ffn_kernel(x_ref, g_ref, wg_ref, wu_ref, wd_ref, *refs, n_cast, has_head):
    head_ref = refs[0] if has_head else None
    refs = refs[1:] if has_head else refs
    cast_in, o_ref, cast_out, u_ref = refs[:n_cast], refs[n_cast], refs[n_cast + 1:-1], refs[-1]
    i, j = pl.program_id(0), pl.program_id(1)

    def side_job():
        for src, dst in zip(cast_in, cast_out):
            dst[...] = src[...].astype(BF16)

    if has_head:
        @pl.when(i == 0)
        def _():
            @pl.when(j < HEAD_CHUNKS)
            def _():
                rows = FFN_TM // HEAD_CHUNKS
                o_ref[pl.ds(pl.multiple_of(j * rows, rows), rows), :] = head_ref[...]

            side_job()

        computing = i > 0
    else:
        computing = True

    @pl.when(computing & (j == 0))
    def _():
        _ffn_first(x_ref, g_ref, u_ref, wg_ref[...], wu_ref[...], wd_ref[...], o_ref)
        side_job()

    @pl.when(computing & (j > 0))
    def _():
        _ffn_step(u_ref, wg_ref[...], wu_ref[...], wd_ref[...], o_ref)
        side_job()


def _ffn(x, g, wg, wu, wd, cast=(), head=None):
    t, d = x.shape
    f = wg.shape[1]
    ni, nj = t // FFN_TM, f // FFN_TF
    has_head = head is not None
    live = (lambda i: jnp.minimum(i, 1)) if has_head else (lambda i: 1)
    row = (lambda i: jnp.maximum(i, 1)) if has_head else (lambda i: i)
    cast_specs = []
    for w in cast:
        r, c = w.shape
        if r % ni == 0 and c % nj == 0 and (c // nj) % LANES == 0:
            cast_specs.append(pl.BlockSpec((r // ni, c // nj), lambda i, j: (i, j)))
        elif r % nj == 0 and c % ni == 0 and (c // ni) % LANES == 0:
            cast_specs.append(pl.BlockSpec((r // nj, c // ni), lambda i, j: (j, i)))
        else:
            assert r % ni == 0, w.shape
            cast_specs.append(pl.BlockSpec((r // ni, c), lambda i, j: (i, 0)))
    head_specs, head_args = [], []
    if has_head:
        last = HEAD_CHUNKS - 1
        head_specs = [pl.BlockSpec((FFN_TM // HEAD_CHUNKS, d),
                                   lambda i, j: (jnp.where(i == 0, jnp.minimum(j, last), last), 0))]
        head_args = [head]
    outs = pl.pallas_call(
        functools.partial(_ffn_kernel, n_cast=len(cast), has_head=has_head),
        grid=(ni, nj),
        in_specs=[
            pl.BlockSpec((FFN_TM, d), lambda i, j: (row(i), 0)),
            pl.BlockSpec((1, d), lambda i, j: (0, 0)),
            pl.BlockSpec((d, FFN_TF), lambda i, j: (0, j * live(i))),
            pl.BlockSpec((d, FFN_TF), lambda i, j: (0, j * live(i))),
            pl.BlockSpec((FFN_TF, d), lambda i, j: (j * live(i), 0)),
        ] + head_specs + cast_specs,
        out_specs=[pl.BlockSpec((FFN_TM, d), lambda i, j: (i, 0))] + cast_specs,
        out_shape=[jax.ShapeDtypeStruct((t, d), F32)] + [jax.ShapeDtypeStruct(w.shape, BF16) for w in cast],
        scratch_shapes=[pltpu.VMEM((FFN_TM, d), BF16)],
        compiler_params=pltpu.CompilerParams(
            dimension_semantics=("parallel", "arbitrary"), vmem_limit_bytes=VMEM_LIMIT),
        name="ffn_cast" if cast else "ffn",
    )(x, g, wg, wu, wd, *head_args, *cast)
    return outs[0], tuple(outs[1:])


def _ffn_head_kernel(x_ref, g_ref, wg_ref, wu_ref, wd_ref, o_ref, wgb_ref, wub_ref, wdb_ref, u_ref):
    wgb_ref[...] = wg_ref[...].astype(BF16)
    wub_ref[...] = wu_ref[...].astype(BF16)
    wdb_ref[...] = wd_ref[...].astype(BF16)

    @pl.when(pl.program_id(0) == 0)
    def _():
        _ffn_first(x_ref, g_ref, u_ref, wgb_ref[...], wub_ref[...], wdb_ref[...], o_ref)

    @pl.when(pl.program_id(0) > 0)
    def _():
        _ffn_step(u_ref, wgb_ref[...], wub_ref[...], wdb_ref[...], o_ref)


def _ffn_head(x, g, wg, wu, wd):
    t, d = x.shape
    f = wg.shape[1]
    once = lambda shape: pl.BlockSpec(shape, lambda j: (0, 0), pipeline_mode=pl.Buffered(1))
    col = pl.BlockSpec((d, HEAD_TF), lambda j: (0, j))
    rowb = pl.BlockSpec((HEAD_TF, d), lambda j: (j, 0))
    return pl.pallas_call(
        _ffn_head_kernel,
        grid=(f // HEAD_TF,),
        in_specs=[once((FFN_TM, d)), once((1, d)), col, col, rowb],
        out_specs=[pl.BlockSpec((FFN_TM, d), lambda j: (0, 0)), col, col, rowb],
        out_shape=[jax.ShapeDtypeStruct((FFN_TM, d), F32), jax.ShapeDtypeStruct(wg.shape, BF16),
                   jax.ShapeDtypeStruct(wu.shape, BF16), jax.ShapeDtypeStruct(wd.shape, BF16)],
        scratch_shapes=[pltpu.VMEM((FFN_TM, d), BF16)],
        compiler_params=pltpu.CompilerParams(dimension_semantics=("arbitrary",), vmem_limit_bytes=VMEM_LIMIT),
        name="ffn_head",
    )(x, g, wg, wu, wd)


def _head_rms(y, gain, bd):
    n = y.shape[1]
    y2 = (y * y).astype(BF16)
    parts = [jnp.dot(y2[:, c * MXU_TILE:(c + 1) * MXU_TILE], bd, preferred_element_type=F32)
             for c in range(n // MXU_TILE)]
    ss = parts[0] if len(parts) == 1 else jnp.concatenate(parts, axis=1)
    return y * lax.rsqrt(ss * (1.0 / HEAD_DIM) + EPS) * gain


def _rope(y, cos, sin_signed, first_half):
    outs = []
    for c in range(y.shape[1] // LANES):
        yc = y[:, c * LANES:(c + 1) * LANES]
        partner = jnp.where(first_half, pltpu.roll(yc, LANES - HEAD_DIM // 2, 1),
                            pltpu.roll(yc, HEAD_DIM // 2, 1))
        outs.append(yc * cos + partner * sin_signed)
    return outs[0] if len(outs) == 1 else jnp.concatenate(outs, axis=1)


def _inproj_kernel(h_ref, g_ref, w_ref, pos_ref, invf_ref, gfq_ref, gfk_ref, gsq_ref, gsk_ref, bf_ref, cast_ref,
                   qf_ref, kf_ref, vf_ref, qs_ref, ks_ref, vs_ref, lf_ref, cast_out_ref):
    u = _rms_rows(h_ref[...], g_ref[...]).astype(BF16)
    cast_out_ref[...] = cast_ref[...].astype(BF16)

    lane = lax.broadcasted_iota(jnp.int32, (1, LANES), 1)
    first_half = (lane % HEAD_DIM) < (HEAD_DIM // 2)
    rope_tables = []

    def rope_table():
        if not rope_tables:
            pos = jnp.transpose(jnp.broadcast_to(pos_ref[...].astype(F32), (LANES, INPROJ_TM)))
            ang = pos * invf_ref[...]
            sin = jnp.sin(ang)
            rope_tables.append((jnp.cos(ang), jnp.where(first_half, -sin, sin)))
        return rope_tables[0]

    r = lax.broadcasted_iota(jnp.int32, (MXU_TILE, MXU_TILE), 0) // HEAD_DIM
    c = lax.broadcasted_iota(jnp.int32, (MXU_TILE, MXU_TILE), 1) // HEAD_DIM
    bd = jnp.where(r == c, 1.0, 0.0).astype(BF16)

    scale = HEAD_DIM ** -0.5

    def fox_q(y, lo, hi):
        qf_ref[:, lo:hi] = (_head_rms(y, gfq_ref[:, lo:hi], bd) * (scale * LOG2E)).astype(BF16)

    def fox_k(y, lo, hi):
        kf_ref[:, lo:hi] = _head_rms(y, gfk_ref[:, lo:hi], bd).astype(BF16)

    def fox_v(y, lo, hi):
        vf_ref[:, lo:hi] = y.astype(BF16)
        rope_table()

    def swa_q(y, lo, hi):
        y = _rope(_head_rms(y, gsq_ref[:, lo:hi], bd), *rope_table(), first_half)
        qs_ref[:, lo:hi] = (y * (scale * LOG2E)).astype(BF16)

    def swa_k(y, lo, hi):
        ks_ref[:, lo:hi] = _rope(_head_rms(y, gsk_ref[:, lo:hi], bd), *rope_table(), first_half).astype(BF16)

    def swa_v(y, lo, hi):
        vs_ref[:, lo:hi] = y.astype(BF16)

    def gate(y, lo, hi):
        z = y + bf_ref[...]
        lf_ref[...] = jnp.minimum(z, 0.0) - jnp.log1p(jnp.exp(-jnp.abs(z)))

    stages = []
    for start, end, epilogue in ((C_FQ, C_FK, fox_q), (C_FK, C_FV, fox_k), (C_FV, C_SQ, fox_v),
                                 (C_SQ, C_SK, swa_q), (C_SK, C_SV, swa_k), (C_SV, C_GATE, swa_v),
                                 (C_GATE, D_IN_PAD, gate)):
        for lo in range(start, end, INPROJ_TN):
            hi = min(lo + INPROJ_TN, end)
            stages.append((lo, hi, lo - start, hi - start, epilogue))
    pending = []
    for lo, hi, olo, ohi, epilogue in stages:
        y = jnp.dot(u, w_ref[:, lo:hi], preferred_element_type=F32)
        if len(pending) >= INPROJ_DEPTH:
            fn, *args = pending.pop(0)
            fn(*args)
        pending.append((epilogue, y, olo, ohi))
    for fn, *args in pending:
        fn(*args)


def _inproj(h, g, w, pos, invf, gfq, gfk, gsq, gsk, bf, cast):
    t, d = h.shape
    steps = t // INPROJ_TM
    row = lambda n: pl.BlockSpec((INPROJ_TM, n), lambda i: (i, 0))
    const = lambda shape: pl.BlockSpec(shape, lambda i: (0, 0))
    cast_spec = pl.BlockSpec((cast.shape[0] // steps, cast.shape[1]), lambda i: (i, 0))
    return pl.pallas_call(
        _inproj_kernel,
        grid=(steps,),
        in_specs=[
            row(d),
            const((1, d)),
            pl.BlockSpec((d, D_IN_PAD), lambda i: (0, 0), pipeline_mode=pl.Buffered(1)),
            pl.BlockSpec((None, 1, INPROJ_TM), lambda i: (i, 0, 0)),
            const((1, LANES)),
            const((1, D_FOX)), const((1, D_FOX)), const((1, D_SWA_Q)), const((1, D_SWA_KV)),
            const((1, LANES)),
            cast_spec,
        ],
        out_specs=[row(D_FOX), row(D_FOX), row(D_FOX), row(D_SWA_Q), row(D_SWA_KV), row(D_SWA_KV), row(LANES),
                   cast_spec],
        out_shape=[
            jax.ShapeDtypeStruct((t, D_FOX), BF16), jax.ShapeDtypeStruct((t, D_FOX), BF16),
            jax.ShapeDtypeStruct((t, D_FOX), BF16), jax.ShapeDtypeStruct((t, D_SWA_Q), BF16),
            jax.ShapeDtypeStruct((t, D_SWA_KV), BF16), jax.ShapeDtypeStruct((t, D_SWA_KV), BF16),
            jax.ShapeDtypeStruct((t, LANES), F32), jax.ShapeDtypeStruct(cast.shape, BF16),
        ],
        compiler_params=pltpu.CompilerParams(dimension_semantics=("parallel",), vmem_limit_bytes=VMEM_LIMIT),
        name="inproj",
    )(h, g, w, pos, invf, gfq, gfk, gsq, gsk, bf, cast)


def _split3(x):
    p1 = x.astype(BF16)
    r1 = x - p1.astype(F32)
    p2 = r1.astype(BF16)
    p3 = (r1 - p2.astype(F32)).astype(BF16)
    return p1, p2, p3


def _cumsum_kernel(lf_ref, pmap_ref, ak_ref, aq_ref):
    s = lf_ref.shape[0]
    r = lax.broadcasted_iota(jnp.int32, (CUM_BLK, CUM_BLK), 0)
    c = lax.broadcasted_iota(jnp.int32, (CUM_BLK, CUM_BLK), 1)
    tri = jnp.where(r >= c, 1.0, 0.0).astype(BF16)
    lane = lax.broadcasted_iota(jnp.int32, (1, LANES), 1)
    is_gate = lane < N_GATE
    one = jnp.where(lane == ONE_LANE, 1.0, 0.0)
    local = []
    for b in range(s // CUM_BLK):
        p1, p2, p3 = _split3(lf_ref[b * CUM_BLK:(b + 1) * CUM_BLK, :])
        local.append(jnp.dot(tri, p1, preferred_element_type=F32)
                     + jnp.dot(tri, p2, preferred_element_type=F32)
                     + jnp.dot(tri, p3, preferred_element_type=F32))
    carry = jnp.zeros((1, LANES), F32)
    for b in range(s // CUM_BLK):
        rows = slice(b * CUM_BLK, (b + 1) * CUM_BLK)
        cb = local[b] + carry
        carry = cb[CUM_BLK - 1:CUM_BLK, :]
        terms = [jnp.where(is_gate, t.astype(F32), 0.0) for t in _split3(cb * LOG2E)]
        packed = one + terms[0]
        for i in range(1, AUG_PARTS):
            packed = packed + pltpu.roll(terms[i], i * N_GATE, 1)
        placed = jnp.dot(packed.astype(BF16), pmap_ref[...], preferred_element_type=F32)
        ak_ref[rows, :] = placed[:, :D_FOX].astype(BF16)
        aq_ref[rows, :] = placed[:, D_FOX:].astype(BF16)


def _aug_map():
    pmap = np.zeros((LANES, 2 * D_FOX), np.float32)
    for h in range(FOX_HEADS):
        base = (h // 2) * LANES + (HEAD_DIM if h % 2 == 0 else 0)
        for i in range(AUG_PARTS):
            pmap[i * N_GATE + h, base + i] = -1.0
            pmap[ONE_LANE, D_FOX + base + i] = 1.0
        pmap[ONE_LANE, base + AUG_PARTS] = 1.0
        pmap[h, D_FOX + base + AUG_PARTS] = 1.0
    return jnp.asarray(pmap, BF16)


def _cumsum(lf, batch, seq):
    pmap = _aug_map()
    out = pl.BlockSpec((seq, D_FOX), lambda b: (b, 0))
    return pl.pallas_call(
        _cumsum_kernel,
        grid=(batch,),
        in_specs=[pl.BlockSpec((seq, LANES), lambda b: (b, 0)), pl.BlockSpec(pmap.shape, lambda b: (0, 0))],
        out_specs=[out, out],
        out_shape=[jax.ShapeDtypeStruct((batch * seq, D_FOX), BF16)] * 2,
        compiler_params=pltpu.CompilerParams(dimension_semantics=("parallel",)),
        name="gate_cumsum",
    )(lf, pmap)


def _fox_kernel(q_ref, k_ref, v_ref, aq_ref, ak_ref, o_ref, qa_ref, ka_ref):
    seq = q_ref.shape[0]
    lane = lax.broadcasted_iota(jnp.int32, (1, LANES), 1)
    lo = lane < HEAD_DIM
    for tile in range(FOX_TILES):
        cols = slice(tile * LANES, (tile + 1) * LANES)
        q, k, aq, ak = q_ref[:, cols], k_ref[:, cols], aq_ref[:, cols], ak_ref[:, cols]
        qa_ref[2 * tile] = jnp.where(lo, q, aq)
        qa_ref[2 * tile + 1] = jnp.where(lo, aq, q)
        ka_ref[2 * tile] = jnp.where(lo, k, ak)
        ka_ref[2 * tile + 1] = jnp.where(lo, ak, k)

    qry = lax.broadcasted_iota(jnp.int32, (FOX_TQ, FOX_TQ), 0)
    key = lax.broadcasted_iota(jnp.int32, (FOX_TQ, FOX_TQ), 1)
    causal = key <= qry
    nt = (((1,), (1,)), ((), ()))

    def scores(i, head):
        rows = slice(i * FOX_TQ, (i + 1) * FOX_TQ)
        return lax.dot_general(qa_ref[head, rows, :], ka_ref[head, 0:(i + 1) * FOX_TQ, :], nt,
                               preferred_element_type=F32)

    def softmax(i, s):
        w = (i + 1) * FOX_TQ
        diag = jnp.where(causal, s[:, w - FOX_TQ:], -jnp.inf)
        s = diag if i == 0 else jnp.concatenate([s[:, :w - FOX_TQ], diag], axis=1)
        m = jnp.max(s, axis=1, keepdims=True)
        p = jnp.exp2(s - m)
        return p.astype(BF16), jnp.sum(p, axis=1, keepdims=True)

    def values(i, head, p, l):
        tile = head // 2
        o = jnp.dot(p, v_ref[0:(i + 1) * FOX_TQ, tile * LANES:(tile + 1) * LANES],
                    preferred_element_type=F32)
        return o / l

    chains = [(i, head) for i in reversed(range(seq // FOX_TQ)) for head in range(2 * FOX_TILES)]
    s_buf, p_buf, done = {}, {}, {}
    for t in range(len(chains) + FOX_LAG_V):
        if t < len(chains):
            s_buf[t] = scores(*chains[t])
        if FOX_LAG_S <= t < len(chains) + FOX_LAG_S:
            p_buf[t - FOX_LAG_S] = softmax(chains[t - FOX_LAG_S][0], s_buf.pop(t - FOX_LAG_S))
        if t >= FOX_LAG_V:
            i, head = chains[t - FOX_LAG_V]
            done[(i, head)] = values(i, head, *p_buf.pop(t - FOX_LAG_V))
            if head % 2 == 1:
                rows = slice(i * FOX_TQ, (i + 1) * FOX_TQ)
                cols = slice((head // 2) * LANES, (head // 2 + 1) * LANES)
                o_ref[rows, cols] = jnp.where(lo, done.pop((i, head - 1)), done.pop((i, head))).astype(BF16)


def _fox(q, k, v, aq, ak, batch, seq):
    width = FOX_TILES * LANES
    blk = pl.BlockSpec((seq, width), lambda b, p: (b, p))
    return pl.pallas_call(
        _fox_kernel,
        grid=(batch, D_FOX // width),
        in_specs=[blk] * 5,
        out_specs=blk,
        out_shape=jax.ShapeDtypeStruct((batch * seq, D_FOX), BF16),
        scratch_shapes=[pltpu.VMEM((2 * FOX_TILES, seq, LANES), BF16),
                        pltpu.VMEM((2 * FOX_TILES, seq, LANES), BF16)],
        compiler_params=pltpu.CompilerParams(
            dimension_semantics=("parallel", "arbitrary"), vmem_limit_bytes=VMEM_LIMIT),
        name="fox_attn",
    )(q, k, v, aq, ak)


SWA_ROWS = SWA_GROUP * WINDOW
SWA_UNROLL = 15
SWA_CHAINS = tuple((p, side) for p in range(D_SWA_KV // LANES) for side in range(2))


def _swa_kernel(sink_ref, q_ref, k_ref, v_ref, o_ref, bias_ref, sink_col_ref):
    seq = q_ref.shape[0]
    nblk = seq // WINDOW
    lane = lax.broadcasted_iota(jnp.int32, (1, LANES), 1)
    lo = lane < HEAD_DIM
    nt = (((1,), (1,)), ((), ()))

    r = lax.broadcasted_iota(jnp.int32, (SWA_ROWS, 2 * WINDOW), 0) % WINDOW
    c = lax.broadcasted_iota(jnp.int32, (SWA_ROWS, 2 * WINDOW), 1)
    bias_ref[...] = jnp.where((c > r) & (c <= r + WINDOW), 0.0, -jnp.inf)
    head = lax.broadcasted_iota(jnp.int32, (SWA_ROWS, LANES), 0) // WINDOW
    for n, (p, side) in enumerate(SWA_CHAINS):
        col = jnp.zeros((SWA_ROWS, LANES), F32)
        for j in range(SWA_GROUP):
            col = jnp.where(head == j, sink_ref[8 * p + 4 * side + j] * LOG2E, col)
        sink_col_ref[n] = col

    def block(q_rows, k_rows, first):
        cols = slice(WINDOW, 2 * WINDOW) if first else slice(0, 2 * WINDOW)

        def scores(n):
            p, side = SWA_CHAINS[n]
            tiles = [q_ref[q_rows, (p * SWA_GROUP + j) * LANES:(p * SWA_GROUP + j + 1) * LANES]
                     for j in range(SWA_GROUP)]
            qst = jnp.concatenate(tiles, axis=0)
            qm = jnp.where(lo == (side == 0), qst, jnp.zeros_like(qst))
            kw = k_ref[k_rows, p * LANES:(p + 1) * LANES]
            return lax.dot_general(qm, kw, nt, preferred_element_type=F32)

        def softmax(n, s):
            s = s + bias_ref[:, cols]
            sk = sink_col_ref[n]
            m = jnp.maximum(jnp.max(s, axis=1, keepdims=True), sk)
            e = jnp.exp2(s - jnp.concatenate([m] * (s.shape[1] // LANES), axis=1))
            l = jnp.sum(e, axis=1, keepdims=True) + jnp.exp2(sk - m)
            return e.astype(BF16), l

        def values(n, e, l):
            p, _ = SWA_CHAINS[n]
            vw = v_ref[k_rows, p * LANES:(p + 1) * LANES]
            return jnp.dot(e, vw, preferred_element_type=F32) / l

        s_buf, e_buf, done = {}, {}, {}
        for t in range(len(SWA_CHAINS) + 2):
            if t < len(SWA_CHAINS):
                s_buf[t] = scores(t)
            if 1 <= t <= len(SWA_CHAINS):
                e_buf[t - 1] = softmax(t - 1, s_buf.pop(t - 1))
            if t >= 2:
                p, side = SWA_CHAINS[t - 2]
                done[side] = values(t - 2, *e_buf.pop(t - 2))
                if side == 1:
                    o_both = jnp.where(lo, done.pop(0), done.pop(1)).astype(BF16)
                    for j in range(SWA_GROUP):
                        tile = p * SWA_GROUP + j
                        o_ref[q_rows, tile * LANES:(tile + 1) * LANES] = o_both[j * WINDOW:(j + 1) * WINDOW, :]

    block(slice(0, WINDOW), slice(0, WINDOW), True)

    def body(n, carry):
        q0 = pl.multiple_of(n * WINDOW, WINDOW)
        k0 = pl.multiple_of((n - 1) * WINDOW, WINDOW)
        block(pl.ds(q0, WINDOW), pl.ds(k0, 2 * WINDOW), False)
        return carry

    lax.fori_loop(1, nblk, body, 0, unroll=SWA_UNROLL)


def _swa(sinks, q, k, v, batch, seq):
    return pl.pallas_call(
        _swa_kernel,
        grid=(batch,),
        in_specs=[pl.BlockSpec(memory_space=pltpu.SMEM),
                  pl.BlockSpec((seq, D_SWA_Q), lambda b: (b, 0)),
                  pl.BlockSpec((seq, D_SWA_KV), lambda b: (b, 0)),
                  pl.BlockSpec((seq, D_SWA_KV), lambda b: (b, 0))],
        out_specs=pl.BlockSpec((seq, D_SWA_Q), lambda b: (b, 0)),
        out_shape=jax.ShapeDtypeStruct((batch * seq, D_SWA_Q), BF16),
        scratch_shapes=[pltpu.VMEM((SWA_ROWS, 2 * WINDOW), F32),
                        pltpu.VMEM((len(SWA_CHAINS), SWA_ROWS, LANES), F32)],
        compiler_params=pltpu.CompilerParams(dimension_semantics=("parallel",), vmem_limit_bytes=VMEM_LIMIT),
        name="swa_attn",
    )(sinks, q, k, v)


def _outproj_kernel(of_ref, os_ref, h_ref, gf_ref, gs_ref, wf_ref, ws_ref, o_ref, wsp_ref):
    @pl.when(pl.program_id(0) == 0)
    def _():
        for k, h in enumerate(SWA_HEAD_ORDER):
            wsp_ref[k * HEAD_DIM:(k + 1) * HEAD_DIM, :] = ws_ref[h * HEAD_DIM:(h + 1) * HEAD_DIM, :]

    rows = [slice(c * OUTPROJ_ROWS, (c + 1) * OUTPROJ_ROWS) for c in range(OUTPROJ_TM // OUTPROJ_ROWS)]

    def norms(r):
        return (_rms_rows(of_ref[r, :].astype(F32), gf_ref[...]).astype(BF16),
                _rms_rows(os_ref[r, :].astype(F32), gs_ref[...]).astype(BF16))

    ready = norms(rows[0])
    for c, r in enumerate(rows):
        nf, ns = ready
        if c + 1 < len(rows):
            ready = norms(rows[c + 1])
        acc = jnp.dot(nf, wf_ref[...], preferred_element_type=F32)
        acc = acc + jnp.dot(ns, wsp_ref[...], preferred_element_type=F32)
        o_ref[r, :] = h_ref[r, :] + acc


def _outproj(of, os_, h, gf, gs, w):
    t, d = h.shape
    row = lambda n: pl.BlockSpec((OUTPROJ_TM, n), lambda i: (i, 0))
    const = lambda shape: pl.BlockSpec(shape, lambda i: (0, 0))
    half = lambda n: pl.BlockSpec((D_FOX, d), lambda i: (n, 0), pipeline_mode=pl.Buffered(1))
    return pl.pallas_call(
        _outproj_kernel,
        grid=(t // OUTPROJ_TM,),
        in_specs=[row(D_FOX), row(D_SWA_Q), row(d), const((1, D_FOX)), const((1, D_SWA_Q)), half(0), half(1)],
        out_specs=row(d),
        out_shape=jax.ShapeDtypeStruct((t, d), F32),
        scratch_shapes=[pltpu.VMEM((D_SWA_Q, d), BF16)],
        compiler_params=pltpu.CompilerParams(dimension_semantics=("arbitrary",), vmem_limit_bytes=VMEM_LIMIT),
        name="outproj",
    )(of, os_, h, gf, gs, w, w)


def _swa_reorder(a, axis):
    shape = a.shape
    a = a.reshape(shape[:axis] + (2, 2, SWA_GROUP, HEAD_DIM) + shape[axis + 1:])
    a = jnp.swapaxes(a, axis + 1, axis + 2)
    return a.reshape(shape)


def _w_in_head_rows():
    gate0 = 3 * D_FOX
    sq0 = gate0 + N_GATE
    starts = [c for c in range(0, gate0, HEAD_DIM)]
    starts += [sq0 + h * HEAD_DIM for h in SWA_HEAD_ORDER]
    starts += [sq0 + D_SWA_Q + c for c in range(0, 2 * D_SWA_KV, HEAD_DIM)]
    starts += [gate0, gate0]
    assert len(starts) * HEAD_DIM == D_IN_PAD
    return np.asarray(starts, np.int32)


def _w_in_prep_kernel(wt_ref, o_ref):
    starts = [int(s) for s in _w_in_head_rows()]
    row = lax.broadcasted_iota(jnp.int32, (LANES, 1), 0)
    for t in range(D_IN_PAD // LANES):
        pieces = [wt_ref[s:s + HEAD_DIM, :] for s in starts[2 * t:2 * t + 2]]
        x = jnp.concatenate(pieces, axis=0)
        if t == D_IN_PAD // LANES - 1:
            x = jnp.where(row >= N_GATE, 0.0, x)
        o_ref[:, t * LANES:(t + 1) * LANES] = jnp.transpose(x).astype(BF16)


def _w_in_prep(w_in_stack, layer):
    _, d, n = w_in_stack.shape
    wt = jnp.swapaxes(w_in_stack, 1, 2)
    return pl.pallas_call(
        _w_in_prep_kernel,
        grid=(d // WPREP_COLS,),
        in_specs=[pl.BlockSpec((None, n, WPREP_COLS), lambda i: (layer, 0, i))],
        out_specs=pl.BlockSpec((WPREP_COLS, D_IN_PAD), lambda i: (i, 0)),
        out_shape=jax.ShapeDtypeStruct((d, D_IN_PAD), BF16),
        compiler_params=pltpu.CompilerParams(dimension_semantics=("parallel",), vmem_limit_bytes=VMEM_LIMIT),
        name="w_in_prep",
    )(wt)


def _layer(h, pos, invf, p, w_in_stack, layer):
    (norm_ffn1_g, w_gate1, w_up1, w_down1, norm_mix_g, b_forget, fox_q_g, fox_k_g, swa_q_g, swa_k_g,
     sinks, out_fox_g, out_swa_g, w_out, norm_ffn2_g, w_gate2, w_up2, w_down2) = p
    batch_seq, d = h.shape
    batch, seq = pos.shape
    row = lambda v: v.reshape(1, -1).astype(F32)
    tile = lambda v, n: jnp.tile(v.astype(F32), n).reshape(1, -1)

    w_in_r = _w_in_prep(w_in_stack, layer)
    bf_pad = jnp.concatenate([b_forget.astype(F32), jnp.zeros((LANES - N_GATE,), F32)]).reshape(1, LANES)

    h1_head, w_gate1_b, w_up1_b, w_down1_b = _ffn_head(h, row(norm_ffn1_g), w_gate1, w_up1, w_down1)
    h1, (w_gate2_b, w_up2_b, w_down2_b) = _ffn(
        h, row(norm_ffn1_g), w_gate1_b, w_up1_b, w_down1_b, cast=(w_gate2, w_up2, w_down2), head=h1_head)

    qf, kf, vf, qs, ks, vs, lf, w_out_b = _inproj(
        h1, row(norm_mix_g), w_in_r, pos.reshape(batch_seq // INPROJ_TM, 1, INPROJ_TM), invf,
        tile(fox_q_g, FOX_HEADS), tile(fox_k_g, FOX_HEADS), tile(swa_q_g, SWA_Q_HEADS),
        tile(swa_k_g, SWA_KV_HEADS), bf_pad, cast=w_out)

    ak, aq = _cumsum(lf, batch, seq)
    o_fox = _fox(qf, kf, vf, aq, ak, batch, seq)
    o_swa = _swa(sinks.astype(F32), qs, ks, vs, batch, seq)

    h2 = _outproj(o_fox, o_swa, h1, row(out_fox_g), row(_swa_reorder(out_swa_g, 0)), w_out_b)
    return _ffn(h2, row(norm_ffn2_g), w_gate2_b, w_up2_b, w_down2_b)[0]


def kernel(x, positions, norm_ffn1_g, ffn1_w_gate, ffn1_w_up, ffn1_w_down, norm_mix_g, w_in, b_forget, fox_q_norm_g, fox_k_norm_g, swa_q_norm_g, swa_k_norm_g, swa_sinks, out_norm_fox_g, out_norm_swa_g, w_out, norm_ffn2_g, ffn2_w_gate, ffn2_w_up, ffn2_w_down):
    batch, seq, d = x.shape
    half = jnp.arange(0, HEAD_DIM, 2, dtype=F32)
    inv_freq = ROPE_THETA ** (-half / HEAD_DIM)
    invf = jnp.tile(inv_freq, LANES // (HEAD_DIM // 2)).reshape(1, LANES)
    stacks = (norm_ffn1_g, ffn1_w_gate, ffn1_w_up, ffn1_w_down, norm_mix_g, b_forget, fox_q_norm_g,
              fox_k_norm_g, swa_q_norm_g, swa_k_norm_g, swa_sinks, out_norm_fox_g, out_norm_swa_g, w_out,
              norm_ffn2_g, ffn2_w_gate, ffn2_w_up, ffn2_w_down)
    h = x.reshape(batch * seq, d)
    for layer in range(norm_ffn1_g.shape[0]):
        h = _layer(h, positions, invf, tuple(s[layer] for s in stacks), w_in, layer)
    return h.reshape(batch, seq, d)
```
